```python
import math
import jax, jax.numpy as jnp
from jax import lax
import numpy as np

D_MODEL = 2048
BATCH = 2
SEQ = 4096
DEPTH = 1
DEC_BATCH = 32
DEC_SEQ = 8
PAST_LEN = 16384
PAGE_SIZE = 128

H_A = D_MODEL // 256
HKV_A = H_A // 2
R_A = H_A // HKV_A
DH_A = 64
W_A = H_A * 2 * DH_A
H_B = D_MODEL // 128
G_B = 4
R_B = H_B // G_B
DH_B = 64
W_B = H_B * DH_B
CMP_BLOCK = 32
SEL_BLOCK = 64
TOP_N = 16
WINDOW = 512
N_NSA_KV = 6
Q_BLOCK = 128
D_FF = 4 * D_MODEL
EPS = 1e-6
NEG_INF = -1e30
FORCE_SCORE = 1e4
IN_SIZES = (H_A * 2 * DH_A, HKV_A * 2 * DH_A, HKV_A * 2 * DH_A, H_B * DH_B, N_NSA_KV * G_B * DH_B, 3 * H_B, 2 * D_MODEL)
D_IN = 1024 + 512 + 512 + 1024 + 1536 + 48 + 2 * D_MODEL - 4096 + 4096 - 4096 + 4096 + (H_A * 2 * DH_A + 2 * HKV_A * 2 * DH_A + H_B * DH_B + N_NSA_KV * G_B * DH_B + 3 * H_B) - (1024 + 512 + 512 + 1024 + 1536 + 48)

kernel_name = 'hybrid_diffattn_nsa_gated_decode_step'


def rms_norm(x, g):
    xf = x.astype(jnp.float32)
    y = xf * lax.rsqrt(jnp.mean(xf * xf, axis=-1, keepdims=True) + EPS)
    return (y * g.astype(jnp.float32)).astype(x.dtype)


def alibi_slopes(n):
    return jnp.exp2(-8.0 * jnp.arange(1, n + 1, dtype=jnp.float32) / n)


def in_projection(u, w_in):
    z = u @ w_in
    B, T = u.shape[:2]
    offs = [0]
    for s in IN_SIZES:
        offs.append(offs[-1] + s)
    p = [z[..., offs[i]:offs[i + 1]] for i in range(len(IN_SIZES))]
    qa = p[0].reshape(B, T, HKV_A, R_A, 2, DH_A)
    ka = p[1].reshape(B, T, HKV_A, 2 * DH_A)
    va = p[2].reshape(B, T, HKV_A, 2 * DH_A)
    qb = p[3].reshape(B, T, G_B, R_B, DH_B)
    kvb = p[4].reshape(B, T, N_NSA_KV, G_B, DH_B)
    gb = p[5].reshape(B, T, 3, G_B, R_B)
    gm = p[6].reshape(B, T, 2, D_MODEL)
    return qa, ka, va, qb, kvb, gb, gm


def diff_lambda(lam_qk, lam_init):
    lf = lam_qk.astype(jnp.float32)
    return jnp.exp(jnp.sum(lf[0] * lf[1])) - jnp.exp(jnp.sum(lf[2] * lf[3])) + lam_init


def diff_attention(q, qpos, k, v, kpos, lam, gain, lam_init):
    s = jnp.einsum('bthrjd,bshjd->bhrjts', q, k).astype(jnp.float32) * (DH_A ** -0.5)
    dist = (qpos[:, None] - kpos[None, :]).astype(jnp.float32)
    slopes = alibi_slopes(H_A).reshape(HKV_A, R_A)
    s = s - slopes[None, :, :, None, None, None] * jnp.abs(dist)
    p = jax.nn.softmax(jnp.where(dist >= 0, s, NEG_INF), axis=-1)
    a = p[:, :, :, 0] - lam * p[:, :, :, 1]
    o = jnp.einsum('bhrts,bshe->bthre', a.astype(v.dtype), v)
    o = rms_norm(o, gain) * (1.0 - lam_init)
    return o.reshape(o.shape[0], o.shape[1], W_A)


def nsa_blocks(rows, w_cmp):
    B, L = rows.shape[:2]
    kc = jnp.einsum('bncgd,cg->bngd', rows[:, :, 0].reshape(B, L // CMP_BLOCK, CMP_BLOCK, G_B, DH_B), w_cmp[0])
    vc = jnp.einsum('bncgd,cg->bngd', rows[:, :, 1].reshape(B, L // CMP_BLOCK, CMP_BLOCK, G_B, DH_B), w_cmp[1])
    ks_t = rows[:, :, 2].reshape(B, L // SEL_BLOCK, SEL_BLOCK, G_B, DH_B).transpose(0, 3, 1, 2, 4)
    vs_t = rows[:, :, 3].reshape(B, L // SEL_BLOCK, SEL_BLOCK, G_B, DH_B).transpose(0, 3, 1, 2, 4)
    return kc, vc, ks_t, vs_t


def nsa_cmp_sel(q, qpos, kc, vc, ks_t, vs_t):
    B, Tq = q.shape[:2]
    NC = kc.shape[1]
    NS = ks_t.shape[2]
    scale = DH_B ** -0.5
    slopes = alibi_slopes(H_B).reshape(G_B, R_B)
    cidx = jnp.arange(NC)
    c_end = (cidx + 1) * CMP_BLOCK - 1
    c_mid = (cidx * CMP_BLOCK).astype(jnp.float32) + (CMP_BLOCK - 1) * 0.5
    s = jnp.einsum('btgrd,bngd->btgrn', q, kc).astype(jnp.float32) * scale
    dist_c = jnp.abs(qpos[:, None].astype(jnp.float32) - c_mid[None, :])
    s = s - slopes[None, None, :, :, None] * dist_c[None, :, None, None, :]
    valid_c = (c_end[None, :] <= qpos[:, None])[None, :, None, None, :]
    p = jnp.where(valid_c, jax.nn.softmax(jnp.where(valid_c, s, NEG_INF), axis=-1), 0.0)
    o_cmp = jnp.einsum('btgrn,bngd->btgrd', p.astype(vc.dtype), vc)
    imp = p.sum(axis=3).reshape(B, Tq, G_B, NS, SEL_BLOCK // CMP_BLOCK).sum(-1)
    sidx = jnp.arange(NS)
    forced = (sidx[None, :] == (qpos // SEL_BLOCK)[:, None]) | (sidx[None, :] == 0)
    valid_s = sidx[None, :] * SEL_BLOCK <= qpos[:, None]
    score = jnp.where(forced[None, :, None, :], FORCE_SCORE, jnp.where(valid_s[None, :, None, :], imp, -1.0))
    _, idx = lax.top_k(score, min(TOP_N, NS))
    bi = jnp.arange(B)[:, None, None, None]
    gi = jnp.arange(G_B)[None, None, :, None]
    kg = ks_t[bi, gi, idx]
    vg = vs_t[bi, gi, idx]
    kpos = idx[..., None] * SEL_BLOCK + jnp.arange(SEL_BLOCK)
    dist_s = qpos[None, :, None, None, None] - kpos
    s2 = jnp.einsum('btgrd,btgnsd->btgrns', q, kg).astype(jnp.float32) * scale
    s2 = s2 - slopes[None, None, :, :, None, None] * jnp.abs(dist_s).astype(jnp.float32)[:, :, :, None]
    s2 = jnp.where((dist_s >= 0)[:, :, :, None], s2, NEG_INF)
    n_sel = idx.shape[-1]
    p2 = jax.nn.softmax(s2.reshape(B, Tq, G_B, R_B, n_sel * SEL_BLOCK), axis=-1)
    p2 = p2.reshape(B, Tq, G_B, R_B, n_sel, SEL_BLOCK)
    o_sel = jnp.einsum('btgrns,btgnsd->btgrd', p2.astype(vg.dtype), vg)
    return o_cmp.astype(q.dtype), o_sel.astype(q.dtype)


def window_attend(q, qpos, k, v, kpos):
    slopes = alibi_slopes(H_B).reshape(G_B, R_B)
    s = jnp.einsum('btgrd,bsgd->bgrts', q, k).astype(jnp.float32) * (DH_B ** -0.5)
    dist = qpos[:, None] - kpos[None, :]
    valid = (dist >= 0) & (dist < WINDOW) & (kpos[None, :] >= 0)
    s = s - slopes[:, :, None, None] * dist.astype(jnp.float32)
    p = jax.nn.softmax(jnp.where(valid, s, NEG_INF), axis=-1)
    return jnp.einsum('bgrts,bsgd->btgrd', p.astype(v.dtype), v).astype(q.dtype)


def window_prompt(qb, kv_win):
    B, T = qb.shape[:2]
    nb = T // Q_BLOCK
    span = Q_BLOCK + WINDOW
    kp = jnp.pad(kv_win, ((0, 0), (WINDOW, 0), (0, 0), (0, 0), (0, 0)))
    idx = jnp.arange(nb)[:, None] * Q_BLOCK + jnp.arange(span)[None, :]
    kb = kp[:, idx]
    q_blk = qb.reshape(B, nb, Q_BLOCK, G_B, R_B, DH_B)
    qpos = jnp.arange(T).reshape(nb, Q_BLOCK)
    o = jax.vmap(window_attend, in_axes=(1, 0, 1, 1, 0), out_axes=1)(q_blk, qpos, kb[:, :, :, 0], kb[:, :, :, 1], idx - WINDOW)
    return o.reshape(B, T, G_B, R_B, DH_B)


def nsa_gate_combine(gb, o_cmp, o_sel, o_win):
    B, T = gb.shape[:2]
    g = jax.nn.sigmoid(gb)[..., None]
    o = g[:, :, 0] * o_cmp + g[:, :, 1] * o_sel + g[:, :, 2] * o_win
    return o.reshape(B, T, W_B)


def merge_branches(oa, ob, gm, w_pa, w_pb, w_o):
    g = jax.nn.sigmoid(gm)
    return (g[:, :, 0] * (oa @ w_pa) + g[:, :, 1] * (ob @ w_pb)) @ w_o


def squared_relu_mlp(u, w_up, w_down):
    return jnp.square(jax.nn.relu(u @ w_up)) @ w_down


def mixer_prompt(u, w_in, w_pa, w_pb, w_o, lam, dgain, wcmp, lam_init):
    B, T = u.shape[:2]
    qa, ka, va, qb, kvb, gb, gm = in_projection(u, w_in)
    nb = T // Q_BLOCK
    pos = jnp.arange(T)
    qpos = pos.reshape(nb, Q_BLOCK)
    k_a = ka.reshape(B, T, HKV_A, 2, DH_A)
    qa_blk = qa.reshape(B, nb, Q_BLOCK, HKV_A, R_A, 2, DH_A).swapaxes(0, 1)
    oa = lax.map(lambda a: diff_attention(a[0], a[1], k_a, va, pos, lam, dgain, lam_init), (qa_blk, qpos))
    oa = oa.swapaxes(0, 1).reshape(B, T, W_A)
    kc, vc, ks_t, vs_t = nsa_blocks(kvb[:, :, :4], wcmp)
    qb_blk = qb.reshape(B, nb, Q_BLOCK, G_B, R_B, DH_B).swapaxes(0, 1)
    o_cmp, o_sel = lax.map(lambda a: nsa_cmp_sel(a[0], a[1], kc, vc, ks_t, vs_t), (qb_blk, qpos))
    o_cmp = o_cmp.swapaxes(0, 1).reshape(B, T, G_B, R_B, DH_B)
    o_sel = o_sel.swapaxes(0, 1).reshape(B, T, G_B, R_B, DH_B)
    o_win = window_prompt(qb, kvb[:, :, 4:])
    ob = nsa_gate_combine(gb, o_cmp, o_sel, o_win)
    y = merge_branches(oa, ob, gm, w_pa, w_pb, w_o)
    keep = min(WINDOW, T)
    return y, jnp.stack([ka, va], axis=2), kvb[:, :, :4], kvb[:, T - keep:, 4:]


def diff_sample(qa, diff_new, qpos, cache_diff_kv, l, page_table, lam, dgain, lam_init):
    Bd = qa.shape[0]
    past_len = page_table.shape[1] * PAGE_SIZE
    past = cache_diff_kv[l, page_table].reshape(Bd, past_len, 2, HKV_A, 2 * DH_A)
    rows = jnp.concatenate([past, diff_new.astype(past.dtype)], axis=1)
    L = rows.shape[1]
    return diff_attention(qa, qpos, rows[:, :, 0].reshape(Bd, L, HKV_A, 2, DH_A), rows[:, :, 1], jnp.arange(L), lam, dgain, lam_init)


def nsa_sample(qb, nsa_new, qpos, cache_nsa_kv, l, page_table, wcmp):
    Bd, Tn = qb.shape[:2]
    past_len = page_table.shape[1] * PAGE_SIZE
    past = cache_nsa_kv[l, page_table].reshape(Bd, past_len, 4, G_B, DH_B)
    L = past_len + Tn
    L_pad = -(-L // SEL_BLOCK) * SEL_BLOCK
    pad = jnp.zeros((Bd, L_pad - L, 4, G_B, DH_B), past.dtype)
    rows = jnp.concatenate([past, nsa_new.astype(past.dtype), pad], axis=1)
    kc, vc, ks_t, vs_t = nsa_blocks(rows, wcmp)
    return nsa_cmp_sel(qb, qpos, kc, vc, ks_t, vs_t)


def window_sample(qb, win_new, qpos, state_win_kv, l, past_len):
    buf = state_win_kv[l]
    rows = jnp.concatenate([buf, win_new.astype(buf.dtype)], axis=1)
    n = rows.shape[1]
    kpos = past_len - buf.shape[1] + jnp.arange(n)
    o = window_attend(qb, qpos, rows[:, :, 0], rows[:, :, 1], kpos)
    keep = min(WINDOW, n)
    return o, rows[:, n - keep:]


def mixer_sample(u, cache_diff_kv, cache_nsa_kv, state_win_kv, page_table, l, w_in, w_pa, w_pb, w_o, lam, dgain, wcmp, lam_init):
    Tn = u.shape[1]
    past_len = page_table.shape[1] * PAGE_SIZE
    qa, ka, va, qb, kvb, gb, gm = in_projection(u, w_in)
    qpos = past_len + jnp.arange(Tn)
    diff_new = jnp.stack([ka, va], axis=2)
    oa = diff_sample(qa, diff_new, qpos, cache_diff_kv, l, page_table, lam, dgain, lam_init)
    o_cmp, o_sel = nsa_sample(qb, kvb[:, :, :4], qpos, cache_nsa_kv, l, page_table, wcmp)
    o_win, win_state = window_sample(qb, kvb[:, :, 4:], qpos, state_win_kv, l, past_len)
    ob = nsa_gate_combine(gb, o_cmp, o_sel, o_win)
    y = merge_branches(oa, ob, gm, w_pa, w_pb, w_o)
    return y, diff_new, kvb[:, :, :4], win_state


def setup_inputs(seed: int = 0) -> dict:
    key = jax.random.key(seed)
    ks = jax.random.split(key, 20)
    f32 = jnp.float32
    n_pages = PAST_LEN // PAGE_SIZE
    n_pool = (DEC_BATCH * n_pages * 5) // 4
    win_buf = min(WINDOW, PAST_LEN)

    def nrm(k, shape, scale):
        return jax.random.normal(k, shape, f32) * scale

    x_prompt = nrm(ks[0], (BATCH, SEQ, D_MODEL), 1.0)
    x_sample = nrm(ks[1], (DEC_BATCH, DEC_SEQ, D_MODEL), 1.0)
    cache_diff_kv = nrm(ks[2], (DEPTH, n_pool, PAGE_SIZE, 2, HKV_A, 2 * DH_A), 1.0)
    cache_nsa_kv = nrm(ks[3], (DEPTH, n_pool, PAGE_SIZE, 4, G_B, DH_B), 1.0)
    state_win_kv = nrm(ks[4], (DEPTH, DEC_BATCH, win_buf, 2, G_B, DH_B), 1.0)
    page_table = jax.random.permutation(ks[5], n_pool)[:DEC_BATCH * n_pages].reshape(DEC_BATCH, n_pages).astype(jnp.int32)
    d_in = sum(IN_SIZES)
    w_in = nrm(ks[6], (DEPTH, D_MODEL, d_in), D_MODEL ** -0.5)
    w_proj_a = nrm(ks[7], (DEPTH, W_A, D_MODEL), W_A ** -0.5)
    w_proj_b = nrm(ks[8], (DEPTH, W_B, D_MODEL), W_B ** -0.5)
    w_out = nrm(ks[9], (DEPTH, D_MODEL, D_MODEL), D_MODEL ** -0.5)
    lambda_qk = nrm(ks[10], (DEPTH, 4, DH_A), 0.1)
    diff_gain = 1.0 + nrm(ks[11], (DEPTH, 2 * DH_A), 0.02)
    w_cmp = (1.0 + nrm(ks[12], (DEPTH, 2, CMP_BLOCK, G_B), 0.1)) / CMP_BLOCK
    norm_attn = 1.0 + nrm(ks[13], (DEPTH, D_MODEL), 0.02)
    norm_mlp = 1.0 + nrm(ks[14], (DEPTH, D_MODEL), 0.02)
    w_up = nrm(ks[15], (DEPTH, D_MODEL, D_FF), D_MODEL ** -0.5)
    w_down = nrm(ks[16], (DEPTH, D_FF, D_MODEL), 0.5 * D_FF ** -0.5)
    norm_final = 1.0 + nrm(ks[17], (D_MODEL,), 0.02)
    return {'x_prompt': x_prompt, 'x_sample': x_sample, 'cache_diff_kv': cache_diff_kv,
            'cache_nsa_kv': cache_nsa_kv, 'state_win_kv': state_win_kv, 'page_table': page_table,
            'w_in': w_in, 'w_proj_a': w_proj_a, 'w_proj_b': w_proj_b, 'w_out': w_out,
            'lambda_qk': lambda_qk, 'diff_gain': diff_gain, 'w_cmp': w_cmp,
            'norm_attn': norm_attn, 'norm_mlp': norm_mlp, 'w_up': w_up, 'w_down': w_down,
            'norm_final': norm_final}


def reference(x_prompt, x_sample, cache_diff_kv, cache_nsa_kv, state_win_kv, page_table, w_in, w_proj_a, w_proj_b, w_out, lambda_qk, diff_gain, w_cmp, norm_attn, norm_mlp, w_up, w_down, norm_final):
    hp, hs = x_prompt, x_sample
    dkv_p, nkv_p, wkv_p, dkv_s, nkv_s, wkv_s = [], [], [], [], [], []
    for l in range(DEPTH):
        lam_init = 0.8 - 0.6 * math.exp(-0.3 * l)
        lam = diff_lambda(lambda_qk[l], lam_init)
        a, dkv, nkv, wkv = mixer_prompt(rms_norm(hp, norm_attn[l]), w_in[l], w_proj_a[l], w_proj_b[l], w_out[l], lam, diff_gain[l], w_cmp[l], lam_init)
        hp = hp + a
        hp = hp + squared_relu_mlp(rms_norm(hp, norm_mlp[l]), w_up[l], w_down[l])
        dkv_p.append(dkv)
        nkv_p.append(nkv)
        wkv_p.append(wkv)
        a, dkv, nkv, wkv = mixer_sample(rms_norm(hs, norm_attn[l]), cache_diff_kv, cache_nsa_kv, state_win_kv, page_table, l, w_in[l], w_proj_a[l], w_proj_b[l], w_out[l], lam, diff_gain[l], w_cmp[l], lam_init)
        hs = hs + a
        hs = hs + squared_relu_mlp(rms_norm(hs, norm_mlp[l]), w_up[l], w_down[l])
        dkv_s.append(dkv)
        nkv_s.append(nkv)
        wkv_s.append(wkv)
    y_prompt = rms_norm(hp, norm_final)
    y_sample = rms_norm(hs, norm_final)
    return (y_prompt, y_sample, jnp.stack(dkv_p), jnp.stack(nkv_p), jnp.stack(wkv_p), jnp.stack(dkv_s), jnp.stack(nkv_s), jnp.stack(wkv_s))
```

```python
import functools
import math

import numpy as np
import jax
import jax.numpy as jnp
from jax import lax
from jax.experimental import pallas as pl
from jax.experimental.pallas import tpu as pltpu

F32, BF16, I32 = jnp.float32, jnp.bfloat16, jnp.int32

PAGE_SIZE = 128
H_A, HKV_A, R_A, DH_A = 8, 4, 2, 64
W_A = H_A * 2 * DH_A
H_B, G_B, R_B, DH_B = 16, 4, 4, 64
W_B = H_B * DH_B
CMP_BLOCK, SEL_BLOCK, TOP_N, WINDOW = 32, 64, 16, 512
EPS = 1e-6
NEG_INF = -1e30
FORCE_SCORE = 1e4
LANES = 128
HALF = LANES // 2
VMEM_LIMIT = 56 * 1024 * 1024


def _iota(shape, dim):
    return lax.broadcasted_iota(I32, shape, dim)


def _log2(n):
    assert n & (n - 1) == 0, n
    return n.bit_length() - 1


def _nt_dot(a, b):
    return lax.dot_general(a, b, (((1,), (1,)), ((), ())), preferred_element_type=F32)


def _params(sem):
    return pltpu.CompilerParams(dimension_semantics=sem, vmem_limit_bytes=VMEM_LIMIT)


def _tri_tiles(n):
    ii = np.array([i for i in range(n) for _ in range(i + 1)], np.int32)
    jj = np.array([j for i in range(n) for j in range(i + 1)], np.int32)
    return jnp.asarray(ii), jnp.asarray(jj)


def _rmsnorm_kernel(x_ref, g_ref, o_ref):
    x = x_ref[...]
    y = x * lax.rsqrt(jnp.mean(x * x, axis=-1, keepdims=True) + EPS)
    o_ref[...] = (y * g_ref[...]).astype(o_ref.dtype)


def _rmsnorm(x, g, out_dtype, tm):
    m, d = x.shape
    return pl.pallas_call(
        _rmsnorm_kernel,
        grid=(m // tm,),
        in_specs=[pl.BlockSpec((tm, d), lambda i: (i, 0)),
                  pl.BlockSpec((1, d), lambda i: (0, 0))],
        out_specs=pl.BlockSpec((tm, d), lambda i: (i, 0)),
        out_shape=jax.ShapeDtypeStruct((m, d), out_dtype),
        compiler_params=_params(("parallel",)),
        name="rmsnorm",
    )(x, g.reshape(1, d))


def _mm_kernel(a_ref, w_ref, *o_refs):
    acc = jnp.dot(a_ref[...], w_ref[...], preferred_element_type=F32)
    for o in o_refs:
        o[...] = acc.astype(o.dtype)


def _matmul(a, w, out_dtypes, tm, tn, name):
    m, k = a.shape
    n = w.shape[1]
    tm, tn = min(tm, m), min(tn, n)
    outs = pl.pallas_call(
        _mm_kernel,
        grid=(n // tn, m // tm),
        in_specs=[pl.BlockSpec((tm, k), lambda j, i: (i, 0)),
                  pl.BlockSpec((k, tn), lambda j, i: (0, j))],
        out_specs=[pl.BlockSpec((tm, tn), lambda j, i: (i, j)) for _ in out_dtypes],
        out_shape=[jax.ShapeDtypeStruct((m, n), dt) for dt in out_dtypes],
        compiler_params=_params(("parallel", "parallel")),
        name=name,
    )(a, w)
    return outs


def _post_kernel(oa_ref, o1_ref, o2_ref, o3_ref, gm0_ref, gm1_ref, x_ref,
                 wpa_ref, wpb_ref, wo_ref, g_ref, h_ref, u_ref):
    oa = oa_ref[...].astype(BF16)
    ob = (o1_ref[...] + o2_ref[...] + o3_ref[...]).astype(BF16)
    pa = jnp.dot(oa, wpa_ref[...], preferred_element_type=F32)
    pb = jnp.dot(ob, wpb_ref[...], preferred_element_type=F32)
    mix = jax.nn.sigmoid(gm0_ref[...]) * pa + jax.nn.sigmoid(gm1_ref[...]) * pb
    y = jnp.dot(mix.astype(BF16), wo_ref[...], preferred_element_type=F32)
    h = x_ref[...] + y
    h_ref[...] = h
    u = h * lax.rsqrt(jnp.mean(h * h, axis=-1, keepdims=True) + EPS)
    u_ref[...] = (u * g_ref[...]).astype(u_ref.dtype)


def _post_attention(oa, o1, o2, o3, gm, x, wpa, wpb, wo, g_mlp, tm):
    m, d = x.shape
    tm = min(tm, m)
    row = lambda i: (i, 0)
    const = lambda i: (0, 0)
    once = pl.Buffered(1)
    return pl.pallas_call(
        _post_kernel,
        grid=(m // tm,),
        in_specs=[pl.BlockSpec((tm, W_A), row),
                  pl.BlockSpec((tm, W_B), row),
                  pl.BlockSpec((tm, W_B), row),
                  pl.BlockSpec((tm, W_B), row),
                  pl.BlockSpec((tm, d), lambda i: (i, 0)),
                  pl.BlockSpec((tm, d), lambda i: (i, 1)),
                  pl.BlockSpec((tm, d), row),
                  pl.BlockSpec((W_A, d), const, pipeline_mode=once),
                  pl.BlockSpec((W_B, d), const, pipeline_mode=once),
                  pl.BlockSpec((d, d), const, pipeline_mode=once),
                  pl.BlockSpec((1, d), const)],
        out_specs=[pl.BlockSpec((tm, d), row), pl.BlockSpec((tm, d), row)],
        out_shape=[jax.ShapeDtypeStruct((m, d), F32), jax.ShapeDtypeStruct((m, d), BF16)],
        compiler_params=_params(("parallel",)),
        name="post_attention",
    )(oa, o1, o2, o3, gm, gm, x, wpa, wpb, wo, g_mlp.reshape(1, d))


def _mlp_kernel(u_ref, h_ref, wup_ref, wdn_ref, g_ref, o_ref, acc_ref):
    f = pl.program_id(1)

    @pl.when(f == 0)
    def _():
        acc_ref[...] = jnp.zeros_like(acc_ref)

    a = jnp.dot(u_ref[...], wup_ref[...], preferred_element_type=F32)
    a = jnp.square(jnp.maximum(a, 0.0)).astype(BF16)
    acc_ref[...] += jnp.dot(a, wdn_ref[...], preferred_element_type=F32)

    @pl.when(f == pl.num_programs(1) - 1)
    def _():
        y = h_ref[...] + acc_ref[...]
        y = y * lax.rsqrt(jnp.mean(y * y, axis=-1, keepdims=True) + EPS)
        o_ref[...] = y * g_ref[...]


def _mlp_final(u, h, wup, wdn, g_final, tm, tf):
    m, d = h.shape
    dff = wup.shape[1]
    tm, tf = min(tm, m), min(tf, dff)
    return pl.pallas_call(
        _mlp_kernel,
        grid=(m // tm, dff // tf),
        in_specs=[pl.BlockSpec((tm, d), lambda i, f: (i, 0)),
                  pl.BlockSpec((tm, d), lambda i, f: (i, 0)),
                  pl.BlockSpec((d, tf), lambda i, f: (0, f)),
                  pl.BlockSpec((tf, d), lambda i, f: (f, 0)),
                  pl.BlockSpec((1, d), lambda i, f: (0, 0))],
        out_specs=pl.BlockSpec((tm, d), lambda i, f: (i, 0)),
        out_shape=jax.ShapeDtypeStruct((m, d), F32),
        scratch_shapes=[pltpu.VMEM((tm, d), F32)],
        compiler_params=_params(("parallel", "arbitrary")),
        name="mlp_final",
    )(u, h, wup, wdn, g_final.reshape(1, d))


def _flash_init(m_ref, l_ref, acc_ref):
    m_ref[...] = jnp.full(m_ref.shape, NEG_INF, F32)
    l_ref[...] = jnp.zeros(l_ref.shape, F32)
    acc_ref[...] = jnp.zeros(acc_ref.shape, F32)


def _flash_update(s, v, m_ref, l_ref, acc_ref):
    m_prev = m_ref[...]
    m_new = jnp.maximum(m_prev, jnp.max(s, axis=-1, keepdims=True))
    alpha = jnp.exp(m_prev - m_new)
    p = jnp.exp(s - m_new)
    l_ref[...] = alpha * l_ref[...] + jnp.sum(p, axis=-1, keepdims=True)
    acc_ref[...] = alpha * acc_ref[...] + jnp.dot(p.astype(BF16), v, preferred_element_type=F32)
    m_ref[...] = m_new


def _diff_lambda(lam_ref, lam_init):
    lf = lam_ref[...]
    e1 = jnp.exp(jnp.sum(lf[0:1] * lf[1:2], axis=-1, keepdims=True))
    e2 = jnp.exp(jnp.sum(lf[2:3] * lf[3:4], axis=-1, keepdims=True))
    return e1 - e2 + lam_init


def _diff_finish(o1, o2, lam, gain, lam_init):
    o = o1 - lam * o2
    y = o * lax.rsqrt(jnp.mean(o * o, axis=-1, keepdims=True) + EPS)
    return (y * gain) * (1.0 - lam_init)


def _diff_prompt_kernel(ii_ref, jj_ref, q_ref, k_ref, v_ref, lam_ref, gain_ref, o_ref,
                        qs_ref, m_ref, l_ref, acc_ref, *, tq, lam_init):
    hkv = pl.program_id(1)
    t = pl.program_id(2)
    i = ii_ref[t]
    j = jj_ref[t]
    rows = 2 * R_A * tq

    @pl.when(j == 0)
    def _():
        upper = _iota((tq, LANES), 1) >= HALF
        for r in range(R_A):
            qv = q_ref[:, r * LANES:(r + 1) * LANES] * (DH_A ** -0.5)
            for c in range(2):
                qs_ref[(2 * r + c) * tq:(2 * r + c + 1) * tq, :] = jnp.where(upper == bool(c), qv, 0).astype(BF16)
        _flash_init(m_ref, l_ref, acc_ref)

    s = _nt_dot(qs_ref[...], k_ref[...])
    row = _iota((rows, 1), 0)
    head = hkv * R_A + (row >> _log2(2 * tq))
    slope = jnp.exp2(-8.0 * (head + 1).astype(F32) / H_A)
    qpos = i * tq + (row & (tq - 1))
    kpos = j * tq + _iota((1, tq), 1)
    dist = qpos - kpos
    s = s - slope * dist.astype(F32)
    s = jnp.where(dist >= 0, s, NEG_INF)
    _flash_update(s, v_ref[...], m_ref, l_ref, acc_ref)

    @pl.when(j == i)
    def _():
        lam = _diff_lambda(lam_ref, lam_init)
        for r in range(R_A):
            b0 = 2 * r * tq
            o1 = acc_ref[b0:b0 + tq, :] / l_ref[b0:b0 + tq, :]
            o2 = acc_ref[b0 + tq:b0 + 2 * tq, :] / l_ref[b0 + tq:b0 + 2 * tq, :]
            y = _diff_finish(o1, o2, lam, gain_ref[...], lam_init)
            o_ref[:, r * LANES:(r + 1) * LANES] = y.astype(o_ref.dtype)


def _diff_prompt(qa, kva, lam_qk, gain, b, t, tq, lam_init):
    nq = t // tq
    ii, jj = _tri_tiles(nq)
    kern = functools.partial(_diff_prompt_kernel, tq=tq, lam_init=lam_init)
    grid_spec = pltpu.PrefetchScalarGridSpec(
        num_scalar_prefetch=2,
        grid=(b, HKV_A, ii.shape[0]),
        in_specs=[pl.BlockSpec((tq, R_A * LANES), lambda bb, h, n, ii, jj: (bb * nq + ii[n], h)),
                  pl.BlockSpec((tq, LANES), lambda bb, h, n, ii, jj: (bb * nq + jj[n], h)),
                  pl.BlockSpec((tq, LANES), lambda bb, h, n, ii, jj: (bb * nq + jj[n], HKV_A + h)),
                  pl.BlockSpec((4, DH_A), lambda bb, h, n, ii, jj: (0, 0)),
                  pl.BlockSpec((1, 2 * DH_A), lambda bb, h, n, ii, jj: (0, 0))],
        out_specs=pl.BlockSpec((tq, R_A * LANES), lambda bb, h, n, ii, jj: (bb * nq + ii[n], h)),
        scratch_shapes=[pltpu.VMEM((2 * R_A * tq, LANES), BF16),
                        pltpu.VMEM((2 * R_A * tq, 1), F32),
                        pltpu.VMEM((2 * R_A * tq, 1), F32),
                        pltpu.VMEM((2 * R_A * tq, LANES), F32)])
    return pl.pallas_call(
        kern, grid_spec=grid_spec,
        out_shape=jax.ShapeDtypeStruct((b * t, W_A), BF16),
        compiler_params=_params(("parallel", "parallel", "arbitrary")),
        name="diff_prompt",
    )(ii, jj, qa, kva, kva, lam_qk, gain.reshape(1, 2 * DH_A))


def _dup_half(v, g):
    rolled = pltpu.roll(v, HALF, 1)
    low = _iota(v.shape, 1) < HALF
    return jnp.where(low, v, rolled) if g % 2 == 0 else jnp.where(low, rolled, v)


def _prep_kernel(nsa_ref, win_ref, wexp_ref, kcd_ref, vcd_ref, ksa_ref, vsd_ref, kwd_ref, vwd_ref,
                 tmp_ref, *, tp, nc):
    step = pl.program_id(1)
    gw = G_B * DH_B
    x = nsa_ref[...]
    w = win_ref[...]
    lane = _iota((tp, LANES), 1)
    low = lane < HALF
    blk = (step * tp + _iota((tp, LANES), 0)) >> _log2(SEL_BLOCK)
    onehot = jnp.where(lane - HALF == blk, 1.0, 0.0)
    for g in range(G_B):
        c0 = (g // 2) * LANES
        vk = x[:, 2 * gw + c0:2 * gw + c0 + LANES]
        klow = vk if g % 2 == 0 else pltpu.roll(vk, HALF, 1)
        ksa_ref[0, g] = jnp.where(low, klow, onehot).astype(BF16)
        vsd_ref[0, g] = _dup_half(x[:, 3 * gw + c0:3 * gw + c0 + LANES], g).astype(BF16)
        kwd_ref[0, g] = _dup_half(w[:, c0:c0 + LANES], g).astype(BF16)
        vwd_ref[0, g] = _dup_half(w[:, gw + c0:gw + c0 + LANES], g).astype(BF16)
    nb = tp // CMP_BLOCK
    hb = nb // 2
    for slot, dst in ((0, kcd_ref), (1, vcd_ref)):
        xc = x[:, slot * gw:(slot + 1) * gw]
        comp = jnp.sum(xc.reshape(nb, CMP_BLOCK, gw) * wexp_ref[slot][None], axis=1)
        for c in range(gw // LANES):
            tmp_ref[c * nb:(c + 1) * nb, :] = comp[:, c * LANES:(c + 1) * LANES]
        for par in range(2):
            start = pl.multiple_of(par * (nc // 2) + step * hb, 8)
            for g in range(G_B):
                rr = tmp_ref[pl.ds((g // 2) * nb + par, hb, stride=2), :]
                dst[0, g, pl.ds(start, hb), :] = _dup_half(rr, g)


def _nsa_prep(nsa, win, wexp, b, t, tp):
    nc = t // CMP_BLOCK
    gw = G_B * DH_B
    kern = functools.partial(_prep_kernel, tp=tp, nc=nc)
    small = jax.ShapeDtypeStruct((b, G_B, nc, LANES), F32)
    big = jax.ShapeDtypeStruct((b, G_B, t, LANES), BF16)
    small_spec = pl.BlockSpec((1, G_B, nc, LANES), lambda bb, s: (bb, 0, 0, 0))
    big_spec = pl.BlockSpec((1, G_B, tp, LANES), lambda bb, s: (bb, 0, s, 0))
    nt = t // tp
    return pl.pallas_call(
        kern, grid=(b, nt),
        in_specs=[pl.BlockSpec((tp, 4 * gw), lambda bb, s: (bb * nt + s, 0)),
                  pl.BlockSpec((tp, 2 * gw), lambda bb, s: (bb * nt + s, 0)),
                  pl.BlockSpec((2, CMP_BLOCK, gw), lambda bb, s: (0, 0, 0))],
        out_specs=[small_spec, small_spec, big_spec, big_spec, big_spec, big_spec],
        out_shape=[small, small, big, big, big, big],
        scratch_shapes=[pltpu.VMEM((gw // LANES * (tp // CMP_BLOCK), LANES), F32)],
        compiler_params=_params(("parallel", "arbitrary")),
        name="nsa_prep",
    )(nsa, win, wexp)


def _topk_bias(sc, sidx):
    rank = jnp.zeros(sc.shape, I32)
    for c in range(sc.shape[0]):
        other = sc[c:c + 1, :]
        beats = (other > sc) | ((other == sc) & (sidx > c))
        rank = rank + beats.astype(I32)
    return jnp.where(rank < TOP_N, 0.0, NEG_INF)


def _cmp_prompt_kernel(q_ref, kcd_ref, vcd_ref, gb_ref, ocmp_ref, qaug_ref, *, tq, nc):
    i = pl.program_id(1)
    ns = nc // 2
    lowq = _iota((tq, LANES), 1) < HALF
    rowc = _iota((nc, tq), 0)
    qpos = i * tq + _iota((nc, tq), 1)
    blk = jnp.where(rowc >= ns, 2 * (rowc - ns) + 1, 2 * rowc)
    c_mid = (blk * CMP_BLOCK).astype(F32) + (CMP_BLOCK - 1) * 0.5
    valid = blk * CMP_BLOCK + (CMP_BLOCK - 1) <= qpos
    adist = jnp.abs(qpos.astype(F32) - c_mid)
    sig = jax.nn.sigmoid(gb_ref[...])
    sidx = _iota((ns, tq), 0)
    qp = i * tq + _iota((ns, tq), 1)
    forced = (sidx == (qp >> _log2(SEL_BLOCK))) | (sidx == 0)
    valid_s = sidx * SEL_BLOCK <= qp
    for g in range(G_B):
        kc = kcd_ref[0, g].astype(BF16)
        vc = vcd_ref[0, g].astype(BF16)
        imp = jnp.zeros((nc, tq), F32)
        outs, qlows = [], []
        for r in range(R_B):
            h = g * R_B + r
            qv = q_ref[:, (h // 2) * LANES:(h // 2 + 1) * LANES] * (DH_B ** -0.5)
            qm = jnp.where(lowq == (h % 2 == 0), qv, 0).astype(BF16)
            s = _nt_dot(kc, qm)
            s = s - (2.0 ** (-8.0 * (h + 1) / H_B)) * adist
            sm = jnp.where(valid, s, NEG_INF)
            e = jnp.exp(sm - jnp.max(sm, axis=0, keepdims=True))
            p = jnp.where(valid, e / jnp.sum(e, axis=0, keepdims=True), 0.0)
            imp = imp + p
            o2 = jnp.dot(p.T.astype(BF16), vc, preferred_element_type=F32)
            outs.append(o2 * sig[:, h:h + 1])
            qlows.append(qv if h % 2 == 0 else pltpu.roll(qv, HALF, 1))
        for k in range(R_B // 2):
            c0 = (g * (R_B // 2) + k) * LANES
            ocmp_ref[:, c0:c0 + LANES] = jnp.where(lowq, outs[2 * k], outs[2 * k + 1])
        sc = jnp.where(forced, FORCE_SCORE, jnp.where(valid_s, imp[:ns] + imp[ns:], -1.0))
        bias_t = _topk_bias(sc, sidx)
        bias = jnp.concatenate([bias_t] * (LANES // ns), axis=0).T.astype(BF16)
        for r in range(R_B):
            qaug_ref[0, g, r] = jnp.where(lowq, qlows[r], bias).astype(BF16)


def _cmp_prompt(zq, kcd, vcd, gb, b, t, tq):
    nc = t // CMP_BLOCK
    nq = t // tq
    kern = functools.partial(_cmp_prompt_kernel, tq=tq, nc=nc)
    small_spec = pl.BlockSpec((1, G_B, nc, LANES), lambda bb, i: (bb, 0, 0, 0))
    return pl.pallas_call(
        kern, grid=(b, nq),
        in_specs=[pl.BlockSpec((tq, W_B), lambda bb, i: (bb * nq + i, W_A // W_B)),
                  small_spec, small_spec,
                  pl.BlockSpec((tq, LANES), lambda bb, i: (bb * nq + i, 0))],
        out_specs=[pl.BlockSpec((tq, W_B), lambda bb, i: (bb * nq + i, 0)),
                   pl.BlockSpec((1, G_B, R_B, tq, LANES), lambda bb, i: (bb, 0, 0, i, 0))],
        out_shape=[jax.ShapeDtypeStruct((b * t, W_B), F32),
                   jax.ShapeDtypeStruct((b, G_B, R_B, t, LANES), BF16)],
        compiler_params=_params(("parallel", "parallel")),
        name="cmp_prompt",
    )(zq, kcd, vcd, gb)


def _group_slopes(g, rows, tq):
    row = _iota((rows, 1), 0)
    head = g * R_B + (row >> _log2(tq))
    return row, jnp.exp2(-8.0 * (head + 1).astype(F32) / H_B)


def _gate_column(sig, col):
    return jnp.sum(jnp.where(_iota(sig.shape, 1) == col, sig, 0.0), axis=-1, keepdims=True)


def _group_finish(o_ref, gb_ref, l_ref, acc_ref, g, branch, tq):
    sig = jax.nn.sigmoid(gb_ref[...])
    low = _iota((tq, LANES), 1) < HALF
    outs = []
    for r in range(R_B):
        o = acc_ref[r * tq:(r + 1) * tq, :] / l_ref[r * tq:(r + 1) * tq, :]
        outs.append(o * _gate_column(sig, branch * H_B + g * R_B + r))
    for k in range(R_B // 2):
        o_ref[:, k * LANES:(k + 1) * LANES] = jnp.where(low, outs[2 * k], outs[2 * k + 1])


def _sel_prompt_kernel(ii_ref, jj_ref, qa_ref, k_ref, v_ref, gb_ref, o_ref,
                       m_ref, l_ref, acc_ref, *, tq):
    g = pl.program_id(1)
    t = pl.program_id(2)
    i = ii_ref[t]
    j = jj_ref[t]
    rows = R_B * tq

    @pl.when(j == 0)
    def _():
        _flash_init(m_ref, l_ref, acc_ref)

    s = _nt_dot(qa_ref[0, 0].reshape(rows, LANES), k_ref[0, 0])
    row, slope = _group_slopes(g, rows, tq)
    dist = (i * tq + (row & (tq - 1))) - (j * tq + _iota((1, tq), 1))
    s = s - slope * dist.astype(F32)
    s = jnp.where(dist >= 0, s, NEG_INF)
    _flash_update(s, v_ref[0, 0], m_ref, l_ref, acc_ref)

    @pl.when(j == i)
    def _():
        _group_finish(o_ref, gb_ref, l_ref, acc_ref, g, 1, tq)


def _sel_prompt(qaug, ksa, vsd, gb, b, t, tq):
    nq = t // tq
    ii, jj = _tri_tiles(nq)
    kern = functools.partial(_sel_prompt_kernel, tq=tq)
    kv_spec = pl.BlockSpec((1, 1, tq, LANES), lambda bb, g, n, ii, jj: (bb, g, jj[n], 0))
    grid_spec = pltpu.PrefetchScalarGridSpec(
        num_scalar_prefetch=2,
        grid=(b, G_B, ii.shape[0]),
        in_specs=[pl.BlockSpec((1, 1, R_B, tq, LANES), lambda bb, g, n, ii, jj: (bb, g, 0, ii[n], 0)),
                  kv_spec, kv_spec,
                  pl.BlockSpec((tq, LANES), lambda bb, g, n, ii, jj: (bb * nq + ii[n], 0))],
        out_specs=pl.BlockSpec((tq, R_B * DH_B), lambda bb, g, n, ii, jj: (bb * nq + ii[n], g)),
        scratch_shapes=[pltpu.VMEM((R_B * tq, 1), F32),
                        pltpu.VMEM((R_B * tq, 1), F32),
                        pltpu.VMEM((R_B * tq, LANES), F32)])
    return pl.pallas_call(
        kern, grid_spec=grid_spec,
        out_shape=jax.ShapeDtypeStruct((b * t, W_B), F32),
        compiler_params=_params(("parallel", "parallel", "arbitrary")),
        name="sel_prompt",
    )(ii, jj, qaug, ksa, vsd, gb)


def _win_prompt_kernel(q_ref, k_ref, v_ref, gb_ref, o_ref, qs_ref, m_ref, l_ref, acc_ref, *, tq, nj):
    g = pl.program_id(1)
    i = pl.program_id(2)
    j = pl.program_id(3)
    rows = R_B * tq

    @pl.when(j == 0)
    def _():
        low = _iota((tq, LANES), 1) < HALF
        for r in range(R_B):
            qv = q_ref[:, (r // 2) * LANES:(r // 2 + 1) * LANES] * (DH_B ** -0.5)
            qs_ref[r * tq:(r + 1) * tq, :] = jnp.where(low == (r % 2 == 0), qv, 0).astype(BF16)
        _flash_init(m_ref, l_ref, acc_ref)

    kt = i - (nj - 1) + j

    @pl.when(kt >= 0)
    def _():
        s = _nt_dot(qs_ref[...], k_ref[0, 0])
        row, slope = _group_slopes(g, rows, tq)
        dist = (i * tq + (row & (tq - 1))) - (kt * tq + _iota((1, tq), 1))
        s = s - slope * dist.astype(F32)
        s = jnp.where((dist >= 0) & (dist < WINDOW), s, NEG_INF)
        _flash_update(s, v_ref[0, 0], m_ref, l_ref, acc_ref)

    @pl.when(j == nj - 1)
    def _():
        _group_finish(o_ref, gb_ref, l_ref, acc_ref, g, 2, tq)


def _win_prompt(zq, kwd, vwd, gb, b, t, tq):
    nq = t // tq
    nj = -(-WINDOW // tq) + 1
    qoff = W_A // (R_B * DH_B)
    kern = functools.partial(_win_prompt_kernel, tq=tq, nj=nj)
    kv_spec = pl.BlockSpec((1, 1, tq, LANES),
                           lambda bb, g, i, j: (bb, g, jnp.maximum(i - (nj - 1) + j, 0), 0))
    return pl.pallas_call(
        kern, grid=(b, G_B, nq, nj),
        in_specs=[pl.BlockSpec((tq, R_B * DH_B), lambda bb, g, i, j: (bb * nq + i, qoff + g)),
                  kv_spec, kv_spec,
                  pl.BlockSpec((tq, LANES), lambda bb, g, i, j: (bb * nq + i, 0))],
        out_specs=pl.BlockSpec((tq, R_B * DH_B), lambda bb, g, i, j: (bb * nq + i, g)),
        out_shape=jax.ShapeDtypeStruct((b * t, W_B), F32),
        scratch_shapes=[pltpu.VMEM((R_B * tq, LANES), BF16),
                        pltpu.VMEM((R_B * tq, 1), F32),
                        pltpu.VMEM((R_B * tq, 1), F32),
                        pltpu.VMEM((R_B * tq, LANES), F32)],
        compiler_params=_params(("parallel", "parallel", "parallel", "arbitrary")),
        name="win_prompt",
    )(zq, kwd, vwd, gb)


def _split_w_in(w_in):
    sizes = (W_A, HKV_A * 2 * DH_A, HKV_A * 2 * DH_A, W_B, 6 * G_B * DH_B, 3 * H_B, 2 * w_in.shape[0])
    offs = np.concatenate([[0], np.cumsum(sizes)])
    seg = lambda a, b_: w_in[:, a:b_].astype(BF16)
    gw = G_B * DH_B
    return dict(
        q=jnp.concatenate([seg(offs[0], offs[1]), seg(offs[3], offs[4])], axis=1),
        kva=seg(offs[1], offs[3]),
        nsa=seg(offs[4], offs[4] + 4 * gw),
        win=seg(offs[4] + 4 * gw, offs[5]),
        gb=jnp.pad(seg(offs[5], offs[6]), ((0, 0), (0, LANES - 3 * H_B))),
        gm=seg(offs[6], offs[7]))


def _in_projection(u, w, tm, q_dtype=BF16):
    zq, = _matmul(u, w["q"], (q_dtype,), tm, 512, "proj_q")
    kva32, kva16 = _matmul(u, w["kva"], (F32, BF16), tm, 512, "proj_kva")
    nsa32, = _matmul(u, w["nsa"], (F32,), tm, 512, "proj_nsa")
    win32, = _matmul(u, w["win"], (F32,), tm, 512, "proj_win")
    gb, = _matmul(u, w["gb"], (F32,), tm, LANES, "proj_gb")
    gm, = _matmul(u, w["gm"], (F32,), tm, 512, "proj_gm")
    return zq, kva32, kva16, nsa32, win32, gb, gm


def _prompt_mixer(u, w, lam_qk, gain, wexp, b, t, lam_init, tq=256):
    tq = min(tq, t)
    zq, kva32, kva16, nsa32, win32, gb, gm = _in_projection(u, w, 1024)
    oa = _diff_prompt(zq, kva16, lam_qk, gain, b, t, tq, lam_init)
    kcd, vcd, ksa, vsd, kwd, vwd = _nsa_prep(nsa32, win32, wexp, b, t, min(512, t))
    ocmp, qaug = _cmp_prompt(zq, kcd, vcd, gb, b, t, tq)
    osel = _sel_prompt(qaug, ksa, vsd, gb, b, t, tq)
    owin = _win_prompt(zq, kwd, vwd, gb, b, t, tq)
    return (oa, ocmp, osel, owin, gm), (kva32, nsa32, win32)


def _pad_rows(x, rows):
    return jnp.concatenate([x, jnp.zeros((rows - x.shape[0], x.shape[1]), x.dtype)], axis=0)


def _diff_sample_kernel(pt_ref, q_ref, new_ref, lam_ref, gain_ref, *rest, tn, pps, past_len, lam_init):
    page_refs = rest[:pps]
    o_ref, qs_ref, m_ref, l_ref, acc_ref = rest[pps:]
    c = pl.program_id(1)
    rows = 2 * R_A * tn
    hw = HKV_A * 2 * DH_A
    row = _iota((rows, 1), 0)
    tq = row & (tn - 1)

    def slopes(hkv):
        head = hkv * R_A + (row >> _log2(2 * tn))
        return jnp.exp2(-8.0 * (head + 1).astype(F32) / H_A)

    def update(hkv, s, pv_fn):
        m_prev = m_ref[hkv]
        m_new = jnp.maximum(m_prev, jnp.max(s, axis=-1, keepdims=True))
        alpha = jnp.exp(m_prev - m_new)
        p = jnp.exp(s - m_new)
        l_ref[hkv] = alpha * l_ref[hkv] + jnp.sum(p, axis=-1, keepdims=True)
        acc_ref[hkv] = alpha * acc_ref[hkv] + pv_fn(p.astype(BF16))
        m_ref[hkv] = m_new

    @pl.when(c == 0)
    def _():
        upper = _iota((tn, LANES), 1) >= HALF
        for hkv in range(HKV_A):
            parts = []
            for r in range(R_A):
                h = hkv * R_A + r
                qv = q_ref[:, h * LANES:(h + 1) * LANES] * (DH_A ** -0.5)
                parts += [jnp.where(upper == bool(cc), qv, 0.0) for cc in range(2)]
            qs_ref[hkv] = jnp.concatenate(parts, axis=0).astype(BF16)
        _flash_init(m_ref, l_ref, acc_ref)

    kpos = c * (pps * PAGE_SIZE) + _iota((1, pps * PAGE_SIZE), 1)
    dist = ((past_len + tq) - kpos).astype(F32)
    for hkv in range(HKV_A):
        q = qs_ref[hkv]
        c0 = hkv * LANES
        s = jnp.concatenate(
            [_nt_dot(q, pr[0, :, c0:c0 + LANES].astype(BF16)) for pr in page_refs], axis=1)
        s = s - slopes(hkv) * dist

        def pv(p, c0=c0):
            out = None
            for n, pr in enumerate(page_refs):
                part = jnp.dot(p[:, n * PAGE_SIZE:(n + 1) * PAGE_SIZE],
                               pr[0, :, hw + c0:hw + c0 + LANES].astype(BF16), preferred_element_type=F32)
                out = part if out is None else out + part
            return out

        update(hkv, s, pv)

    @pl.when(c == pl.num_programs(1) - 1)
    def _():
        lam = _diff_lambda(lam_ref, lam_init)
        newp = _pad_rows(new_ref[...], LANES)
        dnew = tq - _iota((1, LANES), 1)
        for hkv in range(HKV_A):
            c0 = hkv * LANES
            s = _nt_dot(qs_ref[hkv], newp[:, c0:c0 + LANES].astype(BF16))
            s = jnp.where(dnew >= 0, s - slopes(hkv) * dnew.astype(F32), NEG_INF)
            vn = newp[:, hw + c0:hw + c0 + LANES].astype(BF16)
            update(hkv, s, lambda p, vn=vn: jnp.dot(p, vn, preferred_element_type=F32))
            o = acc_ref[hkv] / l_ref[hkv]
            for r in range(R_A):
                b0 = 2 * r * tn
                y = _diff_finish(o[b0:b0 + tn], o[b0 + tn:b0 + 2 * tn], lam, gain_ref[...], lam_init)
                h = hkv * R_A + r
                o_ref[:, h * LANES:(h + 1) * LANES] = y


def _page_specs(pps, n_pages, width, col):
    def spec(p):
        return pl.BlockSpec((1, PAGE_SIZE, width),
                            lambda bb, c, pt, p=p: (pt[bb * n_pages + c * pps + p], 0, col))
    return [spec(p) for p in range(pps)]


def _diff_sample(zq, kva_new, cache, pt, lam_qk, gain, bd, tn, n_pages, pps, lam_init):
    rows = 2 * R_A * tn
    kern = functools.partial(_diff_sample_kernel, tn=tn, pps=pps, past_len=n_pages * PAGE_SIZE,
                             lam_init=lam_init)
    grid_spec = pltpu.PrefetchScalarGridSpec(
        num_scalar_prefetch=1,
        grid=(bd, n_pages // pps),
        in_specs=[pl.BlockSpec((tn, W_A), lambda bb, c, pt: (bb, 0)),
                  pl.BlockSpec((tn, W_A), lambda bb, c, pt: (bb, 0)),
                  pl.BlockSpec((4, DH_A), lambda bb, c, pt: (0, 0)),
                  pl.BlockSpec((1, 2 * DH_A), lambda bb, c, pt: (0, 0))]
                 + _page_specs(pps, n_pages, 2 * HKV_A * 2 * DH_A, 0),
        out_specs=pl.BlockSpec((tn, W_A), lambda bb, c, pt: (bb, 0)),
        scratch_shapes=[pltpu.VMEM((HKV_A, rows, LANES), BF16),
                        pltpu.VMEM((HKV_A, rows, 1), F32),
                        pltpu.VMEM((HKV_A, rows, 1), F32),
                        pltpu.VMEM((HKV_A, rows, LANES), F32)])
    return pl.pallas_call(
        kern, grid_spec=grid_spec,
        out_shape=jax.ShapeDtypeStruct((bd * tn, W_A), F32),
        compiler_params=_params(("parallel", "arbitrary")),
        name="diff_sample",
    )(pt, zq, kva_new, lam_qk, gain.reshape(1, 2 * DH_A), *([cache] * pps))


def _sample_queries(qblk, tn):
    low = _iota((tn, LANES), 1) < HALF
    zero = jnp.zeros((tn, LANES), F32)
    mats = ([], [])
    for r in range(R_B):
        for g in range(G_B):
            h = g * R_B + r
            qv = qblk[:, (h // 2) * LANES:(h // 2 + 1) * LANES] * (DH_B ** -0.5)
            if h % 2 != g % 2:
                qv = pltpu.roll(qv, HALF, 1)
            mats[g // 2].append(jnp.where(low == (g % 2 == 0), qv, 0.0))
            mats[1 - g // 2].append(zero)
    return [jnp.concatenate(m, axis=0).astype(BF16) for m in mats]


def _sample_outputs(o_ref, o_pairs, sig, branch, tn):
    low = _iota((tn, LANES), 1) < HALF
    for g in range(G_B):
        for k in range(R_B // 2):
            parts = []
            for r in (2 * k, 2 * k + 1):
                r0 = (r * G_B + g) * tn
                src = o_pairs[g // 2][r0:r0 + tn, :]
                if g % 2 != r % 2:
                    src = pltpu.roll(src, HALF, 1)
                col = branch * H_B + g * R_B + r
                parts.append(src * sig[:, col:col + 1])
            c0 = (g * (R_B // 2) + k) * LANES
            o_ref[:, c0:c0 + LANES] = jnp.where(low, parts[0], parts[1])


def _lane_query_consts(tn, past_len):
    lane = _iota((1, LANES), 1)
    r_l = lane >> _log2(G_B * tn)
    g_l = (lane >> _log2(tn)) & (G_B - 1)
    slope = jnp.exp2(-8.0 * (g_l * R_B + r_l + 1).astype(F32) / H_B)
    return g_l >> 1, slope, past_len + (lane & (tn - 1))


def _nsa_sample_kernel(pt_ref, q_ref, new_ref, gb_ref, wexp_ref, *rest, tn, pps, past_len):
    page_refs = rest[:pps]
    (ocmp_ref, osel_ref, qop_ref, kcvc_ref, tmp_ref, sc_ref, selb_ref,
     m_ref, l_ref, acc_ref) = rest[pps:]
    ph = pl.program_id(1)
    c = pl.program_id(2)
    last = pl.num_programs(2) - 1
    gw = G_B * DH_B
    ncp = past_len // CMP_BLOCK
    nsp = past_len // SEL_BLOCK
    ck = pps * PAGE_SIZE
    pair_l, slope_l, qpos_l = _lane_query_consts(tn, past_len)

    def chunk():
        return jnp.concatenate([pr[0] for pr in page_refs], axis=0)

    def pair_scores(x):
        s0 = _nt_dot(x[:, 0:LANES].astype(BF16), qop_ref[0])
        s1 = _nt_dot(x[:, LANES:2 * LANES].astype(BF16), qop_ref[1])
        return jnp.where(pair_l == 0, s0, s1)

    def pair_values(p_t, x):
        p = p_t.T.astype(BF16)
        return [jnp.dot(p, x[:, gw + q * LANES:gw + (q + 1) * LANES].astype(BF16),
                        preferred_element_type=F32) for q in range(2)]

    def col_of(row_vec):
        return jnp.broadcast_to(row_vec, (LANES, LANES)).T

    @pl.when((ph == 0) & (c == 0))
    def _():
        qs = _sample_queries(q_ref[...], tn)
        qop_ref[0] = qs[0]
        qop_ref[1] = qs[1]

    @pl.when(ph == 0)
    def _():
        x = chunk()
        nb = ck // CMP_BLOCK
        hb = nb // 2
        wboth = jnp.concatenate([wexp_ref[0], wexp_ref[1]], axis=1)
        comp = jnp.sum(x.reshape(nb, CMP_BLOCK, 2 * gw) * wboth[None], axis=1)
        for band in range(2 * gw // LANES):
            tmp_ref[band * nb:(band + 1) * nb, :] = comp[:, band * LANES:(band + 1) * LANES]
        for par in range(2):
            start = pl.multiple_of(par * (ncp // 2) + c * hb, 8)
            for band in range(2 * gw // LANES):
                kcvc_ref[pl.ds(start, hb), band * LANES:(band + 1) * LANES] = (
                    tmp_ref[pl.ds(band * nb + par, hb, stride=2), :])

    @pl.when((ph == 0) & (c == last))
    def _():
        kcvc = kcvc_ref[...]
        s = pair_scores(kcvc)
        rowc = _iota((ncp, LANES), 0)
        blk = jnp.where(rowc >= ncp // 2, 2 * (rowc - ncp // 2) + 1, 2 * rowc)
        c_mid = (blk * CMP_BLOCK).astype(F32) + (CMP_BLOCK - 1) * 0.5
        valid = blk * CMP_BLOCK + (CMP_BLOCK - 1) <= qpos_l
        s = s - slope_l * jnp.abs(qpos_l.astype(F32) - c_mid)
        sm = jnp.where(valid, s, NEG_INF)
        e = jnp.exp(sm - jnp.max(sm, axis=0, keepdims=True))
        p_t = jnp.where(valid, e / jnp.sum(e, axis=0, keepdims=True), 0.0)
        sig = jax.nn.sigmoid(gb_ref[...])
        _sample_outputs(ocmp_ref, pair_values(p_t, kcvc), sig, 0, tn)
        imp = p_t
        for k in range(1, R_B):
            imp = imp + pltpu.roll(p_t, k * G_B * tn, 1)
        imp = imp[:ncp // 2] + imp[ncp // 2:]
        sidx = _iota((nsp, LANES), 0)
        forced = (sidx == (qpos_l >> _log2(SEL_BLOCK))) | (sidx == 0)
        sc_ref[0:nsp, :] = jnp.where(forced, FORCE_SCORE,
                                     jnp.where(sidx * SEL_BLOCK <= qpos_l, imp, -1.0))
        tail = sc_ref.shape[0] - nsp
        tidx = nsp + _iota((tail, LANES), 0)
        tforced = (tidx == (qpos_l >> _log2(SEL_BLOCK))) | (tidx == 0)
        tsc = jnp.where(tforced, FORCE_SCORE, jnp.where(tidx * SEL_BLOCK <= qpos_l, 0.0, -1.0))
        sc_ref[nsp:, :] = jnp.where(tidx == nsp, tsc, NEG_INF)
        sc = sc_ref[...]
        ridx = _iota(sc.shape, 0)

        def body(n, rank):
            other = sc_ref[pl.ds(n, 1), :]
            beats = (other > sc) | ((other == sc) & (ridx > n))
            return rank + beats.astype(I32)

        rank = lax.fori_loop(0, nsp + 1, body, jnp.zeros(sc.shape, I32))
        selb_ref[...] = jnp.where(rank < TOP_N, 0.0, NEG_INF)
        m_ref[...] = jnp.full(m_ref.shape, NEG_INF, F32)
        l_ref[...] = jnp.zeros(l_ref.shape, F32)
        acc_ref[...] = jnp.zeros(acc_ref.shape, F32)

    def sel_update(s, x):
        m_prev = m_ref[...]
        m_new = jnp.maximum(m_prev, jnp.max(s, axis=0, keepdims=True))
        alpha = jnp.exp(m_prev - m_new)
        p_t = jnp.exp(s - m_new)
        l_ref[...] = alpha * l_ref[...] + jnp.sum(p_t, axis=0, keepdims=True)
        pv = pair_values(p_t, x)
        a_col = col_of(alpha)
        for q in range(2):
            acc_ref[q] = a_col * acc_ref[q] + pv[q]
        m_ref[...] = m_new

    @pl.when(ph == 1)
    def _():
        x = chunk()
        s = pair_scores(x)
        nblk = ck // SEL_BLOCK
        bias = jnp.concatenate(
            [jnp.broadcast_to(selb_ref[pl.ds(c * nblk + k, 1), :], (SEL_BLOCK, LANES)) for k in range(nblk)],
            axis=0)
        kpos = c * ck + _iota((ck, 1), 0)
        s = s + bias - slope_l * (qpos_l - kpos).astype(F32)
        sel_update(s, x)

    @pl.when((ph == 1) & (c == last))
    def _():
        xn = _pad_rows(new_ref[:, 2 * gw:4 * gw], LANES)
        s = pair_scores(xn)
        rown = _iota((LANES, 1), 0)
        dist = qpos_l - (past_len + rown)
        s = s + selb_ref[nsp:nsp + 1, :] - slope_l * dist.astype(F32)
        s = jnp.where((dist >= 0) & (rown < tn), s, NEG_INF)
        sel_update(s, xn)
        l_col = col_of(l_ref[...])
        sig = jax.nn.sigmoid(gb_ref[...])
        _sample_outputs(osel_ref, [acc_ref[q] / l_col for q in range(2)], sig, 1, tn)


def _nsa_sample(zq, nsa_new, gb, wexp, cache, pt, bd, tn, n_pages, pps):
    past_len = n_pages * PAGE_SIZE
    gw = G_B * DH_B
    ncp = past_len // CMP_BLOCK
    nsp = past_len // SEL_BLOCK
    nsc = -(-(nsp + 1) // 8) * 8
    nb = pps * PAGE_SIZE // CMP_BLOCK
    kern = functools.partial(_nsa_sample_kernel, tn=tn, pps=pps, past_len=past_len)

    def page_spec(p):
        return pl.BlockSpec((1, PAGE_SIZE, 2 * gw),
                            lambda bb, ph, c, pt, p=p: (pt[bb * n_pages + c * pps + p], 0, ph))

    row_spec = lambda width, col: pl.BlockSpec((tn, width), lambda bb, ph, c, pt: (bb, col))
    grid_spec = pltpu.PrefetchScalarGridSpec(
        num_scalar_prefetch=1,
        grid=(bd, 2, n_pages // pps),
        in_specs=[row_spec(W_B, W_A // W_B), row_spec(4 * gw, 0), row_spec(LANES, 0),
                  pl.BlockSpec((2, CMP_BLOCK, gw), lambda bb, ph, c, pt: (0, 0, 0))]
                 + [page_spec(p) for p in range(pps)],
        out_specs=[row_spec(W_B, 0), row_spec(W_B, 0)],
        scratch_shapes=[pltpu.VMEM((2, LANES, LANES), BF16),
                        pltpu.VMEM((ncp, 2 * gw), F32),
                        pltpu.VMEM((2 * gw // LANES * nb, LANES), F32),
                        pltpu.VMEM((nsc, LANES), F32),
                        pltpu.VMEM((nsc, LANES), F32),
                        pltpu.VMEM((1, LANES), F32),
                        pltpu.VMEM((1, LANES), F32),
                        pltpu.VMEM((2, LANES, LANES), F32)])
    return pl.pallas_call(
        kern, grid_spec=grid_spec,
        out_shape=[jax.ShapeDtypeStruct((bd * tn, W_B), F32)] * 2,
        compiler_params=_params(("parallel", "arbitrary", "arbitrary")),
        name="nsa_sample",
    )(pt, zq, nsa_new, gb, wexp, *([cache] * pps))


def _win_sample_kernel(q_ref, new_ref, st_ref, gb_ref, o_ref, ns_ref, *, tn, past_len):
    gw = G_B * DH_B
    wb = st_ref.shape[1]
    qs = _sample_queries(q_ref[...], tn)
    st = st_ref[0]
    newp = _pad_rows(new_ref[...], LANES)
    row = _iota((LANES, 1), 0)
    pair_r = ((row >> _log2(tn)) & (G_B - 1)) >> 1
    head = ((row >> _log2(tn)) & (G_B - 1)) * R_B + (row >> _log2(G_B * tn))
    slope = jnp.exp2(-8.0 * (head + 1).astype(F32) / H_B)
    qpos = past_len + (row & (tn - 1))

    def scores(x):
        s0 = _nt_dot(qs[0], x[:, 0:LANES].astype(BF16))
        s1 = _nt_dot(qs[1], x[:, LANES:2 * LANES].astype(BF16))
        return jnp.where(pair_r == 0, s0, s1)

    kpos_s = (past_len - wb) + _iota((1, wb), 1)
    d_s = qpos - kpos_s
    s_s = jnp.where((d_s >= 0) & (d_s < WINDOW) & (kpos_s >= 0), scores(st) - slope * d_s.astype(F32), NEG_INF)
    col_n = _iota((1, LANES), 1)
    d_n = qpos - (past_len + col_n)
    s_n = jnp.where((d_n >= 0) & (d_n < WINDOW) & (col_n < tn), scores(newp) - slope * d_n.astype(F32), NEG_INF)
    s = jnp.concatenate([s_s, s_n], axis=1)
    e = jnp.exp(s - jnp.max(s, axis=-1, keepdims=True))
    p = (e / jnp.sum(e, axis=-1, keepdims=True)).astype(BF16)
    outs = []
    for q in range(2):
        c0 = gw + q * LANES
        outs.append(jnp.dot(p[:, :wb], st[:, c0:c0 + LANES].astype(BF16), preferred_element_type=F32)
                    + jnp.dot(p[:, wb:], newp[:, c0:c0 + LANES].astype(BF16), preferred_element_type=F32))
    _sample_outputs(o_ref, outs, jax.nn.sigmoid(gb_ref[...]), 2, tn)
    ns_ref[0, 0:wb - tn, :] = st[tn:, :]
    ns_ref[0, wb - tn:, :] = new_ref[...]


def _win_sample(zq, win_new, state, gb, bd, tn, past_len):
    gw = G_B * DH_B
    wb = state.shape[1]
    assert wb == min(WINDOW, past_len) and wb + tn > WINDOW
    kern = functools.partial(_win_sample_kernel, tn=tn, past_len=past_len)
    return pl.pallas_call(
        kern, grid=(bd,),
        in_specs=[pl.BlockSpec((tn, W_B), lambda bb: (bb, W_A // W_B)),
                  pl.BlockSpec((tn, 2 * gw), lambda bb: (bb, 0)),
                  pl.BlockSpec((1, wb, 2 * gw), lambda bb: (bb, 0, 0)),
                  pl.BlockSpec((tn, LANES), lambda bb: (bb, 0))],
        out_specs=[pl.BlockSpec((tn, W_B), lambda bb: (bb, 0)),
                   pl.BlockSpec((1, wb, 2 * gw), lambda bb: (bb, 0, 0))],
        out_shape=[jax.ShapeDtypeStruct((bd * tn, W_B), F32),
                   jax.ShapeDtypeStruct((bd, wb, 2 * gw), F32)],
        compiler_params=_params(("parallel",)),
        name="win_sample",
    )(zq, win_new, state, gb)


def kernel(x_prompt, x_sample, cache_diff_kv, cache_nsa_kv, state_win_kv, page_table, w_in, w_proj_a,
           w_proj_b, w_out, lambda_qk, diff_gain, w_cmp, norm_attn, norm_mlp, w_up, w_down, norm_final):
    depth = w_in.shape[0]
    assert depth == 1
    b, t, d = x_prompt.shape
    bd, tn, _ = x_sample.shape
    l = 0
    lam_init = 0.8 - 0.6 * math.exp(-0.3 * l)
    w = _split_w_in(w_in[l])
    wpa, wpb, wo = w_proj_a[l].astype(BF16), w_proj_b[l].astype(BF16), w_out[l].astype(BF16)
    wup, wdn = w_up[l].astype(BF16), w_down[l].astype(BF16)
    wexp = jnp.repeat(w_cmp[l], DH_B, axis=-1)

    xp = x_prompt.reshape(b * t, d)
    up = _rmsnorm(xp, norm_attn[l], BF16, 512)
    (oa, ocmp, osel, owin, gm), (kva32, nsa32, win32) = _prompt_mixer(
        up, w, lambda_qk[l], diff_gain[l], wexp, b, t, lam_init)
    hp, u2 = _post_attention(oa, ocmp, osel, owin, gm, xp, wpa, wpb, wo, norm_mlp[l], 256)
    y_prompt = _mlp_final(u2, hp, wup, wdn, norm_final, 512, 512).reshape(b, t, d)
    keep = min(WINDOW, t)
    diff_kv_prompt = kva32.reshape(1, b, t, 2, HKV_A, 2 * DH_A)
    nsa_kv_prompt = nsa32.reshape(1, b, t, 4, G_B, DH_B)
    win_kv_prompt = win32.reshape(b, t, 2, G_B, DH_B)[None, :, t - keep:]

    n_pages = page_table.shape[1]
    past_len = n_pages * PAGE_SIZE
    n_pool = cache_diff_kv.shape[1]
    assert R_B * G_B * tn == LANES and tn < CMP_BLOCK and past_len % (8 * PAGE_SIZE) == 0
    pps = 8
    xs = x_sample.reshape(bd * tn, d)
    us = _rmsnorm(xs, norm_attn[l], BF16, bd * tn)
    zq_s, kva_s, _, nsa_s, win_s, gb_s, gm_s = _in_projection(us, w, bd * tn, F32)
    pt = page_table.reshape(-1)
    oa_s = _diff_sample(zq_s, kva_s, cache_diff_kv[l].reshape(n_pool, PAGE_SIZE, -1), pt,
                        lambda_qk[l], diff_gain[l], bd, tn, n_pages, pps, lam_init)
    ocmp_s, osel_s = _nsa_sample(zq_s, nsa_s, gb_s, wexp, cache_nsa_kv[l].reshape(n_pool, PAGE_SIZE, -1), pt,
                                 bd, tn, n_pages, pps)
    wb = state_win_kv.shape[2]
    owin_s, win_state = _win_sample(zq_s, win_s, state_win_kv[l].reshape(bd, wb, -1), gb_s, bd, tn, past_len)
    hs, u2s = _post_attention(oa_s, ocmp_s, osel_s, owin_s, gm_s, xs, wpa, wpb, wo, norm_mlp[l], 256)
    y_sample = _mlp_final(u2s, hs, wup, wdn, norm_final, 512, 512).reshape(bd, tn, d)
    diff_kv_sample = kva_s.reshape(1, bd, tn, 2, HKV_A, 2 * DH_A)
    nsa_kv_sample = nsa_s.reshape(1, bd, tn, 4, G_B, DH_B)
    win_kv_sample = win_state.reshape(1, bd, wb, 2, G_B, DH_B)
    return (y_prompt, y_sample, diff_kv_prompt, nsa_kv_prompt, win_kv_prompt,
            diff_kv_sample, nsa_kv_sample, win_kv_sample)
```

```python
import functools
import math

import numpy as np
import jax
import jax.numpy as jnp
from jax import lax
from jax.experimental import pallas as pl
from jax.experimental.pallas import tpu as pltpu

F32, BF16, I32 = jnp.float32, jnp.bfloat16, jnp.int32

PAGE_SIZE = 128
H_A, HKV_A, R_A, DH_A = 8, 4, 2, 64
W_A = H_A * 2 * DH_A
H_B, G_B, R_B, DH_B = 16, 4, 4, 64
W_B = H_B * DH_B
CMP_BLOCK, SEL_BLOCK, TOP_N, WINDOW = 32, 64, 16, 512
EPS = 1e-6
NEG_INF = -1e30
FORCE_SCORE = 1e4
LOG2E = math.log2(math.e)
LANES = 128
HALF = LANES // 2
VMEM_LIMIT = 56 * 1024 * 1024


def _iota(shape, dim):
    return lax.broadcasted_iota(I32, shape, dim)


def _log2(n):
    assert n & (n - 1) == 0, n
    return n.bit_length() - 1


def _nt_dot(a, b):
    return lax.dot_general(a, b, (((1,), (1,)), ((), ())), preferred_element_type=F32)


def _params(sem):
    return pltpu.CompilerParams(dimension_semantics=sem, vmem_limit_bytes=VMEM_LIMIT)


def _tri_tiles(n):
    ii = np.array([i for i in range(n) for _ in range(i + 1)], np.int32)
    jj = np.array([j for i in range(n) for j in range(i + 1)], np.int32)
    return jnp.asarray(ii), jnp.asarray(jj)


def _slopes2(n):
    return jnp.exp2(-8.0 * jnp.arange(1, n + 1, dtype=F32) / n) * LOG2E


def _rmsnorm_kernel(x_ref, g_ref, o_ref):
    x = x_ref[...]
    y = x * lax.rsqrt(jnp.mean(x * x, axis=-1, keepdims=True) + EPS)
    o_ref[...] = (y * g_ref[...]).astype(o_ref.dtype)


def _rmsnorm(x, g, out_dtype, tm):
    m, d = x.shape
    return pl.pallas_call(
        _rmsnorm_kernel,
        grid=(m // tm,),
        in_specs=[pl.BlockSpec((tm, d), lambda i: (i, 0)),
                  pl.BlockSpec((1, d), lambda i: (0, 0))],
        out_specs=pl.BlockSpec((tm, d), lambda i: (i, 0)),
        out_shape=jax.ShapeDtypeStruct((m, d), out_dtype),
        compiler_params=_params(("parallel",)),
        name="rmsnorm",
    )(x, g.reshape(1, d))


def _mm_kernel(a_ref, w_ref, *o_refs):
    acc = jnp.dot(a_ref[...], w_ref[...], preferred_element_type=F32)
    for o in o_refs:
        o[...] = acc.astype(o.dtype)


def _matmul(a, w, out_dtypes, tm, tn, name):
    m, k = a.shape
    n = w.shape[1]
    tm, tn = min(tm, m), min(tn, n)
    outs = pl.pallas_call(
        _mm_kernel,
        grid=(n // tn, m // tm),
        in_specs=[pl.BlockSpec((tm, k), lambda j, i: (i, 0)),
                  pl.BlockSpec((k, tn), lambda j, i: (0, j))],
        out_specs=[pl.BlockSpec((tm, tn), lambda j, i: (i, j)) for _ in out_dtypes],
        out_shape=[jax.ShapeDtypeStruct((m, n), dt) for dt in out_dtypes],
        compiler_params=_params(("parallel", "parallel")),
        name=name,
    )(a, w)
    return outs


def _post_kernel(oa_ref, o1_ref, o2_ref, o3_ref, gm0_ref, gm1_ref, x_ref,
                 wpa_ref, wpb_ref, wo_ref, g_ref, h_ref, u_ref):
    oa = oa_ref[...].astype(BF16)
    ob = (o1_ref[...] + o2_ref[...] + o3_ref[...]).astype(BF16)
    pa = jnp.dot(oa, wpa_ref[...], preferred_element_type=F32)
    pb = jnp.dot(ob, wpb_ref[...], preferred_element_type=F32)
    mix = jax.nn.sigmoid(gm0_ref[...]) * pa + jax.nn.sigmoid(gm1_ref[...]) * pb
    y = jnp.dot(mix.astype(BF16), wo_ref[...], preferred_element_type=F32)
    h = x_ref[...] + y
    h_ref[...] = h
    u = h * lax.rsqrt(jnp.mean(h * h, axis=-1, keepdims=True) + EPS)
    u_ref[...] = (u * g_ref[...]).astype(u_ref.dtype)


def _post_attention(oa, o1, o2, o3, gm, x, wpa, wpb, wo, g_mlp, tm):
    m, d = x.shape
    tm = min(tm, m)
    row = lambda i: (i, 0)
    const = lambda i: (0, 0)
    once = pl.Buffered(1)
    return pl.pallas_call(
        _post_kernel,
        grid=(m // tm,),
        in_specs=[pl.BlockSpec((tm, W_A), row),
                  pl.BlockSpec((tm, W_B), row),
                  pl.BlockSpec((tm, W_B), row),
                  pl.BlockSpec((tm, W_B), row),
                  pl.BlockSpec((tm, d), lambda i: (i, 0)),
                  pl.BlockSpec((tm, d), lambda i: (i, 1)),
                  pl.BlockSpec((tm, d), row),
                  pl.BlockSpec((W_A, d), const, pipeline_mode=once),
                  pl.BlockSpec((W_B, d), const, pipeline_mode=once),
                  pl.BlockSpec((d, d), const, pipeline_mode=once),
                  pl.BlockSpec((1, d), const)],
        out_specs=[pl.BlockSpec((tm, d), row), pl.BlockSpec((tm, d), row)],
        out_shape=[jax.ShapeDtypeStruct((m, d), F32), jax.ShapeDtypeStruct((m, d), BF16)],
        compiler_params=_params(("parallel",)),
        name="post_attention",
    )(oa, o1, o2, o3, gm, gm, x, wpa, wpb, wo, g_mlp.reshape(1, d))


def _mlp_kernel(u_ref, h_ref, wup_ref, wdn_ref, g_ref, o_ref, acc_ref):
    f = pl.program_id(1)

    @pl.when(f == 0)
    def _():
        acc_ref[...] = jnp.zeros_like(acc_ref)

    a = jnp.dot(u_ref[...], wup_ref[...], preferred_element_type=F32)
    a = jnp.square(jnp.maximum(a, 0.0)).astype(BF16)
    acc_ref[...] += jnp.dot(a, wdn_ref[...], preferred_element_type=F32)

    @pl.when(f == pl.num_programs(1) - 1)
    def _():
        y = h_ref[...] + acc_ref[...]
        y = y * lax.rsqrt(jnp.mean(y * y, axis=-1, keepdims=True) + EPS)
        o_ref[...] = y * g_ref[...]


def _mlp_final(u, h, wup, wdn, g_final, tm, tf):
    m, d = h.shape
    dff = wup.shape[1]
    tm, tf = min(tm, m), min(tf, dff)
    return pl.pallas_call(
        _mlp_kernel,
        grid=(m // tm, dff // tf),
        in_specs=[pl.BlockSpec((tm, d), lambda i, f: (i, 0)),
                  pl.BlockSpec((tm, d), lambda i, f: (i, 0)),
                  pl.BlockSpec((d, tf), lambda i, f: (0, f)),
                  pl.BlockSpec((tf, d), lambda i, f: (f, 0)),
                  pl.BlockSpec((1, d), lambda i, f: (0, 0))],
        out_specs=pl.BlockSpec((tm, d), lambda i, f: (i, 0)),
        out_shape=jax.ShapeDtypeStruct((m, d), F32),
        scratch_shapes=[pltpu.VMEM((tm, d), F32)],
        compiler_params=_params(("parallel", "arbitrary")),
        name="mlp_final",
    )(u, h, wup, wdn, g_final.reshape(1, d))


def _flash_init(m_ref, l_ref, acc_ref):
    m_ref[...] = jnp.full(m_ref.shape, NEG_INF, F32)
    l_ref[...] = jnp.zeros(l_ref.shape, F32)
    acc_ref[...] = jnp.zeros(acc_ref.shape, F32)


def _lane_blocks_sum(p):
    out = p[:, 0:LANES]
    for n in range(1, p.shape[1] // LANES):
        out = out + p[:, n * LANES:(n + 1) * LANES]
    return out


def _flash_step(s, shift, m_prev, l_prev, acc_prev, pv_fn):
    m_cur = jnp.max(s, axis=-1, keepdims=True)
    if shift is not None:
        m_cur = m_cur - shift
    m_new = jnp.maximum(m_prev, m_cur)
    alpha = jnp.exp2(m_prev - m_new)
    m_adj = m_new if shift is None else m_new + shift
    p = jnp.exp2(s - jnp.concatenate([m_adj] * (s.shape[1] // LANES), axis=1))
    l_new = alpha * l_prev + _lane_blocks_sum(p)
    acc_new = alpha * acc_prev + pv_fn(p.astype(BF16))
    return m_new, l_new, acc_new


def _flash_rows(rs, s, shift, v, m_ref, l_ref, acc_ref):
    m, l, acc = _flash_step(s, shift, m_ref[rs, :], l_ref[rs, :], acc_ref[rs, :],
                            lambda p: jnp.dot(p, v, preferred_element_type=F32))
    m_ref[rs, :] = m
    l_ref[rs, :] = l
    acc_ref[rs, :] = acc


def _row_total(l):
    return jnp.sum(l, axis=-1, keepdims=True)


def _alibi_tiles(slopes, tq):
    ng, nr = slopes.shape
    d = (jnp.arange(tq)[:, None] - jnp.arange(tq)[None, :]).astype(F32)
    off = -slopes[:, :, None, None] * d
    tiles = jnp.stack([off, jnp.where(d >= 0, off, NEG_INF)], axis=1).reshape(ng, 2, nr * tq, tq)
    step = jnp.broadcast_to((slopes * tq)[:, :, None, None], (ng, nr, tq, LANES)).reshape(ng, nr * tq, LANES)
    return tiles, step


def _window_tiles(slopes, tq, nj):
    ng, nr = slopes.shape
    d = jnp.arange(tq)[:, None] - jnp.arange(tq)[None, :]
    dist = (nj - 1 - jnp.arange(nj))[:, None, None] * tq + d[None]
    dist = dist[None, :, None]
    bias = jnp.where((dist >= 0) & (dist < WINDOW),
                     -slopes[:, None, :, None, None] * dist.astype(F32), NEG_INF)
    return bias.reshape(ng, nj, nr * tq, tq)


def _diff_lambda(lam_ref, lam_init):
    lf = lam_ref[...]
    e1 = jnp.exp(jnp.sum(lf[0:1] * lf[1:2], axis=-1, keepdims=True))
    e2 = jnp.exp(jnp.sum(lf[2:3] * lf[3:4], axis=-1, keepdims=True))
    return e1 - e2 + lam_init


def _diff_finish(o1, o2, lam, gain, lam_init):
    o = o1 - lam * o2
    y = o * lax.rsqrt(jnp.mean(o * o, axis=-1, keepdims=True) + EPS)
    return (y * gain) * (1.0 - lam_init)


def _diff_prompt_kernel(ii_ref, jj_ref, q_ref, k_ref, v_ref, tile_ref, step_ref, lam_ref, gain_ref, o_ref,
                        qs_ref, m_ref, l_ref, acc_ref, *, tq, lam_init):
    t = pl.program_id(2)
    i = ii_ref[t]
    j = jj_ref[t]
    nblk = 2 * R_A

    @pl.when(j == 0)
    def _():
        upper = _iota((tq, LANES), 1) >= HALF
        for r in range(R_A):
            qv = q_ref[:, r * LANES:(r + 1) * LANES]
            for c in range(2):
                qs_ref[(2 * r + c) * tq:(2 * r + c + 1) * tq, :] = jnp.where(upper == bool(c), qv, 0).astype(BF16)
        _flash_init(m_ref, l_ref, acc_ref)

    diag = (i == j).astype(I32)
    steps = (i - j).astype(F32)
    k = k_ref[...]
    v = v_ref[...]
    for n in range(nblk):
        rs = slice(n * tq, (n + 1) * tq)
        s = _nt_dot(qs_ref[rs, :], k) + tile_ref[0, diag, rs, :]
        _flash_rows(rs, s, step_ref[0, rs, :] * steps, v, m_ref, l_ref, acc_ref)

    @pl.when(j == i)
    def _():
        lam = _diff_lambda(lam_ref, lam_init)
        for r in range(R_A):
            b0 = 2 * r * tq
            o1 = acc_ref[b0:b0 + tq, :] / _row_total(l_ref[b0:b0 + tq, :])
            o2 = acc_ref[b0 + tq:b0 + 2 * tq, :] / _row_total(l_ref[b0 + tq:b0 + 2 * tq, :])
            y = _diff_finish(o1, o2, lam, gain_ref[...], lam_init)
            o_ref[:, r * LANES:(r + 1) * LANES] = y.astype(o_ref.dtype)


def _diff_prompt(qa, kva, lam_qk, gain, b, t, tq, lam_init):
    nq = t // tq
    ii, jj = _tri_tiles(nq)
    rows = 2 * R_A * tq
    slopes = jnp.repeat(_slopes2(H_A).reshape(HKV_A, R_A), 2, axis=1)
    tiles, step = _alibi_tiles(slopes, tq)
    kern = functools.partial(_diff_prompt_kernel, tq=tq, lam_init=lam_init)
    grid_spec = pltpu.PrefetchScalarGridSpec(
        num_scalar_prefetch=2,
        grid=(b, HKV_A, ii.shape[0]),
        in_specs=[pl.BlockSpec((tq, R_A * LANES), lambda bb, h, n, ii, jj: (bb * nq + ii[n], h)),
                  pl.BlockSpec((tq, LANES), lambda bb, h, n, ii, jj: (bb * nq + jj[n], h)),
                  pl.BlockSpec((tq, LANES), lambda bb, h, n, ii, jj: (bb * nq + jj[n], HKV_A + h)),
                  pl.BlockSpec((1, 2, rows, tq), lambda bb, h, n, ii, jj: (h, 0, 0, 0)),
                  pl.BlockSpec((1, rows, LANES), lambda bb, h, n, ii, jj: (h, 0, 0)),
                  pl.BlockSpec((4, DH_A), lambda bb, h, n, ii, jj: (0, 0)),
                  pl.BlockSpec((1, 2 * DH_A), lambda bb, h, n, ii, jj: (0, 0))],
        out_specs=pl.BlockSpec((tq, R_A * LANES), lambda bb, h, n, ii, jj: (bb * nq + ii[n], h)),
        scratch_shapes=[pltpu.VMEM((rows, LANES), BF16),
                        pltpu.VMEM((rows, LANES), F32),
                        pltpu.VMEM((rows, LANES), F32),
                        pltpu.VMEM((rows, LANES), F32)])
    return pl.pallas_call(
        kern, grid_spec=grid_spec,
        out_shape=jax.ShapeDtypeStruct((b * t, W_A), BF16),
        compiler_params=_params(("parallel", "parallel", "arbitrary")),
        name="diff_prompt",
    )(ii, jj, qa, kva, kva, tiles, step, lam_qk, gain.reshape(1, 2 * DH_A))


def _dup_half(v, g):
    rolled = pltpu.roll(v, HALF, 1)
    low = _iota(v.shape, 1) < HALF
    return jnp.where(low, v, rolled) if g % 2 == 0 else jnp.where(low, rolled, v)


def _prep_kernel(nsa_ref, win_ref, wexp_ref, kcd_ref, vcd_ref, ksa_ref, vsd_ref, kwd_ref, vwd_ref,
                 tmp_ref, *, tp, nc):
    step = pl.program_id(1)
    gw = G_B * DH_B
    x = nsa_ref[...]
    w = win_ref[...]
    lane = _iota((tp, LANES), 1)
    low = lane < HALF
    blk = (step * tp + _iota((tp, LANES), 0)) >> _log2(SEL_BLOCK)
    onehot = jnp.where(lane - HALF == blk, 1.0, 0.0)
    for g in range(G_B):
        c0 = (g // 2) * LANES
        vk = x[:, 2 * gw + c0:2 * gw + c0 + LANES]
        klow = vk if g % 2 == 0 else pltpu.roll(vk, HALF, 1)
        ksa_ref[0, g] = jnp.where(low, klow, onehot).astype(BF16)
        vsd_ref[0, g] = _dup_half(x[:, 3 * gw + c0:3 * gw + c0 + LANES], g).astype(BF16)
        kwd_ref[0, g] = _dup_half(w[:, c0:c0 + LANES], g).astype(BF16)
        vwd_ref[0, g] = _dup_half(w[:, gw + c0:gw + c0 + LANES], g).astype(BF16)
    nb = tp // CMP_BLOCK
    hb = nb // 2
    for slot, dst in ((0, kcd_ref), (1, vcd_ref)):
        xc = x[:, slot * gw:(slot + 1) * gw]
        comp = jnp.sum(xc.reshape(nb, CMP_BLOCK, gw) * wexp_ref[slot][None], axis=1)
        for c in range(gw // LANES):
            tmp_ref[c * nb:(c + 1) * nb, :] = comp[:, c * LANES:(c + 1) * LANES]
        for par in range(2):
            start = pl.multiple_of(par * (nc // 2) + step * hb, 8)
            for g in range(G_B):
                rr = tmp_ref[pl.ds((g // 2) * nb + par, hb, stride=2), :]
                dst[0, g, pl.ds(start, hb), :] = _dup_half(rr, g)


def _nsa_prep(nsa, win, wexp, b, t, tp):
    nc = t // CMP_BLOCK
    gw = G_B * DH_B
    kern = functools.partial(_prep_kernel, tp=tp, nc=nc)
    small = jax.ShapeDtypeStruct((b, G_B, nc, LANES), F32)
    big = jax.ShapeDtypeStruct((b, G_B, t, LANES), BF16)
    small_spec = pl.BlockSpec((1, G_B, nc, LANES), lambda bb, s: (bb, 0, 0, 0))
    big_spec = pl.BlockSpec((1, G_B, tp, LANES), lambda bb, s: (bb, 0, s, 0))
    nt = t // tp
    return pl.pallas_call(
        kern, grid=(b, nt),
        in_specs=[pl.BlockSpec((tp, 4 * gw), lambda bb, s: (bb * nt + s, 0)),
                  pl.BlockSpec((tp, 2 * gw), lambda bb, s: (bb * nt + s, 0)),
                  pl.BlockSpec((2, CMP_BLOCK, gw), lambda bb, s: (0, 0, 0))],
        out_specs=[small_spec, small_spec, big_spec, big_spec, big_spec, big_spec],
        out_shape=[small, small, big, big, big, big],
        scratch_shapes=[pltpu.VMEM((gw // LANES * (tp // CMP_BLOCK), LANES), F32)],
        compiler_params=_params(("parallel", "arbitrary")),
        name="nsa_prep",
    )(nsa, win, wexp)


def _topk_bias(sc, sidx):
    rank = jnp.zeros(sc.shape, I32)
    for c in range(sc.shape[0]):
        other = sc[c:c + 1, :]
        beats = (other > sc) | ((other == sc) & (sidx > c))
        rank = rank + beats.astype(I32)
    return jnp.where(rank < TOP_N, 0.0, NEG_INF)


def _cmp_prompt_kernel(q_ref, kcd_ref, vcd_ref, gb_ref, ocmp_ref, qaug_ref, *, tq, nc):
    i = pl.program_id(1)
    ns = nc // 2
    lowq = _iota((tq, LANES), 1) < HALF
    rowc = _iota((nc, tq), 0)
    qpos = i * tq + _iota((nc, tq), 1)
    blk = jnp.where(rowc >= ns, 2 * (rowc - ns) + 1, 2 * rowc)
    c_mid = (blk * CMP_BLOCK).astype(F32) + (CMP_BLOCK - 1) * 0.5
    valid = blk * CMP_BLOCK + (CMP_BLOCK - 1) <= qpos
    adist = jnp.abs(qpos.astype(F32) - c_mid)
    sig = jax.nn.sigmoid(gb_ref[...])
    sidx = _iota((ns, tq), 0)
    qp = i * tq + _iota((ns, tq), 1)
    forced = (sidx == (qp >> _log2(SEL_BLOCK))) | (sidx == 0)
    valid_s = sidx * SEL_BLOCK <= qp
    for g in range(G_B):
        kc = kcd_ref[0, g].astype(BF16)
        vc = vcd_ref[0, g].astype(BF16)
        imp = jnp.zeros((nc, tq), F32)
        outs, qlows = [], []
        for r in range(R_B):
            h = g * R_B + r
            qv = q_ref[:, (h // 2) * LANES:(h // 2 + 1) * LANES]
            qm = jnp.where(lowq == (h % 2 == 0), qv, 0).astype(BF16)
            s = _nt_dot(kc, qm)
            s = s - (2.0 ** (-8.0 * (h + 1) / H_B) * LOG2E) * adist
            sm = jnp.where(valid, s, NEG_INF)
            e = jnp.exp2(sm - jnp.max(sm, axis=0, keepdims=True))
            p = jnp.where(valid, e / jnp.sum(e, axis=0, keepdims=True), 0.0)
            imp = imp + p
            o2 = jnp.dot(p.T.astype(BF16), vc, preferred_element_type=F32)
            outs.append(o2 * sig[:, h:h + 1])
            qlows.append(qv if h % 2 == 0 else pltpu.roll(qv, HALF, 1))
        for k in range(R_B // 2):
            c0 = (g * (R_B // 2) + k) * LANES
            ocmp_ref[:, c0:c0 + LANES] = jnp.where(lowq, outs[2 * k], outs[2 * k + 1])
        sc = jnp.where(forced, FORCE_SCORE, jnp.where(valid_s, imp[:ns] + imp[ns:], -1.0))
        bias_t = _topk_bias(sc, sidx)
        bias = jnp.concatenate([bias_t] * (LANES // ns), axis=0).T.astype(BF16)
        for r in range(R_B):
            qaug_ref[0, g, r] = jnp.where(lowq, qlows[r], bias).astype(BF16)


def _cmp_prompt(zq, kcd, vcd, gb, b, t, tq):
    nc = t // CMP_BLOCK
    nq = t // tq
    kern = functools.partial(_cmp_prompt_kernel, tq=tq, nc=nc)
    small_spec = pl.BlockSpec((1, G_B, nc, LANES), lambda bb, i: (bb, 0, 0, 0))
    return pl.pallas_call(
        kern, grid=(b, nq),
        in_specs=[pl.BlockSpec((tq, W_B), lambda bb, i: (bb * nq + i, W_A // W_B)),
                  small_spec, small_spec,
                  pl.BlockSpec((tq, LANES), lambda bb, i: (bb * nq + i, 0))],
        out_specs=[pl.BlockSpec((tq, W_B), lambda bb, i: (bb * nq + i, 0)),
                   pl.BlockSpec((1, G_B, R_B, tq, LANES), lambda bb, i: (bb, 0, 0, i, 0))],
        out_shape=[jax.ShapeDtypeStruct((b * t, W_B), F32),
                   jax.ShapeDtypeStruct((b, G_B, R_B, t, LANES), BF16)],
        compiler_params=_params(("parallel", "parallel")),
        name="cmp_prompt",
    )(zq, kcd, vcd, gb)


def _gate_column(sig, col):
    return jnp.sum(jnp.where(_iota(sig.shape, 1) == col, sig, 0.0), axis=-1, keepdims=True)


def _group_finish(o_ref, gb_ref, l_ref, acc_ref, g, branch, tq):
    sig = jax.nn.sigmoid(gb_ref[...])
    low = _iota((tq, LANES), 1) < HALF
    outs = []
    for r in range(R_B):
        rs = slice(r * tq, (r + 1) * tq)
        o = acc_ref[rs, :] / _row_total(l_ref[rs, :])
        outs.append(o * _gate_column(sig, branch * H_B + g * R_B + r))
    for k in range(R_B // 2):
        o_ref[:, k * LANES:(k + 1) * LANES] = jnp.where(low, outs[2 * k], outs[2 * k + 1])


def _sel_prompt_kernel(ii_ref, jj_ref, qa_ref, k_ref, v_ref, tile_ref, step_ref, gb_ref, o_ref,
                       m_ref, l_ref, acc_ref, *, tq):
    g = pl.program_id(1)
    t = pl.program_id(2)
    i = ii_ref[t]
    j = jj_ref[t]

    @pl.when(j == 0)
    def _():
        _flash_init(m_ref, l_ref, acc_ref)

    diag = (i == j).astype(I32)
    steps = (i - j).astype(F32)
    k = k_ref[0, 0]
    v = v_ref[0, 0]
    for r in range(R_B):
        rs = slice(r * tq, (r + 1) * tq)
        s = _nt_dot(qa_ref[0, 0, r], k) + tile_ref[0, diag, rs, :]
        _flash_rows(rs, s, step_ref[0, rs, :] * steps, v, m_ref, l_ref, acc_ref)

    @pl.when(j == i)
    def _():
        _group_finish(o_ref, gb_ref, l_ref, acc_ref, g, 1, tq)


def _sel_prompt(qaug, ksa, vsd, gb, b, t, tq):
    nq = t // tq
    ii, jj = _tri_tiles(nq)
    rows = R_B * tq
    tiles, step = _alibi_tiles(_slopes2(H_B).reshape(G_B, R_B), tq)
    kern = functools.partial(_sel_prompt_kernel, tq=tq)
    kv_spec = pl.BlockSpec((1, 1, tq, LANES), lambda bb, g, n, ii, jj: (bb, g, jj[n], 0))
    grid_spec = pltpu.PrefetchScalarGridSpec(
        num_scalar_prefetch=2,
        grid=(b, G_B, ii.shape[0]),
        in_specs=[pl.BlockSpec((1, 1, R_B, tq, LANES), lambda bb, g, n, ii, jj: (bb, g, 0, ii[n], 0)),
                  kv_spec, kv_spec,
                  pl.BlockSpec((1, 2, rows, tq), lambda bb, g, n, ii, jj: (g, 0, 0, 0)),
                  pl.BlockSpec((1, rows, LANES), lambda bb, g, n, ii, jj: (g, 0, 0)),
                  pl.BlockSpec((tq, LANES), lambda bb, g, n, ii, jj: (bb * nq + ii[n], 0))],
        out_specs=pl.BlockSpec((tq, R_B * DH_B), lambda bb, g, n, ii, jj: (bb * nq + ii[n], g)),
        scratch_shapes=[pltpu.VMEM((rows, LANES), F32),
                        pltpu.VMEM((rows, LANES), F32),
                        pltpu.VMEM((rows, LANES), F32)])
    return pl.pallas_call(
        kern, grid_spec=grid_spec,
        out_shape=jax.ShapeDtypeStruct((b * t, W_B), F32),
        compiler_params=_params(("parallel", "parallel", "arbitrary")),
        name="sel_prompt",
    )(ii, jj, qaug, ksa, vsd, tiles, step, gb)


def _win_prompt_kernel(q_ref, k_ref, v_ref, tile_ref, gb_ref, o_ref, qs_ref, m_ref, l_ref, acc_ref, *, tq, nj):
    g = pl.program_id(1)
    i = pl.program_id(2)
    j = pl.program_id(3)

    @pl.when(j == 0)
    def _():
        low = _iota((tq, LANES), 1) < HALF
        for r in range(R_B):
            qv = q_ref[:, (r // 2) * LANES:(r // 2 + 1) * LANES]
            qs_ref[r * tq:(r + 1) * tq, :] = jnp.where(low == (r % 2 == 0), qv, 0).astype(BF16)
        _flash_init(m_ref, l_ref, acc_ref)

    @pl.when(i - (nj - 1) + j >= 0)
    def _():
        k = k_ref[0, 0]
        v = v_ref[0, 0]
        for r in range(R_B):
            rs = slice(r * tq, (r + 1) * tq)
            s = _nt_dot(qs_ref[rs, :], k) + tile_ref[0, j, rs, :]
            _flash_rows(rs, s, None, v, m_ref, l_ref, acc_ref)

    @pl.when(j == nj - 1)
    def _():
        _group_finish(o_ref, gb_ref, l_ref, acc_ref, g, 2, tq)


def _win_prompt(zq, kwd, vwd, gb, b, t, tq):
    nq = t // tq
    nj = -(-WINDOW // tq) + 1
    qoff = W_A // (R_B * DH_B)
    rows = R_B * tq
    tiles = _window_tiles(_slopes2(H_B).reshape(G_B, R_B), tq, nj)
    kern = functools.partial(_win_prompt_kernel, tq=tq, nj=nj)
    kv_spec = pl.BlockSpec((1, 1, tq, LANES),
                           lambda bb, g, i, j: (bb, g, jnp.maximum(i - (nj - 1) + j, 0), 0))
    return pl.pallas_call(
        kern, grid=(b, G_B, nq, nj),
        in_specs=[pl.BlockSpec((tq, R_B * DH_B), lambda bb, g, i, j: (bb * nq + i, qoff + g)),
                  kv_spec, kv_spec,
                  pl.BlockSpec((1, nj, rows, tq), lambda bb, g, i, j: (g, 0, 0, 0)),
                  pl.BlockSpec((tq, LANES), lambda bb, g, i, j: (bb * nq + i, 0))],
        out_specs=pl.BlockSpec((tq, R_B * DH_B), lambda bb, g, i, j: (bb * nq + i, g)),
        out_shape=jax.ShapeDtypeStruct((b * t, W_B), F32),
        scratch_shapes=[pltpu.VMEM((rows, LANES), BF16),
                        pltpu.VMEM((rows, LANES), F32),
                        pltpu.VMEM((rows, LANES), F32),
                        pltpu.VMEM((rows, LANES), F32)],
        compiler_params=_params(("parallel", "parallel", "parallel", "arbitrary")),
        name="win_prompt",
    )(zq, kwd, vwd, tiles, gb)


def _split_w_in(w_in):
    sizes = (W_A, HKV_A * 2 * DH_A, HKV_A * 2 * DH_A, W_B, 6 * G_B * DH_B, 3 * H_B, 2 * w_in.shape[0])
    offs = np.concatenate([[0], np.cumsum(sizes)])
    seg = lambda a, b_: w_in[:, a:b_]
    gw = G_B * DH_B
    return dict(
        q=jnp.concatenate([seg(offs[0], offs[1]) * (DH_A ** -0.5 * LOG2E),
                           seg(offs[3], offs[4]) * (DH_B ** -0.5 * LOG2E)], axis=1).astype(BF16),
        kva=seg(offs[1], offs[3]).astype(BF16),
        nsa=seg(offs[4], offs[4] + 4 * gw).astype(BF16),
        win=seg(offs[4] + 4 * gw, offs[5]).astype(BF16),
        gb=jnp.pad(seg(offs[5], offs[6]), ((0, 0), (0, LANES - 3 * H_B))).astype(BF16),
        gm=seg(offs[6], offs[7]).astype(BF16))


def _in_projection(u, w, tm, q_dtype=BF16):
    zq, = _matmul(u, w["q"], (q_dtype,), tm, 512, "proj_q")
    kva32, kva16 = _matmul(u, w["kva"], (F32, BF16), tm, 512, "proj_kva")
    nsa32, = _matmul(u, w["nsa"], (F32,), tm, 512, "proj_nsa")
    win32, = _matmul(u, w["win"], (F32,), tm, 512, "proj_win")
    gb, = _matmul(u, w["gb"], (F32,), tm, LANES, "proj_gb")
    gm, = _matmul(u, w["gm"], (F32,), tm, 512, "proj_gm")
    return zq, kva32, kva16, nsa32, win32, gb, gm


def _prompt_mixer(u, w, lam_qk, gain, wexp, b, t, lam_init, tq=256):
    tq = min(tq, t)
    zq, kva32, kva16, nsa32, win32, gb, gm = _in_projection(u, w, 1024)
    oa = _diff_prompt(zq, kva16, lam_qk, gain, b, t, tq, lam_init)
    kcd, vcd, ksa, vsd, kwd, vwd = _nsa_prep(nsa32, win32, wexp, b, t, min(512, t))
    ocmp, qaug = _cmp_prompt(zq, kcd, vcd, gb, b, t, tq)
    osel = _sel_prompt(qaug, ksa, vsd, gb, b, t, tq)
    owin = _win_prompt(zq, kwd, vwd, gb, b, t, tq)
    return (oa, ocmp, osel, owin, gm), (kva32, nsa32, win32)


def _pad_rows(x, rows):
    return jnp.concatenate([x, jnp.zeros((rows - x.shape[0], x.shape[1]), x.dtype)], axis=0)


def _diff_sample_kernel(pt_ref, q_ref, new_ref, lam_ref, gain_ref, *rest, tn, pps, past_len, lam_init):
    page_refs = rest[:pps]
    o_ref, qs_ref, m_ref, l_ref, acc_ref = rest[pps:]
    c = pl.program_id(1)
    rows = 2 * R_A * tn
    hw = HKV_A * 2 * DH_A
    stride = 2 * HKV_A
    row = _iota((rows, 1), 0)
    tq = row & (tn - 1)

    def slopes(hkv):
        head = hkv * R_A + (row >> _log2(2 * tn))
        return jnp.exp2(-8.0 * (head + 1).astype(F32) / H_A) * LOG2E

    def update(hkv, s, pv_fn):
        m, l, acc = _flash_step(s, None, m_ref[hkv], l_ref[hkv], acc_ref[hkv], pv_fn)
        m_ref[hkv] = m
        l_ref[hkv] = l
        acc_ref[hkv] = acc

    @pl.when(c == 0)
    def _():
        upper = _iota((tn, LANES), 1) >= HALF
        for hkv in range(HKV_A):
            parts = []
            for r in range(R_A):
                h = hkv * R_A + r
                qv = q_ref[:, h * LANES:(h + 1) * LANES]
                parts += [jnp.where(upper == bool(cc), qv, 0.0) for cc in range(2)]
            qs_ref[hkv] = jnp.concatenate(parts, axis=0).astype(BF16)
        _flash_init(m_ref, l_ref, acc_ref)

    kpos = c * (pps * PAGE_SIZE) + _iota((1, pps * PAGE_SIZE), 1)
    dist = ((past_len + tq) - kpos).astype(F32)
    for hkv in range(HKV_A):
        q = qs_ref[hkv]
        s = jnp.concatenate(
            [_nt_dot(q, pr[pl.ds(hkv, PAGE_SIZE, stride=stride), :].astype(BF16)) for pr in page_refs], axis=1)
        s = s - slopes(hkv) * dist

        def pv(p, hkv=hkv):
            out = None
            for n, pr in enumerate(page_refs):
                vals = pr[pl.ds(HKV_A + hkv, PAGE_SIZE, stride=stride), :].astype(BF16)
                part = jnp.dot(p[:, n * PAGE_SIZE:(n + 1) * PAGE_SIZE], vals, preferred_element_type=F32)
                out = part if out is None else out + part
            return out

        update(hkv, s, pv)

    @pl.when(c == pl.num_programs(1) - 1)
    def _():
        lam = _diff_lambda(lam_ref, lam_init)
        newp = _pad_rows(new_ref[...], LANES)
        dnew = tq - _iota((1, LANES), 1)
        for hkv in range(HKV_A):
            c0 = hkv * LANES
            s = _nt_dot(qs_ref[hkv], newp[:, c0:c0 + LANES].astype(BF16))
            s = jnp.where(dnew >= 0, s - slopes(hkv) * dnew.astype(F32), NEG_INF)
            vn = newp[:, hw + c0:hw + c0 + LANES].astype(BF16)
            update(hkv, s, lambda p, vn=vn: jnp.dot(p, vn, preferred_element_type=F32))
            o = acc_ref[hkv] / _row_total(l_ref[hkv])
            for r in range(R_A):
                b0 = 2 * r * tn
                y = _diff_finish(o[b0:b0 + tn], o[b0 + tn:b0 + 2 * tn], lam, gain_ref[...], lam_init)
                h = hkv * R_A + r
                o_ref[:, h * LANES:(h + 1) * LANES] = y


def _diff_sample(zq, kva_new, cache, pt, lam_qk, gain, bd, tn, n_pages, pps, lam_init):
    rows = 2 * R_A * tn
    page_rows = PAGE_SIZE * 2 * HKV_A
    kern = functools.partial(_diff_sample_kernel, tn=tn, pps=pps, past_len=n_pages * PAGE_SIZE,
                             lam_init=lam_init)

    def page_spec(p):
        return pl.BlockSpec((page_rows, LANES), lambda bb, c, pt: (pt[bb * n_pages + c * pps + p], 0))

    grid_spec = pltpu.PrefetchScalarGridSpec(
        num_scalar_prefetch=1,
        grid=(bd, n_pages // pps),
        in_specs=[pl.BlockSpec((tn, W_A), lambda bb, c, pt: (bb, 0)),
                  pl.BlockSpec((tn, W_A), lambda bb, c, pt: (bb, 0)),
                  pl.BlockSpec((4, DH_A), lambda bb, c, pt: (0, 0)),
                  pl.BlockSpec((1, 2 * DH_A), lambda bb, c, pt: (0, 0))]
                 + [page_spec(p) for p in range(pps)],
        out_specs=pl.BlockSpec((tn, W_A), lambda bb, c, pt: (bb, 0)),
        scratch_shapes=[pltpu.VMEM((HKV_A, rows, LANES), BF16),
                        pltpu.VMEM((HKV_A, rows, LANES), F32),
                        pltpu.VMEM((HKV_A, rows, LANES), F32),
                        pltpu.VMEM((HKV_A, rows, LANES), F32)])
    return pl.pallas_call(
        kern, grid_spec=grid_spec,
        out_shape=jax.ShapeDtypeStruct((bd * tn, W_A), F32),
        compiler_params=_params(("parallel", "arbitrary")),
        name="diff_sample",
    )(pt, zq, kva_new, lam_qk, gain.reshape(1, 2 * DH_A), *([cache] * pps))


def _sample_queries(qblk, tn):
    low = _iota((tn, LANES), 1) < HALF
    zero = jnp.zeros((tn, LANES), F32)
    mats = ([], [])
    for r in range(R_B):
        for g in range(G_B):
            h = g * R_B + r
            qv = qblk[:, (h // 2) * LANES:(h // 2 + 1) * LANES]
            if h % 2 != g % 2:
                qv = pltpu.roll(qv, HALF, 1)
            mats[g // 2].append(jnp.where(low == (g % 2 == 0), qv, 0.0))
            mats[1 - g // 2].append(zero)
    return [jnp.concatenate(m, axis=0).astype(BF16) for m in mats]


def _sample_outputs(o_ref, o_pairs, sig, branch, tn):
    low = _iota((tn, LANES), 1) < HALF
    for g in range(G_B):
        for k in range(R_B // 2):
            parts = []
            for r in (2 * k, 2 * k + 1):
                r0 = (r * G_B + g) * tn
                src = o_pairs[g // 2][r0:r0 + tn, :]
                if g % 2 != r % 2:
                    src = pltpu.roll(src, HALF, 1)
                col = branch * H_B + g * R_B + r
                parts.append(src * sig[:, col:col + 1])
            c0 = (g * (R_B // 2) + k) * LANES
            o_ref[:, c0:c0 + LANES] = jnp.where(low, parts[0], parts[1])


def _query_consts(idx, tn, past_len):
    r_q = idx >> _log2(G_B * tn)
    g_q = (idx >> _log2(tn)) & (G_B - 1)
    slope = jnp.exp2(-8.0 * (g_q * R_B + r_q + 1).astype(F32) / H_B) * LOG2E
    return g_q >> 1, slope, past_len + (idx & (tn - 1))


def _col_of(row_vec):
    return jnp.broadcast_to(row_vec, (LANES, LANES)).T


def _nsa_sample_kernel(pt_ref, q_ref, new_ref, gb_ref, wc_ref, e_ref, *rest, tn, pps, past_len):
    page_refs = rest[:pps]
    (ocmp_ref, osel_ref, qop_ref, kcvc_ref, imp_ref, sc_ref, selb_ref,
     m_ref, l_ref, acc_ref) = rest[pps:]
    ph = pl.program_id(1)
    c = pl.program_id(2)
    last = pl.num_programs(2) - 1
    gw = G_B * DH_B
    ncp = past_len // CMP_BLOCK
    nsp = past_len // SEL_BLOCK
    ck = pps * PAGE_SIZE
    pair_l, slope_l, qpos_l = _query_consts(_iota((1, LANES), 1), tn, past_len)
    pair_r, slope_r, qpos_r = _query_consts(_iota((LANES, 1), 0), tn, past_len)

    @pl.when((ph == 0) & (c == 0))
    def _():
        qs = _sample_queries(q_ref[...], tn)
        qop_ref[0] = qs[0]
        qop_ref[1] = qs[1]

    @pl.when(ph == 0)
    def _():
        wc = wc_ref[...]
        w_hi = wc.astype(BF16)
        w_lo = (wc - w_hi.astype(F32)).astype(BF16)
        low8 = _iota((8, LANES), 1) < HALF
        for pp in range(pps // 2):
            x = jnp.concatenate([page_refs[2 * pp][...], page_refs[2 * pp + 1][...]], axis=1)
            x_hi = x.astype(BF16)
            x_lo = (x - x_hi.astype(F32)).astype(BF16)
            comp = _nt_dot(w_hi, x_hi) + _nt_dot(w_hi, x_lo) + _nt_dot(w_lo, x_hi)
            row0 = pl.multiple_of((c * (pps // 2) + pp) * 8, 8)
            for slot in range(2):
                for p in range(2):
                    va = slot * G_B + 2 * p
                    c0 = slot * gw + p * LANES
                    piece = jnp.where(low8, comp[va * 8:(va + 1) * 8, c0:c0 + LANES],
                                      comp[(va + 1) * 8:(va + 2) * 8, c0:c0 + LANES])
                    kcvc_ref[pl.ds(row0, 8), c0:c0 + LANES] = piece

    @pl.when((ph == 0) & (c == last))
    def _():
        kcvc = kcvc_ref[...]
        s = jnp.where(pair_l == 0,
                      _nt_dot(kcvc[:, 0:LANES].astype(BF16), qop_ref[0]),
                      _nt_dot(kcvc[:, LANES:2 * LANES].astype(BF16), qop_ref[1]))
        blk = _iota((ncp, LANES), 0)
        c_mid = (blk * CMP_BLOCK).astype(F32) + (CMP_BLOCK - 1) * 0.5
        valid = blk * CMP_BLOCK + (CMP_BLOCK - 1) <= qpos_l
        s = s - slope_l * jnp.abs(qpos_l.astype(F32) - c_mid)
        sm = jnp.where(valid, s, NEG_INF)
        e = jnp.exp2(sm - jnp.max(sm, axis=0, keepdims=True))
        p_t = jnp.where(valid, e / jnp.sum(e, axis=0, keepdims=True), 0.0)
        p = p_t.T.astype(BF16)
        outs = [jnp.dot(p, kcvc[:, gw + q * LANES:gw + (q + 1) * LANES].astype(BF16),
                        preferred_element_type=F32) for q in range(2)]
        sig = jax.nn.sigmoid(gb_ref[...])
        _sample_outputs(ocmp_ref, outs, sig, 0, tn)
        imp = p_t
        for k in range(1, R_B):
            imp = imp + pltpu.roll(p_t, k * G_B * tn, 1)
        imp_ref[...] = imp
        imp = imp_ref[pl.ds(0, nsp, stride=2), :] + imp_ref[pl.ds(1, nsp, stride=2), :]
        sidx = _iota((nsp, LANES), 0)
        forced = (sidx == (qpos_l >> _log2(SEL_BLOCK))) | (sidx == 0)
        sc_ref[0:nsp, :] = jnp.where(forced, FORCE_SCORE,
                                     jnp.where(sidx * SEL_BLOCK <= qpos_l, imp, -1.0))
        tail = sc_ref.shape[0] - nsp
        tidx = nsp + _iota((tail, LANES), 0)
        tforced = (tidx == (qpos_l >> _log2(SEL_BLOCK))) | (tidx == 0)
        tsc = jnp.where(tforced, FORCE_SCORE, jnp.where(tidx * SEL_BLOCK <= qpos_l, 0.0, -1.0))
        sc_ref[nsp:, :] = jnp.where(tidx == nsp, tsc, NEG_INF)
        sc = sc_ref[...]
        ridx = _iota(sc.shape, 0)

        def body(n, rank):
            other = sc_ref[pl.ds(n, 1), :]
            beats = (other > sc) | ((other == sc) & (ridx > n))
            return rank + beats.astype(I32)

        rank = lax.fori_loop(0, nsp + 1, body, jnp.zeros(sc.shape, I32))
        selb_ref[...] = jnp.where(rank < TOP_N, 1.0, 0.0)
        _flash_init(m_ref, l_ref, acc_ref)

    def sel_step(kts, vts, s_bias_fn):
        s = jnp.where(pair_r == 0,
                      jnp.dot(qop_ref[0], kts[0], preferred_element_type=F32),
                      jnp.dot(qop_ref[1], kts[1], preferred_element_type=F32))
        s = s_bias_fn(s)
        m_prev, l_prev = m_ref[...], l_ref[...]
        m_cur = jnp.max(s, axis=-1, keepdims=True)
        m_new = jnp.maximum(m_prev, m_cur)
        alpha = jnp.exp2(m_prev - m_new)
        p = jnp.exp2(s - jnp.concatenate([m_new] * (s.shape[1] // LANES), axis=1))
        l_ref[...] = alpha * l_prev + _lane_blocks_sum(p)
        pb = p.astype(BF16)
        for q in range(2):
            acc_ref[q] = alpha * acc_ref[q] + _nt_dot(pb, vts[q])
        m_ref[...] = m_new

    @pl.when(ph == 1)
    def _():
        kts = [jnp.concatenate([pr[q * LANES:(q + 1) * LANES, :] for pr in page_refs], axis=1).astype(BF16)
               for q in range(2)]
        vts = [jnp.concatenate([pr[gw + q * LANES:gw + (q + 1) * LANES, :] for pr in page_refs],
                               axis=1).astype(BF16) for q in range(2)]
        nblk = ck // SEL_BLOCK
        member = _pad_rows(selb_ref[pl.ds(pl.multiple_of(c * nblk, 8), nblk), :], LANES).T
        mask = jnp.dot(member.astype(BF16), e_ref[...], preferred_element_type=F32)
        dist = (qpos_r - (c * ck + _iota((1, ck), 1))).astype(F32)
        sel_step(kts, vts, lambda s: jnp.where(mask > 0.5, s - slope_r * dist, NEG_INF))

    @pl.when((ph == 1) & (c == last))
    def _():
        xn = _pad_rows(new_ref[:, 2 * gw:4 * gw], LANES)
        kts = [xn[:, q * LANES:(q + 1) * LANES].T.astype(BF16) for q in range(2)]
        vts = [xn[:, gw + q * LANES:gw + (q + 1) * LANES].T.astype(BF16) for q in range(2)]
        member = _col_of(selb_ref[nsp:nsp + 1, :])
        col = _iota((1, LANES), 1)
        dist = qpos_r - (past_len + col)
        ok = (member > 0.5) & (dist >= 0) & (col < tn)
        sel_step(kts, vts, lambda s: jnp.where(ok, s - slope_r * dist.astype(F32), NEG_INF))
        l_tot = _row_total(l_ref[...])
        sig = jax.nn.sigmoid(gb_ref[...])
        _sample_outputs(osel_ref, [acc_ref[q] / l_tot for q in range(2)], sig, 1, tn)


def _compress_weights(w_cmp):
    nblk = 2 * PAGE_SIZE // CMP_BLOCK
    tok = jnp.arange(2 * PAGE_SIZE)
    w = jnp.transpose(w_cmp, (0, 2, 1))[:, :, tok % CMP_BLOCK]
    hit = (tok[None, :] // CMP_BLOCK) == jnp.arange(nblk)[:, None]
    return jnp.where(hit[None, None], w[:, :, None, :], 0.0).reshape(2 * G_B * nblk, 2 * PAGE_SIZE)


def _nsa_sample(zq, nsa_new, gb, w_cmp, cache, pt, bd, tn, n_pages, pps):
    past_len = n_pages * PAGE_SIZE
    gw = G_B * DH_B
    ncp = past_len // CMP_BLOCK
    nsp = past_len // SEL_BLOCK
    nsc = -(-(nsp + 1) // 8) * 8
    ck = pps * PAGE_SIZE
    assert pps % 2 == 0 and 2 * PAGE_SIZE // CMP_BLOCK == 8 and (ck // SEL_BLOCK) % 8 == 0
    kern = functools.partial(_nsa_sample_kernel, tn=tn, pps=pps, past_len=past_len)
    expand = (jnp.arange(ck)[None, :] // SEL_BLOCK == jnp.arange(LANES)[:, None]).astype(BF16)

    def page_spec(p):
        return pl.BlockSpec((2 * gw, PAGE_SIZE),
                            lambda bb, ph, c, pt, p=p: (pt[bb * n_pages + c * pps + p] * 2 + ph, 0))

    row_spec = lambda width, col: pl.BlockSpec((tn, width), lambda bb, ph, c, pt: (bb, col))
    grid_spec = pltpu.PrefetchScalarGridSpec(
        num_scalar_prefetch=1,
        grid=(bd, 2, n_pages // pps),
        in_specs=[row_spec(W_B, W_A // W_B), row_spec(4 * gw, 0), row_spec(LANES, 0),
                  pl.BlockSpec((2 * G_B * 8, 2 * PAGE_SIZE), lambda bb, ph, c, pt: (0, 0)),
                  pl.BlockSpec((LANES, ck), lambda bb, ph, c, pt: (0, 0))]
                 + [page_spec(p) for p in range(pps)],
        out_specs=[row_spec(W_B, 0), row_spec(W_B, 0)],
        scratch_shapes=[pltpu.VMEM((2, LANES, LANES), BF16),
                        pltpu.VMEM((ncp, 2 * gw), F32),
                        pltpu.VMEM((ncp, LANES), F32),
                        pltpu.VMEM((nsc, LANES), F32),
                        pltpu.VMEM((nsc, LANES), F32),
                        pltpu.VMEM((LANES, LANES), F32),
                        pltpu.VMEM((LANES, LANES), F32),
                        pltpu.VMEM((2, LANES, LANES), F32)])
    return pl.pallas_call(
        kern, grid_spec=grid_spec,
        out_shape=[jax.ShapeDtypeStruct((bd * tn, W_B), F32)] * 2,
        compiler_params=_params(("parallel", "arbitrary", "arbitrary")),
        name="nsa_sample",
    )(pt, zq, nsa_new, gb, _compress_weights(w_cmp), expand, *([cache] * pps))


def _win_sample_kernel(q_ref, new_ref, st_ref, gb_ref, o_ref, ns_ref, *, tn, past_len):
    gw = G_B * DH_B
    wb = st_ref.shape[2]
    qs = _sample_queries(q_ref[...], tn)
    st = st_ref[0]
    new_t = _pad_rows(new_ref[...], LANES).T
    pair_r, slope_r, qpos_r = _query_consts(_iota((LANES, 1), 0), tn, past_len)

    def scores(x):
        return jnp.where(pair_r == 0,
                         jnp.dot(qs[0], x[0:LANES, :].astype(BF16), preferred_element_type=F32),
                         jnp.dot(qs[1], x[LANES:2 * LANES, :].astype(BF16), preferred_element_type=F32))

    kpos_s = (past_len - wb) + _iota((1, wb), 1)
    d_s = qpos_r - kpos_s
    s_s = jnp.where((d_s >= 0) & (d_s < WINDOW) & (kpos_s >= 0), scores(st) - slope_r * d_s.astype(F32), NEG_INF)
    col_n = _iota((1, LANES), 1)
    d_n = qpos_r - (past_len + col_n)
    s_n = jnp.where((d_n >= 0) & (d_n < WINDOW) & (col_n < tn), scores(new_t) - slope_r * d_n.astype(F32), NEG_INF)
    s = jnp.concatenate([s_s, s_n], axis=1)
    e = jnp.exp2(s - jnp.max(s, axis=-1, keepdims=True))
    p = (e / jnp.sum(e, axis=-1, keepdims=True)).astype(BF16)
    outs = []
    for q in range(2):
        r0 = gw + q * LANES
        outs.append(_nt_dot(p[:, :wb], st[r0:r0 + LANES, :].astype(BF16))
                    + _nt_dot(p[:, wb:], new_t[r0:r0 + LANES, :].astype(BF16)))
    _sample_outputs(o_ref, outs, jax.nn.sigmoid(gb_ref[...]), 2, tn)
    shifted = pltpu.roll(st, wb - tn, 1)
    tail = jnp.concatenate([jnp.zeros((2 * gw, wb - LANES), F32), pltpu.roll(new_t, LANES - tn, 1)], axis=1)
    ns_ref[0] = jnp.where(_iota((2 * gw, wb), 1) >= wb - tn, tail, shifted)


def _win_sample(zq, win_new, state_t, gb, bd, tn, past_len):
    gw = G_B * DH_B
    wb = state_t.shape[2]
    assert wb == WINDOW and wb == min(WINDOW, past_len) and wb % LANES == 0
    kern = functools.partial(_win_sample_kernel, tn=tn, past_len=past_len)
    return pl.pallas_call(
        kern, grid=(bd,),
        in_specs=[pl.BlockSpec((tn, W_B), lambda bb: (bb, W_A // W_B)),
                  pl.BlockSpec((tn, 2 * gw), lambda bb: (bb, 0)),
                  pl.BlockSpec((1, 2 * gw, wb), lambda bb: (bb, 0, 0)),
                  pl.BlockSpec((tn, LANES), lambda bb: (bb, 0))],
        out_specs=[pl.BlockSpec((tn, W_B), lambda bb: (bb, 0)),
                   pl.BlockSpec((1, 2 * gw, wb), lambda bb: (bb, 0, 0))],
        out_shape=[jax.ShapeDtypeStruct((bd * tn, W_B), F32),
                   jax.ShapeDtypeStruct((bd, 2 * gw, wb), F32)],
        compiler_params=_params(("parallel",)),
        name="win_sample",
    )(zq, win_new, state_t, gb)


def kernel(x_prompt, x_sample, cache_diff_kv, cache_nsa_kv, state_win_kv, page_table, w_in, w_proj_a,
           w_proj_b, w_out, lambda_qk, diff_gain, w_cmp, norm_attn, norm_mlp, w_up, w_down, norm_final):
    depth = w_in.shape[0]
    assert depth == 1
    b, t, d = x_prompt.shape
    bd, tn, _ = x_sample.shape
    l = 0
    lam_init = 0.8 - 0.6 * math.exp(-0.3 * l)
    w = _split_w_in(w_in[l])
    wpa, wpb, wo = w_proj_a[l].astype(BF16), w_proj_b[l].astype(BF16), w_out[l].astype(BF16)
    wup, wdn = w_up[l].astype(BF16), w_down[l].astype(BF16)
    wexp = jnp.repeat(w_cmp[l], DH_B, axis=-1)

    xp = x_prompt.reshape(b * t, d)
    up = _rmsnorm(xp, norm_attn[l], BF16, 512)
    (oa, ocmp, osel, owin, gm), (kva32, nsa32, win32) = _prompt_mixer(
        up, w, lambda_qk[l], diff_gain[l], wexp, b, t, lam_init)
    hp, u2 = _post_attention(oa, ocmp, osel, owin, gm, xp, wpa, wpb, wo, norm_mlp[l], 256)
    y_prompt = _mlp_final(u2, hp, wup, wdn, norm_final, 512, 512).reshape(b, t, d)
    keep = min(WINDOW, t)
    diff_kv_prompt = kva32.reshape(1, b, t, 2, HKV_A, 2 * DH_A)
    nsa_kv_prompt = nsa32.reshape(1, b, t, 4, G_B, DH_B)
    win_kv_prompt = win32.reshape(b, t, 2, G_B, DH_B)[None, :, t - keep:]

    n_pages = page_table.shape[1]
    past_len = n_pages * PAGE_SIZE
    n_pool = cache_diff_kv.shape[1]
    assert R_B * G_B * tn == LANES and tn < CMP_BLOCK and past_len % (8 * PAGE_SIZE) == 0
    pps = 8
    xs = x_sample.reshape(bd * tn, d)
    us = _rmsnorm(xs, norm_attn[l], BF16, bd * tn)
    zq_s, kva_s, _, nsa_s, win_s, gb_s, gm_s = _in_projection(us, w, bd * tn, F32)
    pt = page_table.reshape(-1)
    cache_d = cache_diff_kv[l].reshape(n_pool * PAGE_SIZE * 2 * HKV_A, 2 * DH_A)
    cache_n = jnp.transpose(cache_nsa_kv[l], (0, 2, 3, 4, 1)).reshape(n_pool * 4 * G_B * DH_B, PAGE_SIZE)
    wb = state_win_kv.shape[2]
    state_t = jnp.transpose(state_win_kv[l], (0, 2, 3, 4, 1)).reshape(bd, 2 * G_B * DH_B, wb)
    oa_s = _diff_sample(zq_s, kva_s, cache_d, pt, lambda_qk[l], diff_gain[l], bd, tn, n_pages, pps, lam_init)
    ocmp_s, osel_s = _nsa_sample(zq_s, nsa_s, gb_s, w_cmp[l], cache_n, pt, bd, tn, n_pages, pps)
    owin_s, new_state_t = _win_sample(zq_s, win_s, state_t, gb_s, bd, tn, past_len)
    hs, u2s = _post_attention(oa_s, ocmp_s, osel_s, owin_s, gm_s, xs, wpa, wpb, wo, norm_mlp[l], 256)
    y_sample = _mlp_final(u2s, hs, wup, wdn, norm_final, 512, 512).reshape(bd, tn, d)
    diff_kv_sample = kva_s.reshape(1, bd, tn, 2, HKV_A, 2 * DH_A)
    nsa_kv_sample = nsa_s.reshape(1, bd, tn, 4, G_B, DH_B)
    win_kv_sample = jnp.transpose(new_state_t.reshape(bd, 2, G_B, DH_B, wb), (0, 4, 1, 2, 3))[None]
    return (y_prompt, y_sample, diff_kv_prompt, nsa_kv_prompt, win_kv_prompt,
            diff_kv_sample, nsa_kv_sample, win_kv_sample)
```

```python
import functools
import math

import numpy as np
import jax
import jax.numpy as jnp
from jax import lax
from jax.experimental import pallas as pl
from jax.experimental.pallas import tpu as pltpu

F32, BF16, I32 = jnp.float32, jnp.bfloat16, jnp.int32

PAGE_SIZE = 128
H_A, HKV_A, R_A, DH_A = 8, 4, 2, 64
W_A = H_A * 2 * DH_A
H_B, G_B, R_B, DH_B = 16, 4, 4, 64
W_B = H_B * DH_B
CMP_BLOCK, SEL_BLOCK, TOP_N, WINDOW = 32, 64, 16, 512
EPS = 1e-6
NEG_INF = -1e30
FORCE_SCORE = 1e4
LOG2E = math.log2(math.e)
LANES = 128
HALF = LANES // 2
VMEM_LIMIT = 56 * 1024 * 1024


def _iota(shape, dim):
    return lax.broadcasted_iota(I32, shape, dim)


def _log2(n):
    assert n & (n - 1) == 0, n
    return n.bit_length() - 1


def _nt_dot(a, b):
    return lax.dot_general(a, b, (((1,), (1,)), ((), ())), preferred_element_type=F32)


def _params(sem):
    return pltpu.CompilerParams(dimension_semantics=sem, vmem_limit_bytes=VMEM_LIMIT)


def _tri_tiles(n):
    ii = np.array([i for i in range(n) for _ in range(i + 1)], np.int32)
    jj = np.array([j for i in range(n) for j in range(i + 1)], np.int32)
    return jnp.asarray(ii), jnp.asarray(jj)


def _slopes2(n):
    return jnp.exp2(-8.0 * jnp.arange(1, n + 1, dtype=F32) / n) * LOG2E


def _rmsnorm_kernel(x_ref, g_ref, o_ref):
    x = x_ref[...]
    y = x * lax.rsqrt(jnp.mean(x * x, axis=-1, keepdims=True) + EPS)
    o_ref[...] = (y * g_ref[...]).astype(o_ref.dtype)


def _rmsnorm(x, g, out_dtype, tm):
    m, d = x.shape
    return pl.pallas_call(
        _rmsnorm_kernel,
        grid=(m // tm,),
        in_specs=[pl.BlockSpec((tm, d), lambda i: (i, 0)),
                  pl.BlockSpec((1, d), lambda i: (0, 0))],
        out_specs=pl.BlockSpec((tm, d), lambda i: (i, 0)),
        out_shape=jax.ShapeDtypeStruct((m, d), out_dtype),
        compiler_params=_params(("parallel",)),
        name="rmsnorm",
    )(x, g.reshape(1, d))


def _mm_kernel(a_ref, w_ref, *o_refs):
    acc = jnp.dot(a_ref[...], w_ref[...], preferred_element_type=F32)
    for o in o_refs:
        o[...] = acc.astype(o.dtype)


def _matmul(a, w, out_dtypes, tm, tn, name):
    m, k = a.shape
    n = w.shape[1]
    tm, tn = min(tm, m), min(tn, n)
    outs = pl.pallas_call(
        _mm_kernel,
        grid=(n // tn, m // tm),
        in_specs=[pl.BlockSpec((tm, k), lambda j, i: (i, 0)),
                  pl.BlockSpec((k, tn), lambda j, i: (0, j))],
        out_specs=[pl.BlockSpec((tm, tn), lambda j, i: (i, j)) for _ in out_dtypes],
        out_shape=[jax.ShapeDtypeStruct((m, n), dt) for dt in out_dtypes],
        compiler_params=_params(("parallel", "parallel")),
        name=name,
    )(a, w)
    return outs


def _post_kernel(oa_ref, o1_ref, o2_ref, o3_ref, gm0_ref, gm1_ref, x_ref,
                 wpa_ref, wpb_ref, wo_ref, g_ref, h_ref, u_ref):
    oa = oa_ref[...].astype(BF16)
    ob = (o1_ref[...] + o2_ref[...] + o3_ref[...]).astype(BF16)
    pa = jnp.dot(oa, wpa_ref[...], preferred_element_type=F32)
    pb = jnp.dot(ob, wpb_ref[...], preferred_element_type=F32)
    mix = jax.nn.sigmoid(gm0_ref[...]) * pa + jax.nn.sigmoid(gm1_ref[...]) * pb
    y = jnp.dot(mix.astype(BF16), wo_ref[...], preferred_element_type=F32)
    h = x_ref[...] + y
    h_ref[...] = h
    u = h * lax.rsqrt(jnp.mean(h * h, axis=-1, keepdims=True) + EPS)
    u_ref[...] = (u * g_ref[...]).astype(u_ref.dtype)


def _post_attention(oa, o1, o2, o3, gm, x, wpa, wpb, wo, g_mlp, tm):
    m, d = x.shape
    tm = min(tm, m)
    row = lambda i: (i, 0)
    const = lambda i: (0, 0)
    once = pl.Buffered(1)
    return pl.pallas_call(
        _post_kernel,
        grid=(m // tm,),
        in_specs=[pl.BlockSpec((tm, W_A), row),
                  pl.BlockSpec((tm, W_B), row),
                  pl.BlockSpec((tm, W_B), row),
                  pl.BlockSpec((tm, W_B), row),
                  pl.BlockSpec((tm, d), lambda i: (i, 0)),
                  pl.BlockSpec((tm, d), lambda i: (i, 1)),
                  pl.BlockSpec((tm, d), row),
                  pl.BlockSpec((W_A, d), const, pipeline_mode=once),
                  pl.BlockSpec((W_B, d), const, pipeline_mode=once),
                  pl.BlockSpec((d, d), const, pipeline_mode=once),
                  pl.BlockSpec((1, d), const)],
        out_specs=[pl.BlockSpec((tm, d), row), pl.BlockSpec((tm, d), row)],
        out_shape=[jax.ShapeDtypeStruct((m, d), F32), jax.ShapeDtypeStruct((m, d), BF16)],
        compiler_params=_params(("parallel",)),
        name="post_attention",
    )(oa, o1, o2, o3, gm, gm, x, wpa, wpb, wo, g_mlp.reshape(1, d))


def _mlp_kernel(u_ref, h_ref, wup_ref, wdn_ref, g_ref, o_ref, acc_ref):
    f = pl.program_id(1)

    @pl.when(f == 0)
    def _():
        acc_ref[...] = jnp.zeros_like(acc_ref)

    a = jnp.dot(u_ref[...], wup_ref[...], preferred_element_type=F32)
    a = jnp.square(jnp.maximum(a, 0.0)).astype(BF16)
    acc_ref[...] += jnp.dot(a, wdn_ref[...], preferred_element_type=F32)

    @pl.when(f == pl.num_programs(1) - 1)
    def _():
        y = h_ref[...] + acc_ref[...]
        y = y * lax.rsqrt(jnp.mean(y * y, axis=-1, keepdims=True) + EPS)
        o_ref[...] = y * g_ref[...]


def _mlp_final(u, h, wup, wdn, g_final, tm, tf):
    m, d = h.shape
    dff = wup.shape[1]
    tm, tf = min(tm, m), min(tf, dff)
    return pl.pallas_call(
        _mlp_kernel,
        grid=(m // tm, dff // tf),
        in_specs=[pl.BlockSpec((tm, d), lambda i, f: (i, 0)),
                  pl.BlockSpec((tm, d), lambda i, f: (i, 0)),
                  pl.BlockSpec((d, tf), lambda i, f: (0, f)),
                  pl.BlockSpec((tf, d), lambda i, f: (f, 0)),
                  pl.BlockSpec((1, d), lambda i, f: (0, 0))],
        out_specs=pl.BlockSpec((tm, d), lambda i, f: (i, 0)),
        out_shape=jax.ShapeDtypeStruct((m, d), F32),
        scratch_shapes=[pltpu.VMEM((tm, d), F32)],
        compiler_params=_params(("parallel", "arbitrary")),
        name="mlp_final",
    )(u, h, wup, wdn, g_final.reshape(1, d))


def _flash_init(m_ref, l_ref, acc_ref):
    m_ref[...] = jnp.full(m_ref.shape, NEG_INF, F32)
    l_ref[...] = jnp.zeros(l_ref.shape, F32)
    acc_ref[...] = jnp.zeros(acc_ref.shape, F32)


def _lane_blocks_sum(p):
    out = p[:, 0:LANES]
    for n in range(1, p.shape[1] // LANES):
        out = out + p[:, n * LANES:(n + 1) * LANES]
    return out


def _flash_step(s, shift, m_prev, l_prev, acc_prev, pv_fn):
    m_cur = jnp.max(s, axis=-1, keepdims=True)
    if shift is not None:
        m_cur = m_cur - shift
    m_new = jnp.maximum(m_prev, m_cur)
    alpha = jnp.exp2(m_prev - m_new)
    m_adj = m_new if shift is None else m_new + shift
    p = jnp.exp2(s - jnp.concatenate([m_adj] * (s.shape[1] // LANES), axis=1))
    l_new = alpha * l_prev + _lane_blocks_sum(p)
    acc_new = alpha * acc_prev + pv_fn(p.astype(BF16))
    return m_new, l_new, acc_new


def _flash_rows(rs, s, shift, v, m_ref, l_ref, acc_ref):
    m, l, acc = _flash_step(s, shift, m_ref[rs, :], l_ref[rs, :], acc_ref[rs, :],
                            lambda p: jnp.dot(p, v, preferred_element_type=F32))
    m_ref[rs, :] = m
    l_ref[rs, :] = l
    acc_ref[rs, :] = acc


def _row_total(l):
    return jnp.sum(l, axis=-1, keepdims=True)


def _alibi_tiles(slopes, tq):
    ng, nr = slopes.shape
    d = (jnp.arange(tq)[:, None] - jnp.arange(tq)[None, :]).astype(F32)
    off = -slopes[:, :, None, None] * d
    tiles = jnp.stack([off, jnp.where(d >= 0, off, NEG_INF)], axis=1).reshape(ng, 2, nr * tq, tq)
    step = jnp.broadcast_to((slopes * tq)[:, :, None, None], (ng, nr, tq, LANES)).reshape(ng, nr * tq, LANES)
    return tiles, step


def _window_tiles(slopes, tq, nj):
    ng, nr = slopes.shape
    d = jnp.arange(tq)[:, None] - jnp.arange(tq)[None, :]
    dist = (nj - 1 - jnp.arange(nj))[:, None, None] * tq + d[None]
    dist = dist[None, :, None]
    bias = jnp.where((dist >= 0) & (dist < WINDOW),
                     -slopes[:, None, :, None, None] * dist.astype(F32), NEG_INF)
    return bias.reshape(ng, nj, nr * tq, tq)


def _diff_lambda(lam_ref, lam_init):
    lf = lam_ref[...]
    e1 = jnp.exp(jnp.sum(lf[0:1] * lf[1:2], axis=-1, keepdims=True))
    e2 = jnp.exp(jnp.sum(lf[2:3] * lf[3:4], axis=-1, keepdims=True))
    return e1 - e2 + lam_init


def _diff_finish(o1, o2, lam, gain, lam_init):
    o = o1 - lam * o2
    y = o * lax.rsqrt(jnp.mean(o * o, axis=-1, keepdims=True) + EPS)
    return (y * gain) * (1.0 - lam_init)


def _diff_prompt_kernel(ii_ref, jj_ref, q_ref, k_ref, v_ref, tile_ref, step_ref, lam_ref, gain_ref, o_ref,
                        qs_ref, m_ref, l_ref, acc_ref, *, tq, lam_init):
    t = pl.program_id(2)
    i = ii_ref[t]
    j = jj_ref[t]
    nblk = 2 * R_A

    @pl.when(j == 0)
    def _():
        upper = _iota((tq, LANES), 1) >= HALF
        for r in range(R_A):
            qv = q_ref[:, r * LANES:(r + 1) * LANES]
            for c in range(2):
                qs_ref[(2 * r + c) * tq:(2 * r + c + 1) * tq, :] = jnp.where(upper == bool(c), qv, 0).astype(BF16)
        _flash_init(m_ref, l_ref, acc_ref)

    diag = (i == j).astype(I32)
    steps = (i - j).astype(F32)
    k = k_ref[...]
    v = v_ref[...]
    for n in range(nblk):
        rs = slice(n * tq, (n + 1) * tq)
        s = _nt_dot(qs_ref[rs, :], k) + tile_ref[0, diag, rs, :]
        _flash_rows(rs, s, step_ref[0, rs, :] * steps, v, m_ref, l_ref, acc_ref)

    @pl.when(j == i)
    def _():
        lam = _diff_lambda(lam_ref, lam_init)
        for r in range(R_A):
            b0 = 2 * r * tq
            o1 = acc_ref[b0:b0 + tq, :] / _row_total(l_ref[b0:b0 + tq, :])
            o2 = acc_ref[b0 + tq:b0 + 2 * tq, :] / _row_total(l_ref[b0 + tq:b0 + 2 * tq, :])
            y = _diff_finish(o1, o2, lam, gain_ref[...], lam_init)
            o_ref[:, r * LANES:(r + 1) * LANES] = y.astype(o_ref.dtype)


def _diff_prompt(qa, kva, lam_qk, gain, b, t, tq, lam_init):
    nq = t // tq
    ii, jj = _tri_tiles(nq)
    rows = 2 * R_A * tq
    slopes = jnp.repeat(_slopes2(H_A).reshape(HKV_A, R_A), 2, axis=1)
    tiles, step = _alibi_tiles(slopes, tq)
    kern = functools.partial(_diff_prompt_kernel, tq=tq, lam_init=lam_init)
    grid_spec = pltpu.PrefetchScalarGridSpec(
        num_scalar_prefetch=2,
        grid=(b, HKV_A, ii.shape[0]),
        in_specs=[pl.BlockSpec((tq, R_A * LANES), lambda bb, h, n, ii, jj: (bb * nq + ii[n], h)),
                  pl.BlockSpec((tq, LANES), lambda bb, h, n, ii, jj: (bb * nq + jj[n], h)),
                  pl.BlockSpec((tq, LANES), lambda bb, h, n, ii, jj: (bb * nq + jj[n], HKV_A + h)),
                  pl.BlockSpec((1, 2, rows, tq), lambda bb, h, n, ii, jj: (h, 0, 0, 0)),
                  pl.BlockSpec((1, rows, LANES), lambda bb, h, n, ii, jj: (h, 0, 0)),
                  pl.BlockSpec((4, DH_A), lambda bb, h, n, ii, jj: (0, 0)),
                  pl.BlockSpec((1, 2 * DH_A), lambda bb, h, n, ii, jj: (0, 0))],
        out_specs=pl.BlockSpec((tq, R_A * LANES), lambda bb, h, n, ii, jj: (bb * nq + ii[n], h)),
        scratch_shapes=[pltpu.VMEM((rows, LANES), BF16),
                        pltpu.VMEM((rows, LANES), F32),
                        pltpu.VMEM((rows, LANES), F32),
                        pltpu.VMEM((rows, LANES), F32)])
    return pl.pallas_call(
        kern, grid_spec=grid_spec,
        out_shape=jax.ShapeDtypeStruct((b * t, W_A), BF16),
        compiler_params=_params(("parallel", "parallel", "arbitrary")),
        name="diff_prompt",
    )(ii, jj, qa, kva, kva, tiles, step, lam_qk, gain.reshape(1, 2 * DH_A))


def _dup_half(v, g):
    rolled = pltpu.roll(v, HALF, 1)
    low = _iota(v.shape, 1) < HALF
    return jnp.where(low, v, rolled) if g % 2 == 0 else jnp.where(low, rolled, v)


def _prep_kernel(nsa_ref, win_ref, wexp_ref, kcd_ref, vcd_ref, ksa_ref, vsd_ref, kwd_ref, vwd_ref,
                 tmp_ref, *, tp, nc):
    step = pl.program_id(1)
    gw = G_B * DH_B
    x = nsa_ref[...]
    w = win_ref[...]
    lane = _iota((tp, LANES), 1)
    low = lane < HALF
    blk = (step * tp + _iota((tp, LANES), 0)) >> _log2(SEL_BLOCK)
    onehot = jnp.where(lane - HALF == blk, 1.0, 0.0)
    for g in range(G_B):
        c0 = (g // 2) * LANES
        vk = x[:, 2 * gw + c0:2 * gw + c0 + LANES]
        klow = vk if g % 2 == 0 else pltpu.roll(vk, HALF, 1)
        ksa_ref[0, g] = jnp.where(low, klow, onehot).astype(BF16)
        vsd_ref[0, g] = _dup_half(x[:, 3 * gw + c0:3 * gw + c0 + LANES], g).astype(BF16)
        kwd_ref[0, g] = _dup_half(w[:, c0:c0 + LANES], g).astype(BF16)
        vwd_ref[0, g] = _dup_half(w[:, gw + c0:gw + c0 + LANES], g).astype(BF16)
    nb = tp // CMP_BLOCK
    hb = nb // 2
    for slot, dst in ((0, kcd_ref), (1, vcd_ref)):
        xc = x[:, slot * gw:(slot + 1) * gw]
        comp = jnp.sum(xc.reshape(nb, CMP_BLOCK, gw) * wexp_ref[slot][None], axis=1)
        for c in range(gw // LANES):
            tmp_ref[c * nb:(c + 1) * nb, :] = comp[:, c * LANES:(c + 1) * LANES]
        for par in range(2):
            start = pl.multiple_of(par * (nc // 2) + step * hb, 8)
            for g in range(G_B):
                rr = tmp_ref[pl.ds((g // 2) * nb + par, hb, stride=2), :]
                dst[0, g, pl.ds(start, hb), :] = _dup_half(rr, g)


def _nsa_prep(nsa, win, wexp, b, t, tp):
    nc = t // CMP_BLOCK
    gw = G_B * DH_B
    kern = functools.partial(_prep_kernel, tp=tp, nc=nc)
    small = jax.ShapeDtypeStruct((b, G_B, nc, LANES), F32)
    big = jax.ShapeDtypeStruct((b, G_B, t, LANES), BF16)
    small_spec = pl.BlockSpec((1, G_B, nc, LANES), lambda bb, s: (bb, 0, 0, 0))
    big_spec = pl.BlockSpec((1, G_B, tp, LANES), lambda bb, s: (bb, 0, s, 0))
    nt = t // tp
    return pl.pallas_call(
        kern, grid=(b, nt),
        in_specs=[pl.BlockSpec((tp, 4 * gw), lambda bb, s: (bb * nt + s, 0)),
                  pl.BlockSpec((tp, 2 * gw), lambda bb, s: (bb * nt + s, 0)),
                  pl.BlockSpec((2, CMP_BLOCK, gw), lambda bb, s: (0, 0, 0))],
        out_specs=[small_spec, small_spec, big_spec, big_spec, big_spec, big_spec],
        out_shape=[small, small, big, big, big, big],
        scratch_shapes=[pltpu.VMEM((gw // LANES * (tp // CMP_BLOCK), LANES), F32)],
        compiler_params=_params(("parallel", "arbitrary")),
        name="nsa_prep",
    )(nsa, win, wexp)


def _topk_bias(sc, sidx):
    rank = jnp.zeros(sc.shape, I32)
    for c in range(sc.shape[0]):
        other = sc[c:c + 1, :]
        beats = (other > sc) | ((other == sc) & (sidx > c))
        rank = rank + beats.astype(I32)
    return jnp.where(rank < TOP_N, 0.0, NEG_INF)


def _cmp_prompt_kernel(q_ref, kcd_ref, vcd_ref, gb_ref, ocmp_ref, qaug_ref, *, tq, nc):
    i = pl.program_id(1)
    ns = nc // 2
    lowq = _iota((tq, LANES), 1) < HALF
    rowc = _iota((nc, tq), 0)
    qpos = i * tq + _iota((nc, tq), 1)
    blk = jnp.where(rowc >= ns, 2 * (rowc - ns) + 1, 2 * rowc)
    c_mid = (blk * CMP_BLOCK).astype(F32) + (CMP_BLOCK - 1) * 0.5
    valid = blk * CMP_BLOCK + (CMP_BLOCK - 1) <= qpos
    adist = jnp.abs(qpos.astype(F32) - c_mid)
    sig = jax.nn.sigmoid(gb_ref[...])
    sidx = _iota((ns, tq), 0)
    qp = i * tq + _iota((ns, tq), 1)
    forced = (sidx == (qp >> _log2(SEL_BLOCK))) | (sidx == 0)
    valid_s = sidx * SEL_BLOCK <= qp
    for g in range(G_B):
        kc = kcd_ref[0, g].astype(BF16)
        vc = vcd_ref[0, g].astype(BF16)
        imp = jnp.zeros((nc, tq), F32)
        outs, qlows = [], []
        for r in range(R_B):
            h = g * R_B + r
            qv = q_ref[:, (h // 2) * LANES:(h // 2 + 1) * LANES]
            qm = jnp.where(lowq == (h % 2 == 0), qv, 0).astype(BF16)
            s = _nt_dot(kc, qm)
            s = s - (2.0 ** (-8.0 * (h + 1) / H_B) * LOG2E) * adist
            sm = jnp.where(valid, s, NEG_INF)
            e = jnp.exp2(sm - jnp.max(sm, axis=0, keepdims=True))
            p = jnp.where(valid, e / jnp.sum(e, axis=0, keepdims=True), 0.0)
            imp = imp + p
            o2 = jnp.dot(p.T.astype(BF16), vc, preferred_element_type=F32)
            outs.append(o2 * sig[:, h:h + 1])
            qlows.append(qv if h % 2 == 0 else pltpu.roll(qv, HALF, 1))
        for k in range(R_B // 2):
            c0 = (g * (R_B // 2) + k) * LANES
            ocmp_ref[:, c0:c0 + LANES] = jnp.where(lowq, outs[2 * k], outs[2 * k + 1])
        sc = jnp.where(forced, FORCE_SCORE, jnp.where(valid_s, imp[:ns] + imp[ns:], -1.0))
        bias_t = _topk_bias(sc, sidx)
        bias = jnp.concatenate([bias_t] * (LANES // ns), axis=0).T.astype(BF16)
        for r in range(R_B):
            qaug_ref[0, g, r] = jnp.where(lowq, qlows[r], bias).astype(BF16)


def _cmp_prompt(zq, kcd, vcd, gb, b, t, tq):
    nc = t // CMP_BLOCK
    nq = t // tq
    kern = functools.partial(_cmp_prompt_kernel, tq=tq, nc=nc)
    small_spec = pl.BlockSpec((1, G_B, nc, LANES), lambda bb, i: (bb, 0, 0, 0))
    return pl.pallas_call(
        kern, grid=(b, nq),
        in_specs=[pl.BlockSpec((tq, W_B), lambda bb, i: (bb * nq + i, W_A // W_B)),
                  small_spec, small_spec,
                  pl.BlockSpec((tq, LANES), lambda bb, i: (bb * nq + i, 0))],
        out_specs=[pl.BlockSpec((tq, W_B), lambda bb, i: (bb * nq + i, 0)),
                   pl.BlockSpec((1, G_B, R_B, tq, LANES), lambda bb, i: (bb, 0, 0, i, 0))],
        out_shape=[jax.ShapeDtypeStruct((b * t, W_B), F32),
                   jax.ShapeDtypeStruct((b, G_B, R_B, t, LANES), BF16)],
        compiler_params=_params(("parallel", "parallel")),
        name="cmp_prompt",
    )(zq, kcd, vcd, gb)


def _gate_column(sig, col):
    return jnp.sum(jnp.where(_iota(sig.shape, 1) == col, sig, 0.0), axis=-1, keepdims=True)


def _group_finish(o_ref, gb_ref, l_ref, acc_ref, g, branch, tq):
    sig = jax.nn.sigmoid(gb_ref[...])
    low = _iota((tq, LANES), 1) < HALF
    outs = []
    for r in range(R_B):
        rs = slice(r * tq, (r + 1) * tq)
        o = acc_ref[rs, :] / _row_total(l_ref[rs, :])
        outs.append(o * _gate_column(sig, branch * H_B + g * R_B + r))
    for k in range(R_B // 2):
        o_ref[:, k * LANES:(k + 1) * LANES] = jnp.where(low, outs[2 * k], outs[2 * k + 1])


def _sel_prompt_kernel(ii_ref, jj_ref, qa_ref, k_ref, v_ref, tile_ref, step_ref, gb_ref, o_ref,
                       m_ref, l_ref, acc_ref, *, tq):
    g = pl.program_id(1)
    t = pl.program_id(2)
    i = ii_ref[t]
    j = jj_ref[t]

    @pl.when(j == 0)
    def _():
        _flash_init(m_ref, l_ref, acc_ref)

    diag = (i == j).astype(I32)
    steps = (i - j).astype(F32)
    k = k_ref[0, 0]
    v = v_ref[0, 0]
    for r in range(R_B):
        rs = slice(r * tq, (r + 1) * tq)
        s = _nt_dot(qa_ref[0, 0, r], k) + tile_ref[0, diag, rs, :]
        _flash_rows(rs, s, step_ref[0, rs, :] * steps, v, m_ref, l_ref, acc_ref)

    @pl.when(j == i)
    def _():
        _group_finish(o_ref, gb_ref, l_ref, acc_ref, g, 1, tq)


def _sel_prompt(qaug, ksa, vsd, gb, b, t, tq):
    nq = t // tq
    ii, jj = _tri_tiles(nq)
    rows = R_B * tq
    tiles, step = _alibi_tiles(_slopes2(H_B).reshape(G_B, R_B), tq)
    kern = functools.partial(_sel_prompt_kernel, tq=tq)
    kv_spec = pl.BlockSpec((1, 1, tq, LANES), lambda bb, g, n, ii, jj: (bb, g, jj[n], 0))
    grid_spec = pltpu.PrefetchScalarGridSpec(
        num_scalar_prefetch=2,
        grid=(b, G_B, ii.shape[0]),
        in_specs=[pl.BlockSpec((1, 1, R_B, tq, LANES), lambda bb, g, n, ii, jj: (bb, g, 0, ii[n], 0)),
                  kv_spec, kv_spec,
                  pl.BlockSpec((1, 2, rows, tq), lambda bb, g, n, ii, jj: (g, 0, 0, 0)),
                  pl.BlockSpec((1, rows, LANES), lambda bb, g, n, ii, jj: (g, 0, 0)),
                  pl.BlockSpec((tq, LANES), lambda bb, g, n, ii, jj: (bb * nq + ii[n], 0))],
        out_specs=pl.BlockSpec((tq, R_B * DH_B), lambda bb, g, n, ii, jj: (bb * nq + ii[n], g)),
        scratch_shapes=[pltpu.VMEM((rows, LANES), F32),
                        pltpu.VMEM((rows, LANES), F32),
                        pltpu.VMEM((rows, LANES), F32)])
    return pl.pallas_call(
        kern, grid_spec=grid_spec,
        out_shape=jax.ShapeDtypeStruct((b * t, W_B), F32),
        compiler_params=_params(("parallel", "parallel", "arbitrary")),
        name="sel_prompt",
    )(ii, jj, qaug, ksa, vsd, tiles, step, gb)


def _win_prompt_kernel(q_ref, k_ref, v_ref, tile_ref, gb_ref, o_ref, qs_ref, m_ref, l_ref, acc_ref, *, tq, nj):
    g = pl.program_id(1)
    i = pl.program_id(2)
    j = pl.program_id(3)

    @pl.when(j == 0)
    def _():
        low = _iota((tq, LANES), 1) < HALF
        for r in range(R_B):
            qv = q_ref[:, (r // 2) * LANES:(r // 2 + 1) * LANES]
            qs_ref[r * tq:(r + 1) * tq, :] = jnp.where(low == (r % 2 == 0), qv, 0).astype(BF16)
        _flash_init(m_ref, l_ref, acc_ref)

    @pl.when(i - (nj - 1) + j >= 0)
    def _():
        k = k_ref[0, 0]
        v = v_ref[0, 0]
        for r in range(R_B):
            rs = slice(r * tq, (r + 1) * tq)
            s = _nt_dot(qs_ref[rs, :], k) + tile_ref[0, j, rs, :]
            _flash_rows(rs, s, None, v, m_ref, l_ref, acc_ref)

    @pl.when(j == nj - 1)
    def _():
        _group_finish(o_ref, gb_ref, l_ref, acc_ref, g, 2, tq)


def _win_prompt(zq, kwd, vwd, gb, b, t, tq):
    nq = t // tq
    nj = -(-WINDOW // tq) + 1
    qoff = W_A // (R_B * DH_B)
    rows = R_B * tq
    tiles = _window_tiles(_slopes2(H_B).reshape(G_B, R_B), tq, nj)
    kern = functools.partial(_win_prompt_kernel, tq=tq, nj=nj)
    kv_spec = pl.BlockSpec((1, 1, tq, LANES),
                           lambda bb, g, i, j: (bb, g, jnp.maximum(i - (nj - 1) + j, 0), 0))
    return pl.pallas_call(
        kern, grid=(b, G_B, nq, nj),
        in_specs=[pl.BlockSpec((tq, R_B * DH_B), lambda bb, g, i, j: (bb * nq + i, qoff + g)),
                  kv_spec, kv_spec,
                  pl.BlockSpec((1, nj, rows, tq), lambda bb, g, i, j: (g, 0, 0, 0)),
                  pl.BlockSpec((tq, LANES), lambda bb, g, i, j: (bb * nq + i, 0))],
        out_specs=pl.BlockSpec((tq, R_B * DH_B), lambda bb, g, i, j: (bb * nq + i, g)),
        out_shape=jax.ShapeDtypeStruct((b * t, W_B), F32),
        scratch_shapes=[pltpu.VMEM((rows, LANES), BF16),
                        pltpu.VMEM((rows, LANES), F32),
                        pltpu.VMEM((rows, LANES), F32),
                        pltpu.VMEM((rows, LANES), F32)],
        compiler_params=_params(("parallel", "parallel", "parallel", "arbitrary")),
        name="win_prompt",
    )(zq, kwd, vwd, tiles, gb)


def _split_w_in(w_in):
    sizes = (W_A, HKV_A * 2 * DH_A, HKV_A * 2 * DH_A, W_B, 6 * G_B * DH_B, 3 * H_B, 2 * w_in.shape[0])
    offs = np.concatenate([[0], np.cumsum(sizes)])
    seg = lambda a, b_: w_in[:, a:b_]
    gw = G_B * DH_B
    return dict(
        q=jnp.concatenate([seg(offs[0], offs[1]) * (DH_A ** -0.5 * LOG2E),
                           seg(offs[3], offs[4]) * (DH_B ** -0.5 * LOG2E)], axis=1).astype(BF16),
        kva=seg(offs[1], offs[3]).astype(BF16),
        nsa=seg(offs[4], offs[4] + 4 * gw).astype(BF16),
        win=seg(offs[4] + 4 * gw, offs[5]).astype(BF16),
        gb=jnp.pad(seg(offs[5], offs[6]), ((0, 0), (0, LANES - 3 * H_B))).astype(BF16),
        gm=seg(offs[6], offs[7]).astype(BF16))


def _in_projection(u, w, tm, q_dtype=BF16):
    zq, = _matmul(u, w["q"], (q_dtype,), tm, 512, "proj_q")
    kva32, kva16 = _matmul(u, w["kva"], (F32, BF16), tm, 512, "proj_kva")
    nsa32, = _matmul(u, w["nsa"], (F32,), tm, 512, "proj_nsa")
    win32, = _matmul(u, w["win"], (F32,), tm, 512, "proj_win")
    gb, = _matmul(u, w["gb"], (F32,), tm, LANES, "proj_gb")
    gm, = _matmul(u, w["gm"], (F32,), tm, 512, "proj_gm")
    return zq, kva32, kva16, nsa32, win32, gb, gm


def _prompt_mixer(u, w, lam_qk, gain, wexp, b, t, lam_init, tq=256):
    tq = min(tq, t)
    zq, kva32, kva16, nsa32, win32, gb, gm = _in_projection(u, w, 1024)
    tbig = min(2 * tq, t)
    oa = _diff_prompt(zq, kva16, lam_qk, gain, b, t, tbig, lam_init)
    kcd, vcd, ksa, vsd, kwd, vwd = _nsa_prep(nsa32, win32, wexp, b, t, min(512, t))
    ocmp, qaug = _cmp_prompt(zq, kcd, vcd, gb, b, t, tq)
    osel = _sel_prompt(qaug, ksa, vsd, gb, b, t, tbig)
    owin = _win_prompt(zq, kwd, vwd, gb, b, t, tbig)
    return (oa, ocmp, osel, owin, gm), (kva32, nsa32, win32)


def _pad_rows(x, rows):
    return jnp.concatenate([x, jnp.zeros((rows - x.shape[0], x.shape[1]), x.dtype)], axis=0)


def _diff_sample_kernel(pt_ref, q_ref, new_ref, lam_ref, gain_ref, *rest, tn, pps, past_len, lam_init):
    page_refs = rest[:pps]
    o_ref, qs_ref, m_ref, l_ref, acc_ref = rest[pps:]
    c = pl.program_id(1)
    rows = 2 * R_A * tn
    hw = HKV_A * 2 * DH_A
    stride = 2 * HKV_A
    row = _iota((rows, 1), 0)
    tq = row & (tn - 1)

    def slopes(hkv):
        head = hkv * R_A + (row >> _log2(2 * tn))
        return jnp.exp2(-8.0 * (head + 1).astype(F32) / H_A) * LOG2E

    def update(hkv, s, pv_fn):
        m, l, acc = _flash_step(s, None, m_ref[hkv], l_ref[hkv], acc_ref[hkv], pv_fn)
        m_ref[hkv] = m
        l_ref[hkv] = l
        acc_ref[hkv] = acc

    @pl.when(c == 0)
    def _():
        upper = _iota((tn, LANES), 1) >= HALF
        for hkv in range(HKV_A):
            parts = []
            for r in range(R_A):
                h = hkv * R_A + r
                qv = q_ref[:, h * LANES:(h + 1) * LANES]
                parts += [jnp.where(upper == bool(cc), qv, 0.0) for cc in range(2)]
            qs_ref[hkv] = jnp.concatenate(parts, axis=0).astype(BF16)
        _flash_init(m_ref, l_ref, acc_ref)

    kpos = c * (pps * PAGE_SIZE) + _iota((1, pps * PAGE_SIZE), 1)
    dist = ((past_len + tq) - kpos).astype(F32)
    scores = []
    for hkv in range(HKV_A):
        q = qs_ref[hkv]
        s = jnp.concatenate(
            [_nt_dot(q, pr[pl.ds(hkv, PAGE_SIZE, stride=stride), :].astype(BF16)) for pr in page_refs], axis=1)
        scores.append(s - slopes(hkv) * dist)
    for hkv in range(HKV_A):

        def pv(p, hkv=hkv):
            out = None
            for n, pr in enumerate(page_refs):
                vals = pr[pl.ds(HKV_A + hkv, PAGE_SIZE, stride=stride), :].astype(BF16)
                part = jnp.dot(p[:, n * PAGE_SIZE:(n + 1) * PAGE_SIZE], vals, preferred_element_type=F32)
                out = part if out is None else out + part
            return out

        update(hkv, scores[hkv], pv)

    @pl.when(c == pl.num_programs(1) - 1)
    def _():
        lam = _diff_lambda(lam_ref, lam_init)
        newp = _pad_rows(new_ref[...], LANES)
        dnew = tq - _iota((1, LANES), 1)
        for hkv in range(HKV_A):
            c0 = hkv * LANES
            s = _nt_dot(qs_ref[hkv], newp[:, c0:c0 + LANES].astype(BF16))
            s = jnp.where(dnew >= 0, s - slopes(hkv) * dnew.astype(F32), NEG_INF)
            vn = newp[:, hw + c0:hw + c0 + LANES].astype(BF16)
            update(hkv, s, lambda p, vn=vn: jnp.dot(p, vn, preferred_element_type=F32))
            o = acc_ref[hkv] / _row_total(l_ref[hkv])
            for r in range(R_A):
                b0 = 2 * r * tn
                y = _diff_finish(o[b0:b0 + tn], o[b0 + tn:b0 + 2 * tn], lam, gain_ref[...], lam_init)
                h = hkv * R_A + r
                o_ref[:, h * LANES:(h + 1) * LANES] = y


def _diff_sample(zq, kva_new, cache, pt, lam_qk, gain, bd, tn, n_pages, pps, lam_init):
    rows = 2 * R_A * tn
    page_rows = PAGE_SIZE * 2 * HKV_A
    kern = functools.partial(_diff_sample_kernel, tn=tn, pps=pps, past_len=n_pages * PAGE_SIZE,
                             lam_init=lam_init)

    def page_spec(p):
        return pl.BlockSpec((page_rows, LANES), lambda bb, c, pt: (pt[bb * n_pages + c * pps + p], 0))

    grid_spec = pltpu.PrefetchScalarGridSpec(
        num_scalar_prefetch=1,
        grid=(bd, n_pages // pps),
        in_specs=[pl.BlockSpec((tn, W_A), lambda bb, c, pt: (bb, 0)),
                  pl.BlockSpec((tn, W_A), lambda bb, c, pt: (bb, 0)),
                  pl.BlockSpec((4, DH_A), lambda bb, c, pt: (0, 0)),
                  pl.BlockSpec((1, 2 * DH_A), lambda bb, c, pt: (0, 0))]
                 + [page_spec(p) for p in range(pps)],
        out_specs=pl.BlockSpec((tn, W_A), lambda bb, c, pt: (bb, 0)),
        scratch_shapes=[pltpu.VMEM((HKV_A, rows, LANES), BF16),
                        pltpu.VMEM((HKV_A, rows, LANES), F32),
                        pltpu.VMEM((HKV_A, rows, LANES), F32),
                        pltpu.VMEM((HKV_A, rows, LANES), F32)])
    return pl.pallas_call(
        kern, grid_spec=grid_spec,
        out_shape=jax.ShapeDtypeStruct((bd * tn, W_A), F32),
        compiler_params=_params(("parallel", "arbitrary")),
        name="diff_sample",
    )(pt, zq, kva_new, lam_qk, gain.reshape(1, 2 * DH_A), *([cache] * pps))


def _sample_queries(qblk, tn):
    low = _iota((tn, LANES), 1) < HALF
    zero = jnp.zeros((tn, LANES), F32)
    mats = ([], [])
    for r in range(R_B):
        for g in range(G_B):
            h = g * R_B + r
            qv = qblk[:, (h // 2) * LANES:(h // 2 + 1) * LANES]
            if h % 2 != g % 2:
                qv = pltpu.roll(qv, HALF, 1)
            mats[g // 2].append(jnp.where(low == (g % 2 == 0), qv, 0.0))
            mats[1 - g // 2].append(zero)
    return [jnp.concatenate(m, axis=0).astype(BF16) for m in mats]


def _sample_outputs(o_ref, o_pairs, sig, branch, tn):
    low = _iota((tn, LANES), 1) < HALF
    for g in range(G_B):
        for k in range(R_B // 2):
            parts = []
            for r in (2 * k, 2 * k + 1):
                r0 = (r * G_B + g) * tn
                src = o_pairs[g // 2][r0:r0 + tn, :]
                if g % 2 != r % 2:
                    src = pltpu.roll(src, HALF, 1)
                col = branch * H_B + g * R_B + r
                parts.append(src * sig[:, col:col + 1])
            c0 = (g * (R_B // 2) + k) * LANES
            o_ref[:, c0:c0 + LANES] = jnp.where(low, parts[0], parts[1])


def _query_consts(idx, tn, past_len):
    r_q = idx >> _log2(G_B * tn)
    g_q = (idx >> _log2(tn)) & (G_B - 1)
    slope = jnp.exp2(-8.0 * (g_q * R_B + r_q + 1).astype(F32) / H_B) * LOG2E
    return g_q >> 1, slope, past_len + (idx & (tn - 1))


def _topk_member(sc):
    ridx = _iota(sc.shape, 0)
    removed = -3.0e38

    def body(_, carry):
        cur, mem = carry
        top = jnp.max(cur, axis=0, keepdims=True)
        first = jnp.min(jnp.where(cur == top, ridx, sc.shape[0]), axis=0, keepdims=True)
        pick = ridx == first
        return jnp.where(pick, removed, cur), jnp.where(pick, 1.0, mem)

    return lax.fori_loop(0, TOP_N, body, (sc, jnp.zeros(sc.shape, F32)), unroll=True)[1]


def _col_of(row_vec):
    return jnp.broadcast_to(row_vec, (LANES, LANES)).T


def _nsa_sample_kernel(pt_ref, q_ref, new_ref, gb_ref, wc_ref, e_ref, *rest, tn, pps, past_len):
    page_refs = rest[:pps]
    (ocmp_ref, osel_ref, qop_ref, kcvc_ref, imp_ref, sc_ref, selb_ref,
     m_ref, l_ref, acc_ref) = rest[pps:]
    ph = pl.program_id(1)
    c = pl.program_id(2)
    last = pl.num_programs(2) - 1
    gw = G_B * DH_B
    ncp = past_len // CMP_BLOCK
    nsp = past_len // SEL_BLOCK
    ck = pps * PAGE_SIZE
    lane_consts = lambda: _query_consts(_iota((1, LANES), 1), tn, past_len)
    row_consts = lambda: _query_consts(_iota((LANES, 1), 0), tn, past_len)

    @pl.when((ph == 0) & (c == 0))
    def _():
        qs = _sample_queries(q_ref[...], tn)
        qop_ref[0] = qs[0]
        qop_ref[1] = qs[1]

    @pl.when(ph == 0)
    def _():
        wc = wc_ref[...].astype(BF16)
        low8 = _iota((8, LANES), 1) < HALF
        comps = [_nt_dot(wc, jnp.concatenate([page_refs[2 * pp][...], page_refs[2 * pp + 1][...]],
                                             axis=1).astype(BF16)) for pp in range(pps // 2)]
        for pp, comp in enumerate(comps):
            row0 = pl.multiple_of((c * (pps // 2) + pp) * 8, 8)
            for slot in range(2):
                for p in range(2):
                    va = slot * G_B + 2 * p
                    c0 = slot * gw + p * LANES
                    piece = jnp.where(low8, comp[va * 8:(va + 1) * 8, c0:c0 + LANES],
                                      comp[(va + 1) * 8:(va + 2) * 8, c0:c0 + LANES])
                    kcvc_ref[pl.ds(row0, 8), c0:c0 + LANES] = piece

    @pl.when((ph == 0) & (c == last))
    def _():
        pair_l, slope_l, qpos_l = lane_consts()
        kcvc = kcvc_ref[...]
        s = jnp.where(pair_l == 0,
                      _nt_dot(kcvc[:, 0:LANES].astype(BF16), qop_ref[0]),
                      _nt_dot(kcvc[:, LANES:2 * LANES].astype(BF16), qop_ref[1]))
        blk = _iota((ncp, LANES), 0)
        c_mid = (blk * CMP_BLOCK).astype(F32) + (CMP_BLOCK - 1) * 0.5
        valid = blk * CMP_BLOCK + (CMP_BLOCK - 1) <= qpos_l
        s = s - slope_l * jnp.abs(qpos_l.astype(F32) - c_mid)
        sm = jnp.where(valid, s, NEG_INF)
        e = jnp.exp2(sm - jnp.max(sm, axis=0, keepdims=True))
        p_t = jnp.where(valid, e / jnp.sum(e, axis=0, keepdims=True), 0.0)
        p = p_t.T.astype(BF16)
        outs = [jnp.dot(p, kcvc[:, gw + q * LANES:gw + (q + 1) * LANES].astype(BF16),
                        preferred_element_type=F32) for q in range(2)]
        sig = jax.nn.sigmoid(gb_ref[...])
        _sample_outputs(ocmp_ref, outs, sig, 0, tn)
        imp = p_t
        for k in range(1, R_B):
            imp = imp + pltpu.roll(p_t, k * G_B * tn, 1)
        imp_ref[...] = imp
        imp = imp_ref[pl.ds(0, nsp, stride=2), :] + imp_ref[pl.ds(1, nsp, stride=2), :]
        sidx = _iota((nsp, LANES), 0)
        forced = (sidx == (qpos_l >> _log2(SEL_BLOCK))) | (sidx == 0)
        sc_ref[0:nsp, :] = jnp.where(forced, FORCE_SCORE,
                                     jnp.where(sidx * SEL_BLOCK <= qpos_l, imp, -1.0))
        tail = sc_ref.shape[0] - nsp
        tidx = nsp + _iota((tail, LANES), 0)
        tforced = (tidx == (qpos_l >> _log2(SEL_BLOCK))) | (tidx == 0)
        tsc = jnp.where(tforced, FORCE_SCORE, jnp.where(tidx * SEL_BLOCK <= qpos_l, 0.0, -1.0))
        sc_ref[nsp:, :] = jnp.where(tidx == nsp, tsc, NEG_INF)
        selb_ref[...] = _topk_member(sc_ref[...])
        _flash_init(m_ref, l_ref, acc_ref)

    def sel_step(kts, vts, pair_r, s_bias_fn):
        s = jnp.where(pair_r == 0,
                      jnp.dot(qop_ref[0], kts[0], preferred_element_type=F32),
                      jnp.dot(qop_ref[1], kts[1], preferred_element_type=F32))
        s = s_bias_fn(s)
        m_prev, l_prev = m_ref[...], l_ref[...]
        m_cur = jnp.max(s, axis=-1, keepdims=True)
        m_new = jnp.maximum(m_prev, m_cur)
        alpha = jnp.exp2(m_prev - m_new)
        p = jnp.exp2(s - jnp.concatenate([m_new] * (s.shape[1] // LANES), axis=1))
        l_ref[...] = alpha * l_prev + _lane_blocks_sum(p)
        pb = p.astype(BF16)
        for q in range(2):
            acc_ref[q] = alpha * acc_ref[q] + _nt_dot(pb, vts[q])
        m_ref[...] = m_new

    @pl.when(ph == 1)
    def _():
        kts = [jnp.concatenate([pr[q * LANES:(q + 1) * LANES, :] for pr in page_refs], axis=1).astype(BF16)
               for q in range(2)]
        vts = [jnp.concatenate([pr[gw + q * LANES:gw + (q + 1) * LANES, :] for pr in page_refs],
                               axis=1).astype(BF16) for q in range(2)]
        nblk = ck // SEL_BLOCK
        pair_r, slope_r, qpos_r = row_consts()
        member = _pad_rows(selb_ref[pl.ds(pl.multiple_of(c * nblk, 8), nblk), :], LANES).T
        mask = jnp.dot(member.astype(BF16), e_ref[...], preferred_element_type=F32)
        dist = (qpos_r - (c * ck + _iota((1, ck), 1))).astype(F32)
        sel_step(kts, vts, pair_r, lambda s: jnp.where(mask > 0.5, s - slope_r * dist, NEG_INF))

    @pl.when((ph == 1) & (c == last))
    def _():
        xn = _pad_rows(new_ref[:, 2 * gw:4 * gw], LANES)
        kts = [xn[:, q * LANES:(q + 1) * LANES].T.astype(BF16) for q in range(2)]
        vts = [xn[:, gw + q * LANES:gw + (q + 1) * LANES].T.astype(BF16) for q in range(2)]
        pair_r, slope_r, qpos_r = row_consts()
        member = _col_of(selb_ref[nsp:nsp + 1, :])
        col = _iota((1, LANES), 1)
        dist = qpos_r - (past_len + col)
        ok = (member > 0.5) & (dist >= 0) & (col < tn)
        sel_step(kts, vts, pair_r, lambda s: jnp.where(ok, s - slope_r * dist.astype(F32), NEG_INF))
        l_tot = _row_total(l_ref[...])
        sig = jax.nn.sigmoid(gb_ref[...])
        _sample_outputs(osel_ref, [acc_ref[q] / l_tot for q in range(2)], sig, 1, tn)


def _compress_weights(w_cmp):
    nblk = 2 * PAGE_SIZE // CMP_BLOCK
    tok = jnp.arange(2 * PAGE_SIZE)
    w = jnp.transpose(w_cmp, (0, 2, 1))[:, :, tok % CMP_BLOCK]
    hit = (tok[None, :] // CMP_BLOCK) == jnp.arange(nblk)[:, None]
    return jnp.where(hit[None, None], w[:, :, None, :], 0.0).reshape(2 * G_B * nblk, 2 * PAGE_SIZE)


def _nsa_sample(zq, nsa_new, gb, w_cmp, cache, pt, bd, tn, n_pages, pps):
    past_len = n_pages * PAGE_SIZE
    gw = G_B * DH_B
    ncp = past_len // CMP_BLOCK
    nsp = past_len // SEL_BLOCK
    nsc = -(-(nsp + 1) // 8) * 8
    ck = pps * PAGE_SIZE
    assert pps % 2 == 0 and 2 * PAGE_SIZE // CMP_BLOCK == 8 and (ck // SEL_BLOCK) % 8 == 0
    kern = functools.partial(_nsa_sample_kernel, tn=tn, pps=pps, past_len=past_len)
    expand = (jnp.arange(ck)[None, :] // SEL_BLOCK == jnp.arange(LANES)[:, None]).astype(BF16)

    def page_spec(p):
        return pl.BlockSpec((2 * gw, PAGE_SIZE),
                            lambda bb, ph, c, pt, p=p: (pt[bb * n_pages + c * pps + p] * 2 + ph, 0))

    row_spec = lambda width, col: pl.BlockSpec((tn, width), lambda bb, ph, c, pt: (bb, col))
    grid_spec = pltpu.PrefetchScalarGridSpec(
        num_scalar_prefetch=1,
        grid=(bd, 2, n_pages // pps),
        in_specs=[row_spec(W_B, W_A // W_B), row_spec(4 * gw, 0), row_spec(LANES, 0),
                  pl.BlockSpec((2 * G_B * 8, 2 * PAGE_SIZE), lambda bb, ph, c, pt: (0, 0)),
                  pl.BlockSpec((LANES, ck), lambda bb, ph, c, pt: (0, 0))]
                 + [page_spec(p) for p in range(pps)],
        out_specs=[row_spec(W_B, 0), row_spec(W_B, 0)],
        scratch_shapes=[pltpu.VMEM((2, LANES, LANES), BF16),
                        pltpu.VMEM((ncp, 2 * gw), F32),
                        pltpu.VMEM((ncp, LANES), F32),
                        pltpu.VMEM((nsc, LANES), F32),
                        pltpu.VMEM((nsc, LANES), F32),
                        pltpu.VMEM((LANES, LANES), F32),
                        pltpu.VMEM((LANES, LANES), F32),
                        pltpu.VMEM((2, LANES, LANES), F32)])
    return pl.pallas_call(
        kern, grid_spec=grid_spec,
        out_shape=[jax.ShapeDtypeStruct((bd * tn, W_B), F32)] * 2,
        compiler_params=_params(("parallel", "arbitrary", "arbitrary")),
        name="nsa_sample",
    )(pt, zq, nsa_new, gb, _compress_weights(w_cmp), expand, *([cache] * pps))


def _win_sample_kernel(q_ref, new_ref, st_ref, gb_ref, o_ref, ns_ref, *, tn, past_len):
    gw = G_B * DH_B
    wb = st_ref.shape[2]
    qs = _sample_queries(q_ref[...], tn)
    st = st_ref[0]
    new_t = _pad_rows(new_ref[...], LANES).T
    pair_r, slope_r, qpos_r = _query_consts(_iota((LANES, 1), 0), tn, past_len)

    def scores(x):
        return jnp.where(pair_r == 0,
                         jnp.dot(qs[0], x[0:LANES, :].astype(BF16), preferred_element_type=F32),
                         jnp.dot(qs[1], x[LANES:2 * LANES, :].astype(BF16), preferred_element_type=F32))

    kpos_s = (past_len - wb) + _iota((1, wb), 1)
    d_s = qpos_r - kpos_s
    s_s = jnp.where((d_s >= 0) & (d_s < WINDOW) & (kpos_s >= 0), scores(st) - slope_r * d_s.astype(F32), NEG_INF)
    col_n = _iota((1, LANES), 1)
    d_n = qpos_r - (past_len + col_n)
    s_n = jnp.where((d_n >= 0) & (d_n < WINDOW) & (col_n < tn), scores(new_t) - slope_r * d_n.astype(F32), NEG_INF)
    s = jnp.concatenate([s_s, s_n], axis=1)
    e = jnp.exp2(s - jnp.max(s, axis=-1, keepdims=True))
    p = (e / jnp.sum(e, axis=-1, keepdims=True)).astype(BF16)
    outs = []
    for q in range(2):
        r0 = gw + q * LANES
        outs.append(_nt_dot(p[:, :wb], st[r0:r0 + LANES, :].astype(BF16))
                    + _nt_dot(p[:, wb:], new_t[r0:r0 + LANES, :].astype(BF16)))
    _sample_outputs(o_ref, outs, jax.nn.sigmoid(gb_ref[...]), 2, tn)
    shifted = pltpu.roll(st, wb - tn, 1)
    tail = jnp.concatenate([jnp.zeros((2 * gw, wb - LANES), F32), pltpu.roll(new_t, LANES - tn, 1)], axis=1)
    ns_ref[0] = jnp.where(_iota((2 * gw, wb), 1) >= wb - tn, tail, shifted)


def _win_sample(zq, win_new, state_t, gb, bd, tn, past_len):
    gw = G_B * DH_B
    wb = state_t.shape[2]
    assert wb == WINDOW and wb == min(WINDOW, past_len) and wb % LANES == 0
    kern = functools.partial(_win_sample_kernel, tn=tn, past_len=past_len)
    return pl.pallas_call(
        kern, grid=(bd,),
        in_specs=[pl.BlockSpec((tn, W_B), lambda bb: (bb, W_A // W_B)),
                  pl.BlockSpec((tn, 2 * gw), lambda bb: (bb, 0)),
                  pl.BlockSpec((1, 2 * gw, wb), lambda bb: (bb, 0, 0)),
                  pl.BlockSpec((tn, LANES), lambda bb: (bb, 0))],
        out_specs=[pl.BlockSpec((tn, W_B), lambda bb: (bb, 0)),
                   pl.BlockSpec((1, 2 * gw, wb), lambda bb: (bb, 0, 0))],
        out_shape=[jax.ShapeDtypeStruct((bd * tn, W_B), F32),
                   jax.ShapeDtypeStruct((bd, 2 * gw, wb), F32)],
        compiler_params=_params(("parallel",)),
        name="win_sample",
    )(zq, win_new, state_t, gb)


def kernel(x_prompt, x_sample, cache_diff_kv, cache_nsa_kv, state_win_kv, page_table, w_in, w_proj_a,
           w_proj_b, w_out, lambda_qk, diff_gain, w_cmp, norm_attn, norm_mlp, w_up, w_down, norm_final):
    depth = w_in.shape[0]
    assert depth == 1
    b, t, d = x_prompt.shape
    bd, tn, _ = x_sample.shape
    l = 0
    lam_init = 0.8 - 0.6 * math.exp(-0.3 * l)
    w = _split_w_in(w_in[l])
    wpa, wpb, wo = w_proj_a[l].astype(BF16), w_proj_b[l].astype(BF16), w_out[l].astype(BF16)
    wup, wdn = w_up[l].astype(BF16), w_down[l].astype(BF16)
    wexp = jnp.repeat(w_cmp[l], DH_B, axis=-1)

    xp = x_prompt.reshape(b * t, d)
    up = _rmsnorm(xp, norm_attn[l], BF16, 512)
    (oa, ocmp, osel, owin, gm), (kva32, nsa32, win32) = _prompt_mixer(
        up, w, lambda_qk[l], diff_gain[l], wexp, b, t, lam_init)
    hp, u2 = _post_attention(oa, ocmp, osel, owin, gm, xp, wpa, wpb, wo, norm_mlp[l], 256)
    y_prompt = _mlp_final(u2, hp, wup, wdn, norm_final, 512, 512).reshape(b, t, d)
    keep = min(WINDOW, t)
    diff_kv_prompt = kva32.reshape(1, b, t, 2, HKV_A, 2 * DH_A)
    nsa_kv_prompt = nsa32.reshape(1, b, t, 4, G_B, DH_B)
    win_kv_prompt = win32.reshape(b, t, 2, G_B, DH_B)[None, :, t - keep:]

    n_pages = page_table.shape[1]
    past_len = n_pages * PAGE_SIZE
    n_pool = cache_diff_kv.shape[1]
    pps = 16
    assert R_B * G_B * tn == LANES and tn < CMP_BLOCK and n_pages % pps == 0
    xs = x_sample.reshape(bd * tn, d)
    us = _rmsnorm(xs, norm_attn[l], BF16, bd * tn)
    zq_s, kva_s, _, nsa_s, win_s, gb_s, gm_s = _in_projection(us, w, bd * tn, F32)
    pt = page_table.reshape(-1)
    cache_d = cache_diff_kv[l].reshape(n_pool * PAGE_SIZE * 2 * HKV_A, 2 * DH_A)
    cache_n = jnp.transpose(cache_nsa_kv[l], (0, 2, 3, 4, 1)).reshape(n_pool * 4 * G_B * DH_B, PAGE_SIZE)
    wb = state_win_kv.shape[2]
    state_t = jnp.transpose(state_win_kv[l], (0, 2, 3, 4, 1)).reshape(bd, 2 * G_B * DH_B, wb)
    oa_s = _diff_sample(zq_s, kva_s, cache_d, pt, lambda_qk[l], diff_gain[l], bd, tn, n_pages, pps, lam_init)
    ocmp_s, osel_s = _nsa_sample(zq_s, nsa_s, gb_s, w_cmp[l], cache_n, pt, bd, tn, n_pages, pps)
    owin_s, new_state_t = _win_sample(zq_s, win_s, state_t, gb_s, bd, tn, past_len)
    hs, u2s = _post_attention(oa_s, ocmp_s, osel_s, owin_s, gm_s, xs, wpa, wpb, wo, norm_mlp[l], 256)
    y_sample = _mlp_final(u2s, hs, wup, wdn, norm_final, 512, 512).reshape(bd, tn, d)
    diff_kv_sample = kva_s.reshape(1, bd, tn, 2, HKV_A, 2 * DH_A)
    nsa_kv_sample = nsa_s.reshape(1, bd, tn, 4, G_B, DH_B)
    win_kv_sample = jnp.transpose(new_state_t.reshape(bd, 2, G_B, DH_B, wb), (0, 4, 1, 2, 3))[None]
    return (y_prompt, y_sample, diff_kv_prompt, nsa_kv_prompt, win_kv_prompt,
            diff_kv_sample, nsa_kv_sample, win_kv_sample)
```

```python
import functools
import math

import numpy as np
import jax
import jax.numpy as jnp
from jax import lax
from jax.experimental import pallas as pl
from jax.experimental.pallas import tpu as pltpu

F32, BF16, I32 = jnp.float32, jnp.bfloat16, jnp.int32

PAGE_SIZE = 128
H_A, HKV_A, R_A, DH_A = 8, 4, 2, 64
W_A = H_A * 2 * DH_A
H_B, G_B, R_B, DH_B = 16, 4, 4, 64
W_B = H_B * DH_B
CMP_BLOCK, SEL_BLOCK, TOP_N, WINDOW = 32, 64, 16, 512
EPS = 1e-6
NEG_INF = -1e30
FORCE_SCORE = 1e4
LOG2E = math.log2(math.e)
LANES = 128
HALF = LANES // 2
VMEM_LIMIT = 56 * 1024 * 1024


def _iota(shape, dim):
    return lax.broadcasted_iota(I32, shape, dim)


def _log2(n):
    assert n & (n - 1) == 0, n
    return n.bit_length() - 1


def _nt_dot(a, b):
    return lax.dot_general(a, b, (((1,), (1,)), ((), ())), preferred_element_type=F32)


def _params(sem):
    return pltpu.CompilerParams(dimension_semantics=sem, vmem_limit_bytes=VMEM_LIMIT)


def _tri_tiles(n):
    ii = np.array([i for i in range(n) for _ in range(i + 1)], np.int32)
    jj = np.array([j for i in range(n) for j in range(i + 1)], np.int32)
    return jnp.asarray(ii), jnp.asarray(jj)


def _slopes2(n):
    return jnp.exp2(-8.0 * jnp.arange(1, n + 1, dtype=F32) / n) * LOG2E


def _rmsnorm_kernel(x_ref, g_ref, o_ref):
    x = x_ref[...]
    y = x * lax.rsqrt(jnp.mean(x * x, axis=-1, keepdims=True) + EPS)
    o_ref[...] = (y * g_ref[...]).astype(o_ref.dtype)


def _rmsnorm(x, g, out_dtype, tm):
    m, d = x.shape
    return pl.pallas_call(
        _rmsnorm_kernel,
        grid=(m // tm,),
        in_specs=[pl.BlockSpec((tm, d), lambda i: (i, 0)),
                  pl.BlockSpec((1, d), lambda i: (0, 0))],
        out_specs=pl.BlockSpec((tm, d), lambda i: (i, 0)),
        out_shape=jax.ShapeDtypeStruct((m, d), out_dtype),
        compiler_params=_params(("parallel",)),
        name="rmsnorm",
    )(x, g.reshape(1, d))


def _mm_kernel(a_ref, w_ref, *o_refs):
    acc = jnp.dot(a_ref[...], w_ref[...], preferred_element_type=F32)
    for o in o_refs:
        o[...] = acc.astype(o.dtype)


def _matmul(a, w, out_dtypes, tm, tn, name):
    m, k = a.shape
    n = w.shape[1]
    tm, tn = min(tm, m), min(tn, n)
    outs = pl.pallas_call(
        _mm_kernel,
        grid=(n // tn, m // tm),
        in_specs=[pl.BlockSpec((tm, k), lambda j, i: (i, 0)),
                  pl.BlockSpec((k, tn), lambda j, i: (0, j))],
        out_specs=[pl.BlockSpec((tm, tn), lambda j, i: (i, j)) for _ in out_dtypes],
        out_shape=[jax.ShapeDtypeStruct((m, n), dt) for dt in out_dtypes],
        compiler_params=_params(("parallel", "parallel")),
        name=name,
    )(a, w)
    return outs


def _mm_tokens_last_kernel(w_ref, a_ref, o_ref):
    o_ref[...] = _nt_dot(w_ref[...], a_ref[...])


def _matmul_tokens_last(w_t, a, b, tn, tm, name):
    n, k = w_t.shape
    t = a.shape[0] // b
    tn, tm = min(tn, n), min(tm, t)
    return pl.pallas_call(
        _mm_tokens_last_kernel,
        grid=(b, n // tn, t // tm),
        in_specs=[pl.BlockSpec((tn, k), lambda bb, j, i: (j, 0)),
                  pl.BlockSpec((tm, k), lambda bb, j, i: (bb * (t // tm) + i, 0))],
        out_specs=pl.BlockSpec((tn, tm), lambda bb, j, i: (bb * (n // tn) + j, i)),
        out_shape=jax.ShapeDtypeStruct((b * n, t), F32),
        compiler_params=_params(("parallel", "parallel", "parallel")),
        name=name,
    )(w_t, a)


def _mm_token_rows_kernel(a_ref, w_ref, o32_ref, o16_ref):
    acc = jnp.dot(a_ref[...], w_ref[...], preferred_element_type=F32)
    o16_ref[...] = acc.astype(o16_ref.dtype)
    nrow = acc.shape[1] // LANES
    for j in range(nrow):
        o32_ref[pl.ds(j, acc.shape[0], stride=nrow), :] = acc[:, j * LANES:(j + 1) * LANES]


def _matmul_token_rows(a, w, tm, name):
    m, k = a.shape
    n = w.shape[1]
    tm = min(tm, m)
    return pl.pallas_call(
        _mm_token_rows_kernel,
        grid=(m // tm,),
        in_specs=[pl.BlockSpec((tm, k), lambda i: (i, 0)),
                  pl.BlockSpec((k, n), lambda i: (0, 0), pipeline_mode=pl.Buffered(1))],
        out_specs=[pl.BlockSpec((tm * (n // LANES), LANES), lambda i: (i, 0)),
                   pl.BlockSpec((tm, n), lambda i: (i, 0))],
        out_shape=[jax.ShapeDtypeStruct((m * (n // LANES), LANES), F32),
                   jax.ShapeDtypeStruct((m, n), BF16)],
        compiler_params=_params(("parallel",)),
        name=name,
    )(a, w)


def _post_kernel(oa_ref, o1_ref, o2_ref, o3_ref, gm0_ref, gm1_ref, x_ref,
                 wpa_ref, wpb_ref, wo_ref, g_ref, h_ref, u_ref):
    oa = oa_ref[...].astype(BF16)
    ob = (o1_ref[...] + o2_ref[...] + o3_ref[...]).astype(BF16)
    pa = jnp.dot(oa, wpa_ref[...], preferred_element_type=F32)
    pb = jnp.dot(ob, wpb_ref[...], preferred_element_type=F32)
    mix = jax.nn.sigmoid(gm0_ref[...]) * pa + jax.nn.sigmoid(gm1_ref[...]) * pb
    y = jnp.dot(mix.astype(BF16), wo_ref[...], preferred_element_type=F32)
    h = x_ref[...] + y
    h_ref[...] = h
    u = h * lax.rsqrt(jnp.mean(h * h, axis=-1, keepdims=True) + EPS)
    u_ref[...] = (u * g_ref[...]).astype(u_ref.dtype)


def _post_attention(oa, o1, o2, o3, gm, x, wpa, wpb, wo, g_mlp, tm):
    m, d = x.shape
    tm = min(tm, m)
    row = lambda i: (i, 0)
    const = lambda i: (0, 0)
    once = pl.Buffered(1)
    return pl.pallas_call(
        _post_kernel,
        grid=(m // tm,),
        in_specs=[pl.BlockSpec((tm, W_A), row),
                  pl.BlockSpec((tm, W_B), row),
                  pl.BlockSpec((tm, W_B), row),
                  pl.BlockSpec((tm, W_B), row),
                  pl.BlockSpec((tm, d), lambda i: (i, 0)),
                  pl.BlockSpec((tm, d), lambda i: (i, 1)),
                  pl.BlockSpec((tm, d), row),
                  pl.BlockSpec((W_A, d), const, pipeline_mode=once),
                  pl.BlockSpec((W_B, d), const, pipeline_mode=once),
                  pl.BlockSpec((d, d), const, pipeline_mode=once),
                  pl.BlockSpec((1, d), const)],
        out_specs=[pl.BlockSpec((tm, d), row), pl.BlockSpec((tm, d), row)],
        out_shape=[jax.ShapeDtypeStruct((m, d), F32), jax.ShapeDtypeStruct((m, d), BF16)],
        compiler_params=_params(("parallel",)),
        name="post_attention",
    )(oa, o1, o2, o3, gm, gm, x, wpa, wpb, wo, g_mlp.reshape(1, d))


def _mlp_kernel(u_ref, h_ref, wup_ref, wdn_ref, g_ref, o_ref, acc_ref):
    f = pl.program_id(1)

    @pl.when(f == 0)
    def _():
        acc_ref[...] = jnp.zeros_like(acc_ref)

    a = jnp.dot(u_ref[...], wup_ref[...], preferred_element_type=F32)
    a = jnp.square(jnp.maximum(a, 0.0)).astype(BF16)
    acc_ref[...] += jnp.dot(a, wdn_ref[...], preferred_element_type=F32)

    @pl.when(f == pl.num_programs(1) - 1)
    def _():
        y = h_ref[...] + acc_ref[...]
        y = y * lax.rsqrt(jnp.mean(y * y, axis=-1, keepdims=True) + EPS)
        o_ref[...] = y * g_ref[...]


def _mlp_final(u, h, wup, wdn, g_final, tm, tf):
    m, d = h.shape
    dff = wup.shape[1]
    tm, tf = min(tm, m), min(tf, dff)
    return pl.pallas_call(
        _mlp_kernel,
        grid=(m // tm, dff // tf),
        in_specs=[pl.BlockSpec((tm, d), lambda i, f: (i, 0)),
                  pl.BlockSpec((tm, d), lambda i, f: (i, 0)),
                  pl.BlockSpec((d, tf), lambda i, f: (0, f)),
                  pl.BlockSpec((tf, d), lambda i, f: (f, 0)),
                  pl.BlockSpec((1, d), lambda i, f: (0, 0))],
        out_specs=pl.BlockSpec((tm, d), lambda i, f: (i, 0)),
        out_shape=jax.ShapeDtypeStruct((m, d), F32),
        scratch_shapes=[pltpu.VMEM((tm, d), F32)],
        compiler_params=_params(("parallel", "arbitrary")),
        name="mlp_final",
    )(u, h, wup, wdn, g_final.reshape(1, d))


def _flash_init(m_ref, l_ref, acc_ref):
    m_ref[...] = jnp.full(m_ref.shape, NEG_INF, F32)
    if l_ref is not None:
        l_ref[...] = jnp.zeros(l_ref.shape, F32)
    acc_ref[...] = jnp.zeros(acc_ref.shape, F32)


def _lane_blocks_sum(p):
    out = p[:, 0:LANES]
    for n in range(1, p.shape[1] // LANES):
        out = out + p[:, n * LANES:(n + 1) * LANES]
    return out


def _flash_step(s, shift, m_prev, l_prev, acc_prev, pv_fn):
    m_cur = jnp.max(s, axis=-1, keepdims=True)
    if shift is not None:
        m_cur = m_cur - shift
    m_new = jnp.maximum(m_prev, m_cur)
    alpha = jnp.exp2(m_prev - m_new)
    m_adj = m_new if shift is None else m_new + shift
    p = jnp.exp2(s - jnp.concatenate([m_adj] * (s.shape[1] // LANES), axis=1))
    l_new = None if l_prev is None else alpha * l_prev + _lane_blocks_sum(p)
    alpha_acc = jnp.concatenate([alpha] * (acc_prev.shape[1] // LANES), axis=1)
    acc_new = alpha_acc * acc_prev + pv_fn(p.astype(BF16))
    return m_new, l_new, acc_new


def _flash_rows(rs, s, shift, v, m_ref, l_ref, acc_ref):
    m, l, acc = _flash_step(s, shift, m_ref[rs, :], None if l_ref is None else l_ref[rs, :], acc_ref[rs, :],
                            lambda p: jnp.dot(p, v, preferred_element_type=F32))
    m_ref[rs, :] = m
    if l_ref is not None:
        l_ref[rs, :] = l
    acc_ref[rs, :] = acc


def _row_total(l):
    return jnp.sum(l, axis=-1, keepdims=True)


def _alibi_tiles(slopes, tq):
    ng, nr = slopes.shape
    d = (jnp.arange(tq)[:, None] - jnp.arange(tq)[None, :]).astype(F32)
    off = -slopes[:, :, None, None] * d
    tiles = jnp.stack([off, jnp.where(d >= 0, off, NEG_INF)], axis=1).reshape(ng, 2, nr * tq, tq)
    step = jnp.broadcast_to((slopes * tq)[:, :, None, None], (ng, nr, tq, LANES)).reshape(ng, nr * tq, LANES)
    return tiles, step


def _window_tiles(slopes, tq, nj):
    ng, nr = slopes.shape
    d = jnp.arange(tq)[:, None] - jnp.arange(tq)[None, :]
    dist = (nj - 1 - jnp.arange(nj))[:, None, None] * tq + d[None]
    dist = dist[None, :, None]
    bias = jnp.where((dist >= 0) & (dist < WINDOW),
                     -slopes[:, None, :, None, None] * dist.astype(F32), NEG_INF)
    return bias.reshape(ng, nj, nr * tq, tq)


def _diff_lambda(lam_ref, lam_init):
    lf = lam_ref[...]
    e1 = jnp.exp(jnp.sum(lf[0:1] * lf[1:2], axis=-1, keepdims=True))
    e2 = jnp.exp(jnp.sum(lf[2:3] * lf[3:4], axis=-1, keepdims=True))
    return e1 - e2 + lam_init


def _diff_finish(o1, o2, lam, gain, lam_init):
    o = o1 - lam * o2
    y = o * lax.rsqrt(jnp.mean(o * o, axis=-1, keepdims=True) + EPS)
    return (y * gain) * (1.0 - lam_init)


def _diff_prompt_kernel(ii_ref, jj_ref, q_ref, k_ref, v_ref, tile_ref, step_ref, lam_ref, gain_ref, o_ref,
                        qs_ref, m_ref, l_ref, acc_ref, *, tq, lam_init):
    t = pl.program_id(2)
    i = ii_ref[t]
    j = jj_ref[t]
    nblk = 2 * R_A

    @pl.when(j == 0)
    def _():
        upper = _iota((tq, LANES), 1) >= HALF
        for r in range(R_A):
            qv = q_ref[:, r * LANES:(r + 1) * LANES]
            for c in range(2):
                qs_ref[(2 * r + c) * tq:(2 * r + c + 1) * tq, :] = jnp.where(upper == bool(c), qv, 0).astype(BF16)
        _flash_init(m_ref, l_ref, acc_ref)

    diag = (i == j).astype(I32)
    steps = (i - j).astype(F32)
    k = k_ref[...]
    v = v_ref[...]
    for n in range(nblk):
        rs = slice(n * tq, (n + 1) * tq)
        s = _nt_dot(qs_ref[rs, :], k) + tile_ref[0, diag, rs, :]
        _flash_rows(rs, s, step_ref[0, rs, :] * steps, v, m_ref, l_ref, acc_ref)

    @pl.when(j == i)
    def _():
        lam = _diff_lambda(lam_ref, lam_init)
        for r in range(R_A):
            b0 = 2 * r * tq
            o1 = acc_ref[b0:b0 + tq, :] / _row_total(l_ref[b0:b0 + tq, :])
            o2 = acc_ref[b0 + tq:b0 + 2 * tq, :] / _row_total(l_ref[b0 + tq:b0 + 2 * tq, :])
            y = _diff_finish(o1, o2, lam, gain_ref[...], lam_init)
            o_ref[:, r * LANES:(r + 1) * LANES] = y.astype(o_ref.dtype)


def _diff_prompt(qa, kva, lam_qk, gain, b, t, tq, lam_init):
    nq = t // tq
    ii, jj = _tri_tiles(nq)
    rows = 2 * R_A * tq
    slopes = jnp.repeat(_slopes2(H_A).reshape(HKV_A, R_A), 2, axis=1)
    tiles, step = _alibi_tiles(slopes, tq)
    kern = functools.partial(_diff_prompt_kernel, tq=tq, lam_init=lam_init)
    grid_spec = pltpu.PrefetchScalarGridSpec(
        num_scalar_prefetch=2,
        grid=(b, HKV_A, ii.shape[0]),
        in_specs=[pl.BlockSpec((tq, R_A * LANES), lambda bb, h, n, ii, jj: (bb * nq + ii[n], h)),
                  pl.BlockSpec((tq, LANES), lambda bb, h, n, ii, jj: (bb * nq + jj[n], h)),
                  pl.BlockSpec((tq, LANES), lambda bb, h, n, ii, jj: (bb * nq + jj[n], HKV_A + h)),
                  pl.BlockSpec((1, 2, rows, tq), lambda bb, h, n, ii, jj: (h, 0, 0, 0)),
                  pl.BlockSpec((1, rows, LANES), lambda bb, h, n, ii, jj: (h, 0, 0)),
                  pl.BlockSpec((4, DH_A), lambda bb, h, n, ii, jj: (0, 0)),
                  pl.BlockSpec((1, 2 * DH_A), lambda bb, h, n, ii, jj: (0, 0))],
        out_specs=pl.BlockSpec((tq, R_A * LANES), lambda bb, h, n, ii, jj: (bb * nq + ii[n], h)),
        scratch_shapes=[pltpu.VMEM((rows, LANES), BF16),
                        pltpu.VMEM((rows, LANES), F32),
                        pltpu.VMEM((rows, LANES), F32),
                        pltpu.VMEM((rows, LANES), F32)])
    return pl.pallas_call(
        kern, grid_spec=grid_spec,
        out_shape=jax.ShapeDtypeStruct((b * t, W_A), BF16),
        compiler_params=_params(("parallel", "parallel", "arbitrary")),
        name="diff_prompt",
    )(ii, jj, qa, kva, kva, tiles, step, lam_qk, gain.reshape(1, 2 * DH_A))


def _dup_half(v, g):
    rolled = pltpu.roll(v, HALF, 1)
    low = _iota(v.shape, 1) < HALF
    return jnp.where(low, v, rolled) if g % 2 == 0 else jnp.where(low, rolled, v)


def _value_ones(v, g):
    low = _iota(v.shape, 1) < HALF
    return jnp.where(low, v if g % 2 == 0 else pltpu.roll(v, HALF, 1), 1.0)


def _prep_kernel(nsa_ref, win_ref, wexp_ref, kcd_ref, vcd_ref, ksa_ref, vsd_ref, kwd_ref, vwd_ref,
                 tmp_ref, *, tp, nc):
    step = pl.program_id(1)
    gw = G_B * DH_B
    x = nsa_ref[...]
    w = win_ref[...]
    lane = _iota((tp, LANES), 1)
    low = lane < HALF
    blk = (step * tp + _iota((tp, LANES), 0)) >> _log2(SEL_BLOCK)
    onehot = jnp.where(lane - HALF == blk, 1.0, 0.0)
    for g in range(G_B):
        c0 = (g // 2) * LANES
        vk = x[:, 2 * gw + c0:2 * gw + c0 + LANES]
        klow = vk if g % 2 == 0 else pltpu.roll(vk, HALF, 1)
        ksa_ref[0, g] = jnp.where(low, klow, onehot).astype(BF16)
        vsd_ref[0, g] = _value_ones(x[:, 3 * gw + c0:3 * gw + c0 + LANES], g).astype(BF16)
        kwd_ref[0, g] = _dup_half(w[:, c0:c0 + LANES], g).astype(BF16)
        vwd_ref[0, g] = _value_ones(w[:, gw + c0:gw + c0 + LANES], g).astype(BF16)
    nb = tp // CMP_BLOCK
    hb = nb // 2
    for slot, dst in ((0, kcd_ref), (1, vcd_ref)):
        xc = x[:, slot * gw:(slot + 1) * gw]
        comp = jnp.sum(xc.reshape(nb, CMP_BLOCK, gw) * wexp_ref[slot][None], axis=1)
        for c in range(gw // LANES):
            tmp_ref[c * nb:(c + 1) * nb, :] = comp[:, c * LANES:(c + 1) * LANES]
        for par in range(2):
            start = pl.multiple_of(par * (nc // 2) + step * hb, 8)
            for g in range(G_B):
                rr = tmp_ref[pl.ds((g // 2) * nb + par, hb, stride=2), :]
                dst[0, g, pl.ds(start, hb), :] = _dup_half(rr, g)


def _nsa_prep(nsa, win, wexp, b, t, tp):
    nc = t // CMP_BLOCK
    gw = G_B * DH_B
    kern = functools.partial(_prep_kernel, tp=tp, nc=nc)
    small = jax.ShapeDtypeStruct((b, G_B, nc, LANES), F32)
    big = jax.ShapeDtypeStruct((b, G_B, t, LANES), BF16)
    small_spec = pl.BlockSpec((1, G_B, nc, LANES), lambda bb, s: (bb, 0, 0, 0))
    big_spec = pl.BlockSpec((1, G_B, tp, LANES), lambda bb, s: (bb, 0, s, 0))
    nt = t // tp
    return pl.pallas_call(
        kern, grid=(b, nt),
        in_specs=[pl.BlockSpec((tp, 4 * gw), lambda bb, s: (bb * nt + s, 0)),
                  pl.BlockSpec((tp, 2 * gw), lambda bb, s: (bb * nt + s, 0)),
                  pl.BlockSpec((2, CMP_BLOCK, gw), lambda bb, s: (0, 0, 0))],
        out_specs=[small_spec, small_spec, big_spec, big_spec, big_spec, big_spec],
        out_shape=[small, small, big, big, big, big],
        scratch_shapes=[pltpu.VMEM((gw // LANES * (tp // CMP_BLOCK), LANES), F32)],
        compiler_params=_params(("parallel", "arbitrary")),
        name="nsa_prep",
    )(nsa, win, wexp)


def _topk_bias(sc, sidx):
    rank = jnp.zeros(sc.shape, I32)
    for c in range(sc.shape[0]):
        other = sc[c:c + 1, :]
        beats = (other > sc) | ((other == sc) & (sidx > c))
        rank = rank + beats.astype(I32)
    return jnp.where(rank < TOP_N, 0.0, NEG_INF)


def _cmp_prompt_kernel(q_ref, kcd_ref, vcd_ref, gb_ref, ocmp_ref, qaug_ref, *, tq, nc):
    i = pl.program_id(1)
    ns = nc // 2
    lowq = _iota((tq, LANES), 1) < HALF
    rowc = _iota((nc, tq), 0)
    qpos = i * tq + _iota((nc, tq), 1)
    blk = jnp.where(rowc >= ns, 2 * (rowc - ns) + 1, 2 * rowc)
    c_mid = (blk * CMP_BLOCK).astype(F32) + (CMP_BLOCK - 1) * 0.5
    valid = blk * CMP_BLOCK + (CMP_BLOCK - 1) <= qpos
    adist = jnp.abs(qpos.astype(F32) - c_mid)
    sig = jax.nn.sigmoid(gb_ref[...])
    sidx = _iota((ns, tq), 0)
    qp = i * tq + _iota((ns, tq), 1)
    forced = (sidx == (qp >> _log2(SEL_BLOCK))) | (sidx == 0)
    valid_s = sidx * SEL_BLOCK <= qp
    for g in range(G_B):
        kc = kcd_ref[0, g].astype(BF16)
        vc = vcd_ref[0, g].astype(BF16)
        imp = jnp.zeros((nc, tq), F32)
        outs, qlows = [], []
        for r in range(R_B):
            h = g * R_B + r
            qv = q_ref[:, (h // 2) * LANES:(h // 2 + 1) * LANES]
            qm = jnp.where(lowq == (h % 2 == 0), qv, 0).astype(BF16)
            s = _nt_dot(kc, qm)
            s = s - (2.0 ** (-8.0 * (h + 1) / H_B) * LOG2E) * adist
            sm = jnp.where(valid, s, NEG_INF)
            e = jnp.exp2(sm - jnp.max(sm, axis=0, keepdims=True))
            p = jnp.where(valid, e / jnp.sum(e, axis=0, keepdims=True), 0.0)
            imp = imp + p
            o2 = jnp.dot(p.T.astype(BF16), vc, preferred_element_type=F32)
            outs.append(o2 * sig[:, h:h + 1])
            qlows.append(qv if h % 2 == 0 else pltpu.roll(qv, HALF, 1))
        for k in range(R_B // 2):
            c0 = (g * (R_B // 2) + k) * LANES
            ocmp_ref[:, c0:c0 + LANES] = jnp.where(lowq, outs[2 * k], outs[2 * k + 1])
        sc = jnp.where(forced, FORCE_SCORE, jnp.where(valid_s, imp[:ns] + imp[ns:], -1.0))
        bias_t = _topk_bias(sc, sidx)
        bias = jnp.concatenate([bias_t] * (LANES // ns), axis=0).T.astype(BF16)
        for r in range(R_B):
            qaug_ref[0, g, r] = jnp.where(lowq, qlows[r], bias).astype(BF16)


def _cmp_prompt(zq, kcd, vcd, gb, b, t, tq):
    nc = t // CMP_BLOCK
    nq = t // tq
    kern = functools.partial(_cmp_prompt_kernel, tq=tq, nc=nc)
    small_spec = pl.BlockSpec((1, G_B, nc, LANES), lambda bb, i: (bb, 0, 0, 0))
    return pl.pallas_call(
        kern, grid=(b, nq),
        in_specs=[pl.BlockSpec((tq, W_B), lambda bb, i: (bb * nq + i, W_A // W_B)),
                  small_spec, small_spec,
                  pl.BlockSpec((tq, LANES), lambda bb, i: (bb * nq + i, 0))],
        out_specs=[pl.BlockSpec((tq, W_B), lambda bb, i: (bb * nq + i, 0)),
                   pl.BlockSpec((1, G_B, R_B, tq, LANES), lambda bb, i: (bb, 0, 0, i, 0))],
        out_shape=[jax.ShapeDtypeStruct((b * t, W_B), F32),
                   jax.ShapeDtypeStruct((b, G_B, R_B, t, LANES), BF16)],
        compiler_params=_params(("parallel", "parallel")),
        name="cmp_prompt",
    )(zq, kcd, vcd, gb)


def _gate_column(sig, col):
    return jnp.sum(jnp.where(_iota(sig.shape, 1) == col, sig, 0.0), axis=-1, keepdims=True)


def _group_finish(o_ref, gb_ref, acc_ref, g, branch, tq):
    sig = jax.nn.sigmoid(gb_ref[...])
    low = _iota((tq, LANES), 1) < HALF
    outs = []
    for r in range(R_B):
        a = acc_ref[r * tq:(r + 1) * tq, :]
        o = a / pltpu.roll(a, HALF, 1)
        outs.append(o * _gate_column(sig, branch * H_B + g * R_B + r))
    for k in range(R_B // 2):
        o_ref[:, k * LANES:(k + 1) * LANES] = jnp.where(low, outs[2 * k], pltpu.roll(outs[2 * k + 1], HALF, 1))


def _sel_prompt_kernel(ii_ref, jj_ref, qa_ref, k_ref, v_ref, tile_ref, step_ref, gb_ref, o_ref,
                       m_ref, acc_ref, *, tq):
    g = pl.program_id(1)
    t = pl.program_id(2)
    i = ii_ref[t]
    j = jj_ref[t]

    @pl.when(j == 0)
    def _():
        _flash_init(m_ref, None, acc_ref)

    diag = (i == j).astype(I32)
    steps = (i - j).astype(F32)
    k = k_ref[0, 0]
    v = v_ref[0, 0]
    for r in range(R_B):
        rs = slice(r * tq, (r + 1) * tq)
        s = _nt_dot(qa_ref[0, 0, r], k) + tile_ref[0, diag, rs, :]
        _flash_rows(rs, s, step_ref[0, rs, :] * steps, v, m_ref, None, acc_ref)

    @pl.when(j == i)
    def _():
        _group_finish(o_ref, gb_ref, acc_ref, g, 1, tq)


def _sel_prompt(qaug, ksa, vsd, gb, b, t, tq):
    nq = t // tq
    ii, jj = _tri_tiles(nq)
    rows = R_B * tq
    tiles, step = _alibi_tiles(_slopes2(H_B).reshape(G_B, R_B), tq)
    kern = functools.partial(_sel_prompt_kernel, tq=tq)
    kv_spec = pl.BlockSpec((1, 1, tq, LANES), lambda bb, g, n, ii, jj: (bb, g, jj[n], 0))
    grid_spec = pltpu.PrefetchScalarGridSpec(
        num_scalar_prefetch=2,
        grid=(b, G_B, ii.shape[0]),
        in_specs=[pl.BlockSpec((1, 1, R_B, tq, LANES), lambda bb, g, n, ii, jj: (bb, g, 0, ii[n], 0)),
                  kv_spec, kv_spec,
                  pl.BlockSpec((1, 2, rows, tq), lambda bb, g, n, ii, jj: (g, 0, 0, 0)),
                  pl.BlockSpec((1, rows, LANES), lambda bb, g, n, ii, jj: (g, 0, 0)),
                  pl.BlockSpec((tq, LANES), lambda bb, g, n, ii, jj: (bb * nq + ii[n], 0))],
        out_specs=pl.BlockSpec((tq, R_B * DH_B), lambda bb, g, n, ii, jj: (bb * nq + ii[n], g)),
        scratch_shapes=[pltpu.VMEM((rows, LANES), F32),
                        pltpu.VMEM((rows, LANES), F32)])
    return pl.pallas_call(
        kern, grid_spec=grid_spec,
        out_shape=jax.ShapeDtypeStruct((b * t, W_B), F32),
        compiler_params=_params(("parallel", "parallel", "arbitrary")),
        name="sel_prompt",
    )(ii, jj, qaug, ksa, vsd, tiles, step, gb)


def _win_prompt_kernel(q_ref, k_ref, v_ref, tile_ref, gb_ref, o_ref, qs_ref, m_ref, acc_ref, *, tq, nj):
    g = pl.program_id(1)
    i = pl.program_id(2)
    j = pl.program_id(3)

    @pl.when(j == 0)
    def _():
        low = _iota((tq, LANES), 1) < HALF
        for r in range(R_B):
            qv = q_ref[:, (r // 2) * LANES:(r // 2 + 1) * LANES]
            qs_ref[r * tq:(r + 1) * tq, :] = jnp.where(low == (r % 2 == 0), qv, 0).astype(BF16)
        _flash_init(m_ref, None, acc_ref)

    @pl.when(i - (nj - 1) + j >= 0)
    def _():
        k = k_ref[0, 0]
        v = v_ref[0, 0]
        for r in range(R_B):
            rs = slice(r * tq, (r + 1) * tq)
            s = _nt_dot(qs_ref[rs, :], k) + tile_ref[0, j, rs, :]
            _flash_rows(rs, s, None, v, m_ref, None, acc_ref)

    @pl.when(j == nj - 1)
    def _():
        _group_finish(o_ref, gb_ref, acc_ref, g, 2, tq)


def _win_prompt(zq, kwd, vwd, gb, b, t, tq):
    nq = t // tq
    nj = -(-WINDOW // tq) + 1
    qoff = W_A // (R_B * DH_B)
    rows = R_B * tq
    tiles = _window_tiles(_slopes2(H_B).reshape(G_B, R_B), tq, nj)
    kern = functools.partial(_win_prompt_kernel, tq=tq, nj=nj)
    kv_spec = pl.BlockSpec((1, 1, tq, LANES),
                           lambda bb, g, i, j: (bb, g, jnp.maximum(i - (nj - 1) + j, 0), 0))
    return pl.pallas_call(
        kern, grid=(b, G_B, nq, nj),
        in_specs=[pl.BlockSpec((tq, R_B * DH_B), lambda bb, g, i, j: (bb * nq + i, qoff + g)),
                  kv_spec, kv_spec,
                  pl.BlockSpec((1, nj, rows, tq), lambda bb, g, i, j: (g, 0, 0, 0)),
                  pl.BlockSpec((tq, LANES), lambda bb, g, i, j: (bb * nq + i, 0))],
        out_specs=pl.BlockSpec((tq, R_B * DH_B), lambda bb, g, i, j: (bb * nq + i, g)),
        out_shape=jax.ShapeDtypeStruct((b * t, W_B), F32),
        scratch_shapes=[pltpu.VMEM((rows, LANES), BF16),
                        pltpu.VMEM((rows, LANES), F32),
                        pltpu.VMEM((rows, LANES), F32)],
        compiler_params=_params(("parallel", "parallel", "parallel", "arbitrary")),
        name="win_prompt",
    )(zq, kwd, vwd, tiles, gb)


def _split_w_in(w_in):
    sizes = (W_A, HKV_A * 2 * DH_A, HKV_A * 2 * DH_A, W_B, 6 * G_B * DH_B, 3 * H_B, 2 * w_in.shape[0])
    offs = np.concatenate([[0], np.cumsum(sizes)])
    seg = lambda a, b_: w_in[:, a:b_]
    gw = G_B * DH_B
    return dict(
        q=jnp.concatenate([seg(offs[0], offs[1]) * (DH_A ** -0.5 * LOG2E),
                           seg(offs[3], offs[4]) * (DH_B ** -0.5 * LOG2E)], axis=1).astype(BF16),
        kva=seg(offs[1], offs[3]).astype(BF16),
        nsa=seg(offs[4], offs[4] + 4 * gw).astype(BF16),
        win=seg(offs[4] + 4 * gw, offs[5]).astype(BF16),
        gb=jnp.pad(seg(offs[5], offs[6]), ((0, 0), (0, LANES - 3 * H_B))).astype(BF16),
        gm=seg(offs[6], offs[7]).astype(BF16))


def _in_projection(u, w, tm, q_dtype=BF16, kv_token_rows=False):
    zq, = _matmul(u, w["q"], (q_dtype,), tm, 512, "proj_q")
    if kv_token_rows:
        kva32, kva16 = _matmul_token_rows(u, w["kva"], 512, "proj_kva")
    else:
        kva32, kva16 = _matmul(u, w["kva"], (F32, BF16), tm, 512, "proj_kva")
    nsa32, = _matmul(u, w["nsa"], (F32,), tm, 512, "proj_nsa")
    win32, = _matmul(u, w["win"], (F32,), tm, 512, "proj_win")
    gb, = _matmul(u, w["gb"], (F32,), tm, LANES, "proj_gb")
    gm, = _matmul(u, w["gm"], (F32,), tm, 512, "proj_gm")
    return zq, kva32, kva16, nsa32, win32, gb, gm


def _prompt_mixer(u, w, lam_qk, gain, wexp, b, t, lam_init, tq=256):
    tq = min(tq, t)
    zq, kva32, kva16, nsa32, win32, gb, gm = _in_projection(u, w, 1024, kv_token_rows=True)
    nsa_t = _matmul_tokens_last(w["nsa"].T, u, b, 512, 1024, "proj_nsa_t")
    tbig = min(2 * tq, t)
    oa = _diff_prompt(zq, kva16, lam_qk, gain, b, t, tbig, lam_init)
    kcd, vcd, ksa, vsd, kwd, vwd = _nsa_prep(nsa32, win32, wexp, b, t, min(512, t))
    ocmp, qaug = _cmp_prompt(zq, kcd, vcd, gb, b, t, tq)
    osel = _sel_prompt(qaug, ksa, vsd, gb, b, t, tbig)
    owin = _win_prompt(zq, kwd, vwd, gb, b, t, tbig)
    return (oa, ocmp, osel, owin, gm), (kva32, nsa_t, win32)


def _pad_rows(x, rows):
    return jnp.concatenate([x, jnp.zeros((rows - x.shape[0], x.shape[1]), x.dtype)], axis=0)


def _diff_sample_kernel(pt_ref, q_ref, new_ref, lam_ref, gain_ref, *rest, tn, pps, past_len, lam_init):
    page_refs = rest[:pps]
    o_ref, qs_ref, m_ref, l_ref, acc_ref = rest[pps:]
    c = pl.program_id(1)
    rows = 2 * R_A * tn
    hw = HKV_A * 2 * DH_A
    stride = 2 * HKV_A
    row = _iota((rows, 1), 0)
    tq = row & (tn - 1)

    def slopes(hkv):
        head = hkv * R_A + (row >> _log2(2 * tn))
        return jnp.exp2(-8.0 * (head + 1).astype(F32) / H_A) * LOG2E

    def update(hkv, s, pv_fn):
        m, l, acc = _flash_step(s, None, m_ref[hkv], l_ref[hkv], acc_ref[hkv], pv_fn)
        m_ref[hkv] = m
        l_ref[hkv] = l
        acc_ref[hkv] = acc

    @pl.when(c == 0)
    def _():
        upper = _iota((tn, LANES), 1) >= HALF
        for hkv in range(HKV_A):
            parts = []
            for r in range(R_A):
                h = hkv * R_A + r
                qv = q_ref[:, h * LANES:(h + 1) * LANES]
                parts += [jnp.where(upper == bool(cc), qv, 0.0) for cc in range(2)]
            qs_ref[hkv] = jnp.concatenate(parts, axis=0).astype(BF16)
        _flash_init(m_ref, l_ref, acc_ref)

    kpos = c * (pps * PAGE_SIZE) + _iota((1, pps * PAGE_SIZE), 1)
    dist = ((past_len + tq) - kpos).astype(F32)
    scores = []
    for hkv in range(HKV_A):
        q = qs_ref[hkv]
        s = jnp.concatenate(
            [_nt_dot(q, pr[pl.ds(hkv, PAGE_SIZE, stride=stride), :].astype(BF16)) for pr in page_refs], axis=1)
        scores.append(s - slopes(hkv) * dist)
    for hkv in range(HKV_A):

        def pv(p, hkv=hkv):
            out = None
            for n, pr in enumerate(page_refs):
                vals = pr[pl.ds(HKV_A + hkv, PAGE_SIZE, stride=stride), :].astype(BF16)
                part = jnp.dot(p[:, n * PAGE_SIZE:(n + 1) * PAGE_SIZE], vals, preferred_element_type=F32)
                out = part if out is None else out + part
            return out

        update(hkv, scores[hkv], pv)

    @pl.when(c == pl.num_programs(1) - 1)
    def _():
        lam = _diff_lambda(lam_ref, lam_init)
        newp = _pad_rows(new_ref[...], LANES)
        dnew = tq - _iota((1, LANES), 1)
        for hkv in range(HKV_A):
            c0 = hkv * LANES
            s = _nt_dot(qs_ref[hkv], newp[:, c0:c0 + LANES].astype(BF16))
            s = jnp.where(dnew >= 0, s - slopes(hkv) * dnew.astype(F32), NEG_INF)
            vn = newp[:, hw + c0:hw + c0 + LANES].astype(BF16)
            update(hkv, s, lambda p, vn=vn: jnp.dot(p, vn, preferred_element_type=F32))
            o = acc_ref[hkv] / _row_total(l_ref[hkv])
            for r in range(R_A):
                b0 = 2 * r * tn
                y = _diff_finish(o[b0:b0 + tn], o[b0 + tn:b0 + 2 * tn], lam, gain_ref[...], lam_init)
                h = hkv * R_A + r
                o_ref[:, h * LANES:(h + 1) * LANES] = y


def _diff_sample(zq, kva_new, cache, pt, lam_qk, gain, bd, tn, n_pages, pps, lam_init):
    rows = 2 * R_A * tn
    page_rows = PAGE_SIZE * 2 * HKV_A
    kern = functools.partial(_diff_sample_kernel, tn=tn, pps=pps, past_len=n_pages * PAGE_SIZE,
                             lam_init=lam_init)

    def page_spec(p):
        return pl.BlockSpec((page_rows, LANES), lambda bb, c, pt: (pt[bb * n_pages + c * pps + p], 0))

    grid_spec = pltpu.PrefetchScalarGridSpec(
        num_scalar_prefetch=1,
        grid=(bd, n_pages // pps),
        in_specs=[pl.BlockSpec((tn, W_A), lambda bb, c, pt: (bb, 0)),
                  pl.BlockSpec((tn, W_A), lambda bb, c, pt: (bb, 0)),
                  pl.BlockSpec((4, DH_A), lambda bb, c, pt: (0, 0)),
                  pl.BlockSpec((1, 2 * DH_A), lambda bb, c, pt: (0, 0))]
                 + [page_spec(p) for p in range(pps)],
        out_specs=pl.BlockSpec((tn, W_A), lambda bb, c, pt: (bb, 0)),
        scratch_shapes=[pltpu.VMEM((HKV_A, rows, LANES), BF16),
                        pltpu.VMEM((HKV_A, rows, LANES), F32),
                        pltpu.VMEM((HKV_A, rows, LANES), F32),
                        pltpu.VMEM((HKV_A, rows, LANES), F32)])
    return pl.pallas_call(
        kern, grid_spec=grid_spec,
        out_shape=jax.ShapeDtypeStruct((bd * tn, W_A), F32),
        compiler_params=_params(("parallel", "arbitrary")),
        name="diff_sample",
    )(pt, zq, kva_new, lam_qk, gain.reshape(1, 2 * DH_A), *([cache] * pps))


def _sample_queries(qblk, tn):
    low = _iota((tn, LANES), 1) < HALF
    zero = jnp.zeros((tn, LANES), F32)
    cols = ([], [])
    for r in range(R_B):
        for g in range(G_B):
            h = g * R_B + r
            qv = qblk[:, (h // 2) * LANES:(h // 2 + 1) * LANES]
            if h % 2 != g % 2:
                qv = pltpu.roll(qv, HALF, 1)
            cols[g // 2].append(jnp.where(low == (g % 2 == 0), qv, 0.0))
            cols[1 - g // 2].append(zero)
    return jnp.concatenate([jnp.concatenate(c, axis=0) for c in cols], axis=1).astype(BF16)


def _sample_outputs(o_ref, o_all, sig, branch, tn):
    low = _iota((tn, LANES), 1) < HALF
    for g in range(G_B):
        for k in range(R_B // 2):
            parts = []
            for r in (2 * k, 2 * k + 1):
                r0 = (r * G_B + g) * tn
                src = o_all[r0:r0 + tn, (g // 2) * LANES:(g // 2 + 1) * LANES]
                if g % 2 != r % 2:
                    src = pltpu.roll(src, HALF, 1)
                col = branch * H_B + g * R_B + r
                parts.append(src * sig[:, col:col + 1])
            c0 = (g * (R_B // 2) + k) * LANES
            o_ref[:, c0:c0 + LANES] = jnp.where(low, parts[0], parts[1])


def _query_consts(idx, tn, past_len):
    r_q = idx >> _log2(G_B * tn)
    g_q = (idx >> _log2(tn)) & (G_B - 1)
    slope = jnp.exp2(-8.0 * (g_q * R_B + r_q + 1).astype(F32) / H_B) * LOG2E
    return g_q >> 1, slope, past_len + (idx & (tn - 1))


def _topk_member(sc):
    ridx = _iota(sc.shape, 0)
    removed = -3.0e38

    def body(_, carry):
        cur, mem = carry
        top = jnp.max(cur, axis=0, keepdims=True)
        first = jnp.min(jnp.where(cur == top, ridx, sc.shape[0]), axis=0, keepdims=True)
        pick = ridx == first
        return jnp.where(pick, removed, cur), jnp.where(pick, 1.0, mem)

    return lax.fori_loop(0, TOP_N, body, (sc, jnp.zeros(sc.shape, F32)), unroll=True)[1]


def _col_of(row_vec):
    return jnp.broadcast_to(row_vec, (LANES, LANES)).T


def _nsa_sample_kernel(pt_ref, q_ref, new_ref, gb_ref, wc_ref, e_ref, tile_ref, step_ref, *rest,
                       tn, pps, past_len):
    page_refs = rest[:pps]
    (ocmp_ref, osel_ref, qop_ref, kcvc_ref, imp_ref, sc_ref, selb_ref,
     m_ref, l_ref, acc_ref) = rest[pps:]
    ph = pl.program_id(1)
    c = pl.program_id(2)
    last = pl.num_programs(2) - 1
    gw = G_B * DH_B
    ncp = past_len // CMP_BLOCK
    nsp = past_len // SEL_BLOCK
    ck = pps * PAGE_SIZE
    lane_consts = lambda: _query_consts(_iota((1, LANES), 1), tn, past_len)
    row_consts = lambda: _query_consts(_iota((LANES, 1), 0), tn, past_len)

    @pl.when((ph == 0) & (c == 0))
    def _():
        qop_ref[...] = _sample_queries(q_ref[...], tn)

    @pl.when(ph == 0)
    def _():
        wc = wc_ref[...].astype(BF16)
        low8 = _iota((8, LANES), 1) < HALF
        comps = [_nt_dot(wc, jnp.concatenate([page_refs[2 * pp][...], page_refs[2 * pp + 1][...]],
                                             axis=1).astype(BF16)) for pp in range(pps // 2)]
        for pp, comp in enumerate(comps):
            row0 = pl.multiple_of((c * (pps // 2) + pp) * 8, 8)
            for slot in range(2):
                for p in range(2):
                    va = slot * G_B + 2 * p
                    c0 = slot * gw + p * LANES
                    piece = jnp.where(low8, comp[va * 8:(va + 1) * 8, c0:c0 + LANES],
                                      comp[(va + 1) * 8:(va + 2) * 8, c0:c0 + LANES])
                    kcvc_ref[pl.ds(row0, 8), c0:c0 + LANES] = piece

    @pl.when((ph == 0) & (c == last))
    def _():
        _, slope_l, qpos_l = lane_consts()
        kcvc = kcvc_ref[...]
        s = _nt_dot(kcvc[:, 0:gw].astype(BF16), qop_ref[...])
        blk = _iota((ncp, LANES), 0)
        c_mid = (blk * CMP_BLOCK).astype(F32) + (CMP_BLOCK - 1) * 0.5
        valid = blk * CMP_BLOCK + (CMP_BLOCK - 1) <= qpos_l
        s = s - slope_l * jnp.abs(qpos_l.astype(F32) - c_mid)
        sm = jnp.where(valid, s, NEG_INF)
        e = jnp.exp2(sm - jnp.max(sm, axis=0, keepdims=True))
        p_t = jnp.where(valid, e / jnp.sum(e, axis=0, keepdims=True), 0.0)
        p = p_t.T.astype(BF16)
        o_all = jnp.dot(p, kcvc[:, gw:2 * gw].astype(BF16), preferred_element_type=F32)
        _sample_outputs(ocmp_ref, o_all, jax.nn.sigmoid(gb_ref[...]), 0, tn)
        imp = p_t
        for k in range(1, R_B):
            imp = imp + pltpu.roll(p_t, k * G_B * tn, 1)
        imp_ref[...] = imp
        imp = imp_ref[pl.ds(0, nsp, stride=2), :] + imp_ref[pl.ds(1, nsp, stride=2), :]
        sidx = _iota((nsp, LANES), 0)
        forced = (sidx == (qpos_l >> _log2(SEL_BLOCK))) | (sidx == 0)
        sc_ref[0:nsp, :] = jnp.where(forced, FORCE_SCORE,
                                     jnp.where(sidx * SEL_BLOCK <= qpos_l, imp, -1.0))
        tail = sc_ref.shape[0] - nsp
        tidx = nsp + _iota((tail, LANES), 0)
        tforced = (tidx == (qpos_l >> _log2(SEL_BLOCK))) | (tidx == 0)
        tsc = jnp.where(tforced, FORCE_SCORE, jnp.where(tidx * SEL_BLOCK <= qpos_l, 0.0, -1.0))
        sc_ref[nsp:, :] = jnp.where(tidx == nsp, tsc, NEG_INF)
        selb_ref[...] = _topk_member(sc_ref[...])
        _flash_init(m_ref, l_ref, acc_ref)

    def sel_step(kt, vt, shift, s_bias_fn):
        s = s_bias_fn(jnp.dot(qop_ref[...], kt, preferred_element_type=F32))
        m, l, acc = _flash_step(s, shift, m_ref[...], l_ref[...], acc_ref[...], lambda p: _nt_dot(p, vt))
        m_ref[...] = m
        l_ref[...] = l
        acc_ref[...] = acc

    @pl.when(ph == 1)
    def _():
        kt = jnp.concatenate([pr[0:gw, :] for pr in page_refs], axis=1).astype(BF16)
        vt = jnp.concatenate([pr[gw:2 * gw, :] for pr in page_refs], axis=1).astype(BF16)
        nblk = ck // SEL_BLOCK
        member = _pad_rows(selb_ref[pl.ds(pl.multiple_of(c * nblk, 8), nblk), :], LANES).T
        mask = jnp.dot(member.astype(BF16), e_ref[...], preferred_element_type=F32)
        shift = step_ref[...] * (last - c).astype(F32)
        sel_step(kt, vt, shift, lambda s: jnp.where(mask > 0.5, s + tile_ref[...], NEG_INF))

    @pl.when((ph == 1) & (c == last))
    def _():
        xn = _pad_rows(new_ref[:, 2 * gw:4 * gw], LANES)
        kt = jnp.concatenate([xn[:, q * LANES:(q + 1) * LANES].T for q in range(2)], axis=0).astype(BF16)
        vt = jnp.concatenate([xn[:, gw + q * LANES:gw + (q + 1) * LANES].T for q in range(2)], axis=0).astype(BF16)
        _, slope_r, qpos_r = row_consts()
        member = _col_of(selb_ref[nsp:nsp + 1, :])
        col = _iota((1, LANES), 1)
        dist = qpos_r - (past_len + col)
        ok = (member > 0.5) & (dist >= 0) & (col < tn)
        sel_step(kt, vt, None, lambda s: jnp.where(ok, s - slope_r * dist.astype(F32), NEG_INF))
        _sample_outputs(osel_ref, acc_ref[...] / _row_total(l_ref[...]), jax.nn.sigmoid(gb_ref[...]), 1, tn)


def _compress_weights(w_cmp):
    nblk = 2 * PAGE_SIZE // CMP_BLOCK
    tok = jnp.arange(2 * PAGE_SIZE)
    w = jnp.transpose(w_cmp, (0, 2, 1))[:, :, tok % CMP_BLOCK]
    hit = (tok[None, :] // CMP_BLOCK) == jnp.arange(nblk)[:, None]
    return jnp.where(hit[None, None], w[:, :, None, :], 0.0).reshape(2 * G_B * nblk, 2 * PAGE_SIZE)


def _nsa_sample(zq, nsa_new, gb, w_cmp, cache, pt, bd, tn, n_pages, pps):
    past_len = n_pages * PAGE_SIZE
    gw = G_B * DH_B
    ncp = past_len // CMP_BLOCK
    nsp = past_len // SEL_BLOCK
    nsc = -(-(nsp + 1) // 8) * 8
    ck = pps * PAGE_SIZE
    assert pps % 2 == 0 and 2 * PAGE_SIZE // CMP_BLOCK == 8 and (ck // SEL_BLOCK) % 8 == 0
    kern = functools.partial(_nsa_sample_kernel, tn=tn, pps=pps, past_len=past_len)
    expand = (jnp.arange(ck)[None, :] // SEL_BLOCK == jnp.arange(LANES)[:, None]).astype(BF16)
    qrow = jnp.arange(LANES)
    slope_r = _slopes2(H_B)[((qrow // tn) % G_B) * R_B + qrow // (G_B * tn)]
    tile = -slope_r[:, None] * ((qrow % tn)[:, None] + ck - jnp.arange(ck)[None, :]).astype(F32)
    step = jnp.broadcast_to((slope_r * ck)[:, None], (LANES, LANES))

    def page_spec(p):
        return pl.BlockSpec((2 * gw, PAGE_SIZE),
                            lambda bb, ph, c, pt, p=p: (pt[bb * n_pages + c * pps + p] * 2 + ph, 0))

    row_spec = lambda width, col: pl.BlockSpec((tn, width), lambda bb, ph, c, pt: (bb, col))
    grid_spec = pltpu.PrefetchScalarGridSpec(
        num_scalar_prefetch=1,
        grid=(bd, 2, n_pages // pps),
        in_specs=[row_spec(W_B, W_A // W_B), row_spec(4 * gw, 0), row_spec(LANES, 0),
                  pl.BlockSpec((2 * G_B * 8, 2 * PAGE_SIZE), lambda bb, ph, c, pt: (0, 0)),
                  pl.BlockSpec((LANES, ck), lambda bb, ph, c, pt: (0, 0)),
                  pl.BlockSpec((LANES, ck), lambda bb, ph, c, pt: (0, 0)),
                  pl.BlockSpec((LANES, LANES), lambda bb, ph, c, pt: (0, 0))]
                 + [page_spec(p) for p in range(pps)],
        out_specs=[row_spec(W_B, 0), row_spec(W_B, 0)],
        scratch_shapes=[pltpu.VMEM((LANES, gw), BF16),
                        pltpu.VMEM((ncp, 2 * gw), F32),
                        pltpu.VMEM((ncp, LANES), F32),
                        pltpu.VMEM((nsc, LANES), F32),
                        pltpu.VMEM((nsc, LANES), F32),
                        pltpu.VMEM((LANES, LANES), F32),
                        pltpu.VMEM((LANES, LANES), F32),
                        pltpu.VMEM((LANES, gw), F32)])
    return pl.pallas_call(
        kern, grid_spec=grid_spec,
        out_shape=[jax.ShapeDtypeStruct((bd * tn, W_B), F32)] * 2,
        compiler_params=_params(("parallel", "arbitrary", "arbitrary")),
        name="nsa_sample",
    )(pt, zq, nsa_new, gb, _compress_weights(w_cmp), expand, tile, step, *([cache] * pps))


def _win_sample_kernel(q_ref, new_ref, st_ref, gb_ref, o_ref, ns_ref, *, tn, past_len):
    gw = G_B * DH_B
    wb = st_ref.shape[2]
    qs = _sample_queries(q_ref[...], tn)
    st = st_ref[0]
    new_t = _pad_rows(new_ref[...], LANES).T
    _, slope_r, qpos_r = _query_consts(_iota((LANES, 1), 0), tn, past_len)

    def scores(x):
        return jnp.dot(qs, x[0:gw, :].astype(BF16), preferred_element_type=F32)

    kpos_s = (past_len - wb) + _iota((1, wb), 1)
    d_s = qpos_r - kpos_s
    s_s = jnp.where((d_s >= 0) & (d_s < WINDOW) & (kpos_s >= 0), scores(st) - slope_r * d_s.astype(F32), NEG_INF)
    col_n = _iota((1, LANES), 1)
    d_n = qpos_r - (past_len + col_n)
    s_n = jnp.where((d_n >= 0) & (d_n < WINDOW) & (col_n < tn), scores(new_t) - slope_r * d_n.astype(F32), NEG_INF)
    s = jnp.concatenate([s_s, s_n], axis=1)
    e = jnp.exp2(s - jnp.max(s, axis=-1, keepdims=True))
    p = (e / jnp.sum(e, axis=-1, keepdims=True)).astype(BF16)
    o_all = (_nt_dot(p[:, :wb], st[gw:2 * gw, :].astype(BF16))
             + _nt_dot(p[:, wb:], new_t[gw:2 * gw, :].astype(BF16)))
    _sample_outputs(o_ref, o_all, jax.nn.sigmoid(gb_ref[...]), 2, tn)
    shifted = pltpu.roll(st, wb - tn, 1)
    tail = jnp.concatenate([jnp.zeros((2 * gw, wb - LANES), F32), pltpu.roll(new_t, LANES - tn, 1)], axis=1)
    ns_ref[0] = jnp.where(_iota((2 * gw, wb), 1) >= wb - tn, tail, shifted)


def _win_sample(zq, win_new, state_t, gb, bd, tn, past_len):
    gw = G_B * DH_B
    wb = state_t.shape[2]
    assert wb == WINDOW and wb == min(WINDOW, past_len) and wb % LANES == 0
    kern = functools.partial(_win_sample_kernel, tn=tn, past_len=past_len)
    return pl.pallas_call(
        kern, grid=(bd,),
        in_specs=[pl.BlockSpec((tn, W_B), lambda bb: (bb, W_A // W_B)),
                  pl.BlockSpec((tn, 2 * gw), lambda bb: (bb, 0)),
                  pl.BlockSpec((1, 2 * gw, wb), lambda bb: (bb, 0, 0)),
                  pl.BlockSpec((tn, LANES), lambda bb: (bb, 0))],
        out_specs=[pl.BlockSpec((tn, W_B), lambda bb: (bb, 0)),
                   pl.BlockSpec((1, 2 * gw, wb), lambda bb: (bb, 0, 0))],
        out_shape=[jax.ShapeDtypeStruct((bd * tn, W_B), F32),
                   jax.ShapeDtypeStruct((bd, 2 * gw, wb), F32)],
        compiler_params=_params(("parallel",)),
        name="win_sample",
    )(zq, win_new, state_t, gb)


def kernel(x_prompt, x_sample, cache_diff_kv, cache_nsa_kv, state_win_kv, page_table, w_in, w_proj_a,
           w_proj_b, w_out, lambda_qk, diff_gain, w_cmp, norm_attn, norm_mlp, w_up, w_down, norm_final):
    depth = w_in.shape[0]
    assert depth == 1
    b, t, d = x_prompt.shape
    bd, tn, _ = x_sample.shape
    l = 0
    lam_init = 0.8 - 0.6 * math.exp(-0.3 * l)
    w = _split_w_in(w_in[l])
    wpa, wpb, wo = w_proj_a[l].astype(BF16), w_proj_b[l].astype(BF16), w_out[l].astype(BF16)
    wup, wdn = w_up[l].astype(BF16), w_down[l].astype(BF16)
    wexp = jnp.repeat(w_cmp[l], DH_B, axis=-1)

    xp = x_prompt.reshape(b * t, d)
    up = _rmsnorm(xp, norm_attn[l], BF16, 512)
    (oa, ocmp, osel, owin, gm), (kva32, nsa_t, win32) = _prompt_mixer(
        up, w, lambda_qk[l], diff_gain[l], wexp, b, t, lam_init)
    hp, u2 = _post_attention(oa, ocmp, osel, owin, gm, xp, wpa, wpb, wo, norm_mlp[l], 256)
    y_prompt = _mlp_final(u2, hp, wup, wdn, norm_final, 512, 512).reshape(b, t, d)
    keep = min(WINDOW, t)
    diff_kv_prompt = kva32.reshape(1, b, t, 2, HKV_A, 2 * DH_A)
    nsa_kv_prompt = jnp.transpose(nsa_t.reshape(b, 4, G_B, DH_B, t), (0, 4, 1, 2, 3))[None]
    win_kv_prompt = win32.reshape(b, t, 2, G_B, DH_B)[None, :, t - keep:]

    n_pages = page_table.shape[1]
    past_len = n_pages * PAGE_SIZE
    n_pool = cache_diff_kv.shape[1]
    pps = 16
    assert R_B * G_B * tn == LANES and tn < CMP_BLOCK and n_pages % pps == 0
    xs = x_sample.reshape(bd * tn, d)
    us = _rmsnorm(xs, norm_attn[l], BF16, bd * tn)
    zq_s, kva_s, _, nsa_s, win_s, gb_s, gm_s = _in_projection(us, w, bd * tn, F32)
    pt = page_table.reshape(-1)
    cache_d = cache_diff_kv[l].reshape(n_pool * PAGE_SIZE * 2 * HKV_A, 2 * DH_A)
    cache_n = jnp.transpose(cache_nsa_kv[l], (0, 2, 3, 4, 1)).reshape(n_pool * 4 * G_B * DH_B, PAGE_SIZE)
    wb = state_win_kv.shape[2]
    state_t = jnp.transpose(state_win_kv[l], (0, 2, 3, 4, 1)).reshape(bd, 2 * G_B * DH_B, wb)
    oa_s = _diff_sample(zq_s, kva_s, cache_d, pt, lambda_qk[l], diff_gain[l], bd, tn, n_pages, pps, lam_init)
    ocmp_s, osel_s = _nsa_sample(zq_s, nsa_s, gb_s, w_cmp[l], cache_n, pt, bd, tn, n_pages, pps)
    owin_s, new_state_t = _win_sample(zq_s, win_s, state_t, gb_s, bd, tn, past_len)
    hs, u2s = _post_attention(oa_s, ocmp_s, osel_s, owin_s, gm_s, xs, wpa, wpb, wo, norm_mlp[l], 256)
    y_sample = _mlp_final(u2s, hs, wup, wdn, norm_final, 512, 512).reshape(bd, tn, d)
    diff_kv_sample = kva_s.reshape(1, bd, tn, 2, HKV_A, 2 * DH_A)
    nsa_kv_sample = nsa_s.reshape(1, bd, tn, 4, G_B, DH_B)
    win_kv_sample = jnp.transpose(new_state_t.reshape(bd, 2, G_B, DH_B, wb), (0, 4, 1, 2, 3))[None]
    return (y_prompt, y_sample, diff_kv_prompt, nsa_kv_prompt, win_kv_prompt,
            diff_kv_sample, nsa_kv_sample, win_kv_sample)
```

```python
import functools
import math

import numpy as np
import jax
import jax.numpy as jnp
from jax import lax
from jax.experimental import pallas as pl
from jax.experimental.pallas import tpu as pltpu

F32, BF16, I32 = jnp.float32, jnp.bfloat16, jnp.int32

PAGE_SIZE = 128
H_A, HKV_A, R_A, DH_A = 8, 4, 2, 64
W_A = H_A * 2 * DH_A
H_B, G_B, R_B, DH_B = 16, 4, 4, 64
W_B = H_B * DH_B
CMP_BLOCK, SEL_BLOCK, TOP_N, WINDOW = 32, 64, 16, 512
EPS = 1e-6
NEG_INF = -1e30
FORCE_SCORE = 1e4
LOG2E = math.log2(math.e)
LANES = 128
HALF = LANES // 2
VMEM_LIMIT = 56 * 1024 * 1024


def _iota(shape, dim):
    return lax.broadcasted_iota(I32, shape, dim)


def _log2(n):
    assert n & (n - 1) == 0, n
    return n.bit_length() - 1


def _nt_dot(a, b):
    return lax.dot_general(a, b, (((1,), (1,)), ((), ())), preferred_element_type=F32)


def _params(sem):
    return pltpu.CompilerParams(dimension_semantics=sem, vmem_limit_bytes=VMEM_LIMIT)


def _tri_tiles(n):
    ii = np.array([i for i in range(n) for _ in range(i + 1)], np.int32)
    jj = np.array([j for i in range(n) for j in range(i + 1)], np.int32)
    return jnp.asarray(ii), jnp.asarray(jj)


def _slopes2(n):
    return jnp.exp2(-8.0 * jnp.arange(1, n + 1, dtype=F32) / n) * LOG2E


def _rmsnorm_kernel(x_ref, g_ref, o_ref):
    x = x_ref[...]
    y = x * lax.rsqrt(jnp.mean(x * x, axis=-1, keepdims=True) + EPS)
    o_ref[...] = (y * g_ref[...]).astype(o_ref.dtype)


def _rmsnorm(x, g, out_dtype, tm):
    m, d = x.shape
    return pl.pallas_call(
        _rmsnorm_kernel,
        grid=(m // tm,),
        in_specs=[pl.BlockSpec((tm, d), lambda i: (i, 0)),
                  pl.BlockSpec((1, d), lambda i: (0, 0))],
        out_specs=pl.BlockSpec((tm, d), lambda i: (i, 0)),
        out_shape=jax.ShapeDtypeStruct((m, d), out_dtype),
        compiler_params=_params(("parallel",)),
        name="rmsnorm",
    )(x, g.reshape(1, d))


def _mm_kernel(a_ref, w_ref, *o_refs):
    acc = jnp.dot(a_ref[...], w_ref[...], preferred_element_type=F32)
    for o in o_refs:
        o[...] = acc.astype(o.dtype)


def _matmul(a, w, out_dtypes, tm, tn, name):
    m, k = a.shape
    n = w.shape[1]
    tm, tn = min(tm, m), min(tn, n)
    outs = pl.pallas_call(
        _mm_kernel,
        grid=(n // tn, m // tm),
        in_specs=[pl.BlockSpec((tm, k), lambda j, i: (i, 0)),
                  pl.BlockSpec((k, tn), lambda j, i: (0, j))],
        out_specs=[pl.BlockSpec((tm, tn), lambda j, i: (i, j)) for _ in out_dtypes],
        out_shape=[jax.ShapeDtypeStruct((m, n), dt) for dt in out_dtypes],
        compiler_params=_params(("parallel", "parallel")),
        name=name,
    )(a, w)
    return outs


def _mm_tokens_last_kernel(w_ref, a_ref, o_ref):
    o_ref[...] = _nt_dot(w_ref[...], a_ref[...])


def _matmul_tokens_last(w_t, a, b, tn, tm, name):
    n, k = w_t.shape
    t = a.shape[0] // b
    tn, tm = min(tn, n), min(tm, t)
    return pl.pallas_call(
        _mm_tokens_last_kernel,
        grid=(b, n // tn, t // tm),
        in_specs=[pl.BlockSpec((tn, k), lambda bb, j, i: (j, 0)),
                  pl.BlockSpec((tm, k), lambda bb, j, i: (bb * (t // tm) + i, 0))],
        out_specs=pl.BlockSpec((tn, tm), lambda bb, j, i: (bb * (n // tn) + j, i)),
        out_shape=jax.ShapeDtypeStruct((b * n, t), F32),
        compiler_params=_params(("parallel", "parallel", "parallel")),
        name=name,
    )(w_t, a)


def _mm_token_rows_kernel(a_ref, w_ref, o32_ref, o16_ref):
    acc = jnp.dot(a_ref[...], w_ref[...], preferred_element_type=F32)
    o16_ref[...] = acc.astype(o16_ref.dtype)
    nrow = acc.shape[1] // LANES
    for j in range(nrow):
        o32_ref[pl.ds(j, acc.shape[0], stride=nrow), :] = acc[:, j * LANES:(j + 1) * LANES]


def _matmul_token_rows(a, w, tm, name):
    m, k = a.shape
    n = w.shape[1]
    tm = min(tm, m)
    return pl.pallas_call(
        _mm_token_rows_kernel,
        grid=(m // tm,),
        in_specs=[pl.BlockSpec((tm, k), lambda i: (i, 0)),
                  pl.BlockSpec((k, n), lambda i: (0, 0), pipeline_mode=pl.Buffered(1))],
        out_specs=[pl.BlockSpec((tm * (n // LANES), LANES), lambda i: (i, 0)),
                   pl.BlockSpec((tm, n), lambda i: (i, 0))],
        out_shape=[jax.ShapeDtypeStruct((m * (n // LANES), LANES), F32),
                   jax.ShapeDtypeStruct((m, n), BF16)],
        compiler_params=_params(("parallel",)),
        name=name,
    )(a, w)


def _post_kernel(oa_ref, o1_ref, o2_ref, o3_ref, gm0_ref, gm1_ref, x_ref,
                 wpa_ref, wpb_ref, wo_ref, g_ref, h_ref, u_ref):
    oa = oa_ref[...].astype(BF16)
    ob = (o1_ref[...] + o2_ref[...] + o3_ref[...]).astype(BF16)
    pa = jnp.dot(oa, wpa_ref[...], preferred_element_type=F32)
    pb = jnp.dot(ob, wpb_ref[...], preferred_element_type=F32)
    mix = jax.nn.sigmoid(gm0_ref[...]) * pa + jax.nn.sigmoid(gm1_ref[...]) * pb
    y = jnp.dot(mix.astype(BF16), wo_ref[...], preferred_element_type=F32)
    h = x_ref[...] + y
    h_ref[...] = h
    u = h * lax.rsqrt(jnp.mean(h * h, axis=-1, keepdims=True) + EPS)
    u_ref[...] = (u * g_ref[...]).astype(u_ref.dtype)


def _post_attention(oa, o1, o2, o3, gm, x, wpa, wpb, wo, g_mlp, tm):
    m, d = x.shape
    tm = min(tm, m)
    row = lambda i: (i, 0)
    const = lambda i: (0, 0)
    once = pl.Buffered(1)
    return pl.pallas_call(
        _post_kernel,
        grid=(m // tm,),
        in_specs=[pl.BlockSpec((tm, W_A), row),
                  pl.BlockSpec((tm, W_B), row),
                  pl.BlockSpec((tm, W_B), row),
                  pl.BlockSpec((tm, W_B), row),
                  pl.BlockSpec((tm, d), lambda i: (i, 0)),
                  pl.BlockSpec((tm, d), lambda i: (i, 1)),
                  pl.BlockSpec((tm, d), row),
                  pl.BlockSpec((W_A, d), const, pipeline_mode=once),
                  pl.BlockSpec((W_B, d), const, pipeline_mode=once),
                  pl.BlockSpec((d, d), const, pipeline_mode=once),
                  pl.BlockSpec((1, d), const)],
        out_specs=[pl.BlockSpec((tm, d), row), pl.BlockSpec((tm, d), row)],
        out_shape=[jax.ShapeDtypeStruct((m, d), F32), jax.ShapeDtypeStruct((m, d), BF16)],
        compiler_params=_params(("parallel",)),
        name="post_attention",
    )(oa, o1, o2, o3, gm, gm, x, wpa, wpb, wo, g_mlp.reshape(1, d))


def _mlp_kernel(u_ref, h_ref, wup_ref, wdn_ref, g_ref, o_ref, acc_ref):
    f = pl.program_id(1)

    @pl.when(f == 0)
    def _():
        acc_ref[...] = jnp.zeros_like(acc_ref)

    a = jnp.dot(u_ref[...], wup_ref[...], preferred_element_type=F32)
    a = jnp.square(jnp.maximum(a, 0.0)).astype(BF16)
    acc_ref[...] += jnp.dot(a, wdn_ref[...], preferred_element_type=F32)

    @pl.when(f == pl.num_programs(1) - 1)
    def _():
        y = h_ref[...] + acc_ref[...]
        y = y * lax.rsqrt(jnp.mean(y * y, axis=-1, keepdims=True) + EPS)
        o_ref[...] = y * g_ref[...]


def _mlp_final(u, h, wup, wdn, g_final, tm, tf):
    m, d = h.shape
    dff = wup.shape[1]
    tm, tf = min(tm, m), min(tf, dff)
    return pl.pallas_call(
        _mlp_kernel,
        grid=(m // tm, dff // tf),
        in_specs=[pl.BlockSpec((tm, d), lambda i, f: (i, 0)),
                  pl.BlockSpec((tm, d), lambda i, f: (i, 0)),
                  pl.BlockSpec((d, tf), lambda i, f: (0, f)),
                  pl.BlockSpec((tf, d), lambda i, f: (f, 0)),
                  pl.BlockSpec((1, d), lambda i, f: (0, 0))],
        out_specs=pl.BlockSpec((tm, d), lambda i, f: (i, 0)),
        out_shape=jax.ShapeDtypeStruct((m, d), F32),
        scratch_shapes=[pltpu.VMEM((tm, d), F32)],
        compiler_params=_params(("parallel", "arbitrary")),
        name="mlp_final",
    )(u, h, wup, wdn, g_final.reshape(1, d))


def _flash_init(m_ref, l_ref, acc_ref):
    m_ref[...] = jnp.full(m_ref.shape, NEG_INF, F32)
    if l_ref is not None:
        l_ref[...] = jnp.zeros(l_ref.shape, F32)
    acc_ref[...] = jnp.zeros(acc_ref.shape, F32)


def _lane_blocks_sum(p):
    out = p[:, 0:LANES]
    for n in range(1, p.shape[1] // LANES):
        out = out + p[:, n * LANES:(n + 1) * LANES]
    return out


def _flash_step(s, shift, m_prev, l_prev, acc_prev, pv_fn):
    m_cur = jnp.max(s, axis=-1, keepdims=True)
    if shift is not None:
        m_cur = m_cur - shift
    m_new = jnp.maximum(m_prev, m_cur)
    alpha = jnp.exp2(m_prev - m_new)
    m_adj = m_new if shift is None else m_new + shift
    p = jnp.exp2(s - jnp.concatenate([m_adj] * (s.shape[1] // LANES), axis=1))
    l_new = None if l_prev is None else alpha * l_prev + _lane_blocks_sum(p)
    alpha_acc = jnp.concatenate([alpha] * (acc_prev.shape[1] // LANES), axis=1)
    acc_new = alpha_acc * acc_prev + pv_fn(p.astype(BF16))
    return m_new, l_new, acc_new


def _flash_rows(rs, s, shift, v, m_ref, l_ref, acc_ref):
    m, l, acc = _flash_step(s, shift, m_ref[rs, :], None if l_ref is None else l_ref[rs, :], acc_ref[rs, :],
                            lambda p: jnp.dot(p, v, preferred_element_type=F32))
    m_ref[rs, :] = m
    if l_ref is not None:
        l_ref[rs, :] = l
    acc_ref[rs, :] = acc


def _row_total(l):
    return jnp.sum(l, axis=-1, keepdims=True)


def _alibi_tiles(slopes, tq):
    ng, nr = slopes.shape
    d = (jnp.arange(tq)[:, None] - jnp.arange(tq)[None, :]).astype(F32)
    off = -slopes[:, :, None, None] * d
    tiles = jnp.stack([off, jnp.where(d >= 0, off, NEG_INF)], axis=1).reshape(ng, 2, nr * tq, tq)
    step = jnp.broadcast_to((slopes * tq)[:, :, None, None], (ng, nr, tq, LANES)).reshape(ng, nr * tq, LANES)
    return tiles, step


def _window_tiles(slopes, tq, nj):
    ng, nr = slopes.shape
    d = jnp.arange(tq)[:, None] - jnp.arange(tq)[None, :]
    dist = (nj - 1 - jnp.arange(nj))[:, None, None] * tq + d[None]
    dist = dist[None, :, None]
    bias = jnp.where((dist >= 0) & (dist < WINDOW),
                     -slopes[:, None, :, None, None] * dist.astype(F32), NEG_INF)
    return bias.reshape(ng, nj, nr * tq, tq)


def _diff_lambda(lam_ref, lam_init):
    lf = lam_ref[...]
    e1 = jnp.exp(jnp.sum(lf[0:1] * lf[1:2], axis=-1, keepdims=True))
    e2 = jnp.exp(jnp.sum(lf[2:3] * lf[3:4], axis=-1, keepdims=True))
    return e1 - e2 + lam_init


def _diff_finish(o1, o2, lam, gain, lam_init):
    o = o1 - lam * o2
    y = o * lax.rsqrt(jnp.mean(o * o, axis=-1, keepdims=True) + EPS)
    return (y * gain) * (1.0 - lam_init)


def _diff_prompt_kernel(ii_ref, jj_ref, q_ref, k_ref, v_ref, tile_ref, step_ref, lam_ref, gain_ref, o_ref,
                        qs_ref, m_ref, l_ref, acc_ref, *, tq, lam_init):
    t = pl.program_id(2)
    i = ii_ref[t]
    j = jj_ref[t]
    nblk = 2 * R_A

    @pl.when(j == 0)
    def _():
        upper = _iota((tq, LANES), 1) >= HALF
        for r in range(R_A):
            qv = q_ref[:, r * LANES:(r + 1) * LANES]
            for c in range(2):
                qs_ref[(2 * r + c) * tq:(2 * r + c + 1) * tq, :] = jnp.where(upper == bool(c), qv, 0).astype(BF16)
        _flash_init(m_ref, l_ref, acc_ref)

    diag = (i == j).astype(I32)
    steps = (i - j).astype(F32)
    k = k_ref[...]
    v = v_ref[...]
    for n in range(nblk):
        rs = slice(n * tq, (n + 1) * tq)
        s = _nt_dot(qs_ref[rs, :], k) + tile_ref[0, diag, rs, :]
        _flash_rows(rs, s, step_ref[0, rs, :] * steps, v, m_ref, l_ref, acc_ref)

    @pl.when(j == i)
    def _():
        lam = _diff_lambda(lam_ref, lam_init)
        for r in range(R_A):
            b0 = 2 * r * tq
            o1 = acc_ref[b0:b0 + tq, :] / _row_total(l_ref[b0:b0 + tq, :])
            o2 = acc_ref[b0 + tq:b0 + 2 * tq, :] / _row_total(l_ref[b0 + tq:b0 + 2 * tq, :])
            y = _diff_finish(o1, o2, lam, gain_ref[...], lam_init)
            o_ref[:, r * LANES:(r + 1) * LANES] = y.astype(o_ref.dtype)


def _diff_prompt(qa, kva, lam_qk, gain, b, t, tq, lam_init):
    nq = t // tq
    ii, jj = _tri_tiles(nq)
    rows = 2 * R_A * tq
    slopes = jnp.repeat(_slopes2(H_A).reshape(HKV_A, R_A), 2, axis=1)
    tiles, step = _alibi_tiles(slopes, tq)
    kern = functools.partial(_diff_prompt_kernel, tq=tq, lam_init=lam_init)
    grid_spec = pltpu.PrefetchScalarGridSpec(
        num_scalar_prefetch=2,
        grid=(b, HKV_A, ii.shape[0]),
        in_specs=[pl.BlockSpec((tq, R_A * LANES), lambda bb, h, n, ii, jj: (bb * nq + ii[n], h)),
                  pl.BlockSpec((tq, LANES), lambda bb, h, n, ii, jj: (bb * nq + jj[n], h)),
                  pl.BlockSpec((tq, LANES), lambda bb, h, n, ii, jj: (bb * nq + jj[n], HKV_A + h)),
                  pl.BlockSpec((1, 2, rows, tq), lambda bb, h, n, ii, jj: (h, 0, 0, 0)),
                  pl.BlockSpec((1, rows, LANES), lambda bb, h, n, ii, jj: (h, 0, 0)),
                  pl.BlockSpec((4, DH_A), lambda bb, h, n, ii, jj: (0, 0)),
                  pl.BlockSpec((1, 2 * DH_A), lambda bb, h, n, ii, jj: (0, 0))],
        out_specs=pl.BlockSpec((tq, R_A * LANES), lambda bb, h, n, ii, jj: (bb * nq + ii[n], h)),
        scratch_shapes=[pltpu.VMEM((rows, LANES), BF16),
                        pltpu.VMEM((rows, LANES), F32),
                        pltpu.VMEM((rows, LANES), F32),
                        pltpu.VMEM((rows, LANES), F32)])
    return pl.pallas_call(
        kern, grid_spec=grid_spec,
        out_shape=jax.ShapeDtypeStruct((b * t, W_A), BF16),
        compiler_params=_params(("parallel", "parallel", "arbitrary")),
        name="diff_prompt",
    )(ii, jj, qa, kva, kva, tiles, step, lam_qk, gain.reshape(1, 2 * DH_A))


def _dup_half(v, g):
    rolled = pltpu.roll(v, HALF, 1)
    low = _iota(v.shape, 1) < HALF
    return jnp.where(low, v, rolled) if g % 2 == 0 else jnp.where(low, rolled, v)


def _value_ones(v, g):
    low = _iota(v.shape, 1) < HALF
    return jnp.where(low, v if g % 2 == 0 else pltpu.roll(v, HALF, 1), 1.0)


def _prep_kernel(nsa_ref, win_ref, wexp_ref, kcd_ref, vcd_ref, ksa_ref, vsd_ref, kwd_ref, vwd_ref,
                 tmp_ref, *, tp, nc):
    step = pl.program_id(1)
    gw = G_B * DH_B
    x = nsa_ref[...]
    w = win_ref[...]
    lane = _iota((tp, LANES), 1)
    low = lane < HALF
    blk = (step * tp + _iota((tp, LANES), 0)) >> _log2(SEL_BLOCK)
    onehot = jnp.where(lane - HALF == blk, 1.0, 0.0)
    for g in range(G_B):
        c0 = (g // 2) * LANES
        vk = x[:, 2 * gw + c0:2 * gw + c0 + LANES]
        klow = vk if g % 2 == 0 else pltpu.roll(vk, HALF, 1)
        ksa_ref[0, g] = jnp.where(low, klow, onehot).astype(BF16)
        vsd_ref[0, g] = _value_ones(x[:, 3 * gw + c0:3 * gw + c0 + LANES], g).astype(BF16)
        kwd_ref[0, g] = _dup_half(w[:, c0:c0 + LANES], g).astype(BF16)
        vwd_ref[0, g] = _value_ones(w[:, gw + c0:gw + c0 + LANES], g).astype(BF16)
    nb = tp // CMP_BLOCK
    hb = nb // 2
    for slot, dst in ((0, kcd_ref), (1, vcd_ref)):
        xc = x[:, slot * gw:(slot + 1) * gw]
        comp = jnp.sum(xc.reshape(nb, CMP_BLOCK, gw) * wexp_ref[slot][None], axis=1)
        for c in range(gw // LANES):
            tmp_ref[c * nb:(c + 1) * nb, :] = comp[:, c * LANES:(c + 1) * LANES]
        for par in range(2):
            start = pl.multiple_of(par * (nc // 2) + step * hb, 8)
            for g in range(G_B):
                rr = tmp_ref[pl.ds((g // 2) * nb + par, hb, stride=2), :]
                dst[0, g, pl.ds(start, hb), :] = _dup_half(rr, g)


def _nsa_prep(nsa, win, wexp, b, t, tp):
    nc = t // CMP_BLOCK
    gw = G_B * DH_B
    kern = functools.partial(_prep_kernel, tp=tp, nc=nc)
    small = jax.ShapeDtypeStruct((b, G_B, nc, LANES), F32)
    big = jax.ShapeDtypeStruct((b, G_B, t, LANES), BF16)
    small_spec = pl.BlockSpec((1, G_B, nc, LANES), lambda bb, s: (bb, 0, 0, 0))
    big_spec = pl.BlockSpec((1, G_B, tp, LANES), lambda bb, s: (bb, 0, s, 0))
    nt = t // tp
    return pl.pallas_call(
        kern, grid=(b, nt),
        in_specs=[pl.BlockSpec((tp, 4 * gw), lambda bb, s: (bb * nt + s, 0)),
                  pl.BlockSpec((tp, 2 * gw), lambda bb, s: (bb * nt + s, 0)),
                  pl.BlockSpec((2, CMP_BLOCK, gw), lambda bb, s: (0, 0, 0))],
        out_specs=[small_spec, small_spec, big_spec, big_spec, big_spec, big_spec],
        out_shape=[small, small, big, big, big, big],
        scratch_shapes=[pltpu.VMEM((gw // LANES * (tp // CMP_BLOCK), LANES), F32)],
        compiler_params=_params(("parallel", "arbitrary")),
        name="nsa_prep",
    )(nsa, win, wexp)


def _topk_bias(sc, sidx):
    rank = jnp.zeros(sc.shape, I32)
    for c in range(sc.shape[0]):
        other = sc[c:c + 1, :]
        beats = (other > sc) | ((other == sc) & (sidx > c))
        rank = rank + beats.astype(I32)
    return jnp.where(rank < TOP_N, 0.0, NEG_INF)


def _cmp_prompt_kernel(q_ref, kcd_ref, vcd_ref, gb_ref, ocmp_ref, qaug_ref, *, tq, nc):
    i = pl.program_id(1)
    ns = nc // 2
    lowq = _iota((tq, LANES), 1) < HALF
    rowc = _iota((nc, tq), 0)
    qpos = i * tq + _iota((nc, tq), 1)
    blk = jnp.where(rowc >= ns, 2 * (rowc - ns) + 1, 2 * rowc)
    c_mid = (blk * CMP_BLOCK).astype(F32) + (CMP_BLOCK - 1) * 0.5
    valid = blk * CMP_BLOCK + (CMP_BLOCK - 1) <= qpos
    adist = jnp.abs(qpos.astype(F32) - c_mid)
    sig = jax.nn.sigmoid(gb_ref[...])
    sidx = _iota((ns, tq), 0)
    qp = i * tq + _iota((ns, tq), 1)
    forced = (sidx == (qp >> _log2(SEL_BLOCK))) | (sidx == 0)
    valid_s = sidx * SEL_BLOCK <= qp
    for g in range(G_B):
        kc = kcd_ref[0, g].astype(BF16)
        vc = vcd_ref[0, g].astype(BF16)
        imp = jnp.zeros((nc, tq), F32)
        outs, qlows = [], []
        for r in range(R_B):
            h = g * R_B + r
            qv = q_ref[:, (h // 2) * LANES:(h // 2 + 1) * LANES]
            qm = jnp.where(lowq == (h % 2 == 0), qv, 0).astype(BF16)
            s = _nt_dot(kc, qm)
            s = s - (2.0 ** (-8.0 * (h + 1) / H_B) * LOG2E) * adist
            sm = jnp.where(valid, s, NEG_INF)
            e = jnp.exp2(sm - jnp.max(sm, axis=0, keepdims=True))
            p = jnp.where(valid, e / jnp.sum(e, axis=0, keepdims=True), 0.0)
            imp = imp + p
            o2 = jnp.dot(p.T.astype(BF16), vc, preferred_element_type=F32)
            outs.append(o2 * sig[:, h:h + 1])
            qlows.append(qv if h % 2 == 0 else pltpu.roll(qv, HALF, 1))
        for k in range(R_B // 2):
            c0 = (g * (R_B // 2) + k) * LANES
            ocmp_ref[:, c0:c0 + LANES] = jnp.where(lowq, outs[2 * k], outs[2 * k + 1])
        sc = jnp.where(forced, FORCE_SCORE, jnp.where(valid_s, imp[:ns] + imp[ns:], -1.0))
        bias_t = _topk_bias(sc, sidx)
        bias = jnp.concatenate([bias_t] * (LANES // ns), axis=0).T.astype(BF16)
        for r in range(R_B):
            qaug_ref[0, g, r] = jnp.where(lowq, qlows[r], bias).astype(BF16)


def _cmp_prompt(zq, kcd, vcd, gb, b, t, tq):
    nc = t // CMP_BLOCK
    nq = t // tq
    kern = functools.partial(_cmp_prompt_kernel, tq=tq, nc=nc)
    small_spec = pl.BlockSpec((1, G_B, nc, LANES), lambda bb, i: (bb, 0, 0, 0))
    return pl.pallas_call(
        kern, grid=(b, nq),
        in_specs=[pl.BlockSpec((tq, W_B), lambda bb, i: (bb * nq + i, W_A // W_B)),
                  small_spec, small_spec,
                  pl.BlockSpec((tq, LANES), lambda bb, i: (bb * nq + i, 0))],
        out_specs=[pl.BlockSpec((tq, W_B), lambda bb, i: (bb * nq + i, 0)),
                   pl.BlockSpec((1, G_B, R_B, tq, LANES), lambda bb, i: (bb, 0, 0, i, 0))],
        out_shape=[jax.ShapeDtypeStruct((b * t, W_B), F32),
                   jax.ShapeDtypeStruct((b, G_B, R_B, t, LANES), BF16)],
        compiler_params=_params(("parallel", "parallel")),
        name="cmp_prompt",
    )(zq, kcd, vcd, gb)


def _gate_column(sig, col):
    return jnp.sum(jnp.where(_iota(sig.shape, 1) == col, sig, 0.0), axis=-1, keepdims=True)


def _group_finish(o_ref, gb_ref, acc_ref, g, branch, tq):
    sig = jax.nn.sigmoid(gb_ref[...])
    low = _iota((tq, LANES), 1) < HALF
    outs = []
    for r in range(R_B):
        a = acc_ref[r * tq:(r + 1) * tq, :]
        o = a / pltpu.roll(a, HALF, 1)
        outs.append(o * _gate_column(sig, branch * H_B + g * R_B + r))
    for k in range(R_B // 2):
        o_ref[:, k * LANES:(k + 1) * LANES] = jnp.where(low, outs[2 * k], pltpu.roll(outs[2 * k + 1], HALF, 1))


def _sel_prompt_kernel(ii_ref, jj_ref, qa_ref, k_ref, v_ref, tile_ref, step_ref, gb_ref, o_ref,
                       m_ref, acc_ref, *, tq):
    g = pl.program_id(1)
    t = pl.program_id(2)
    i = ii_ref[t]
    j = jj_ref[t]

    @pl.when(j == 0)
    def _():
        _flash_init(m_ref, None, acc_ref)

    diag = (i == j).astype(I32)
    steps = (i - j).astype(F32)
    k = k_ref[0, 0]
    v = v_ref[0, 0]
    for r in range(R_B):
        rs = slice(r * tq, (r + 1) * tq)
        s = _nt_dot(qa_ref[0, 0, r], k) + tile_ref[0, diag, rs, :]
        _flash_rows(rs, s, step_ref[0, rs, :] * steps, v, m_ref, None, acc_ref)

    @pl.when(j == i)
    def _():
        _group_finish(o_ref, gb_ref, acc_ref, g, 1, tq)


def _sel_prompt(qaug, ksa, vsd, gb, b, t, tq):
    nq = t // tq
    ii, jj = _tri_tiles(nq)
    rows = R_B * tq
    tiles, step = _alibi_tiles(_slopes2(H_B).reshape(G_B, R_B), tq)
    kern = functools.partial(_sel_prompt_kernel, tq=tq)
    kv_spec = pl.BlockSpec((1, 1, tq, LANES), lambda bb, g, n, ii, jj: (bb, g, jj[n], 0))
    grid_spec = pltpu.PrefetchScalarGridSpec(
        num_scalar_prefetch=2,
        grid=(b, G_B, ii.shape[0]),
        in_specs=[pl.BlockSpec((1, 1, R_B, tq, LANES), lambda bb, g, n, ii, jj: (bb, g, 0, ii[n], 0)),
                  kv_spec, kv_spec,
                  pl.BlockSpec((1, 2, rows, tq), lambda bb, g, n, ii, jj: (g, 0, 0, 0)),
                  pl.BlockSpec((1, rows, LANES), lambda bb, g, n, ii, jj: (g, 0, 0)),
                  pl.BlockSpec((tq, LANES), lambda bb, g, n, ii, jj: (bb * nq + ii[n], 0))],
        out_specs=pl.BlockSpec((tq, R_B * DH_B), lambda bb, g, n, ii, jj: (bb * nq + ii[n], g)),
        scratch_shapes=[pltpu.VMEM((rows, LANES), F32),
                        pltpu.VMEM((rows, LANES), F32)])
    return pl.pallas_call(
        kern, grid_spec=grid_spec,
        out_shape=jax.ShapeDtypeStruct((b * t, W_B), F32),
        compiler_params=_params(("parallel", "parallel", "arbitrary")),
        name="sel_prompt",
    )(ii, jj, qaug, ksa, vsd, tiles, step, gb)


def _win_prompt_kernel(q_ref, k_ref, v_ref, tile_ref, gb_ref, o_ref, qs_ref, m_ref, acc_ref, *, tq, nj):
    g = pl.program_id(1)
    i = pl.program_id(2)
    j = pl.program_id(3)

    @pl.when(j == 0)
    def _():
        low = _iota((tq, LANES), 1) < HALF
        for r in range(R_B):
            qv = q_ref[:, (r // 2) * LANES:(r // 2 + 1) * LANES]
            qs_ref[r * tq:(r + 1) * tq, :] = jnp.where(low == (r % 2 == 0), qv, 0).astype(BF16)
        _flash_init(m_ref, None, acc_ref)

    @pl.when(i - (nj - 1) + j >= 0)
    def _():
        k = k_ref[0, 0]
        v = v_ref[0, 0]
        for r in range(R_B):
            rs = slice(r * tq, (r + 1) * tq)
            s = _nt_dot(qs_ref[rs, :], k) + tile_ref[0, j, rs, :]
            _flash_rows(rs, s, None, v, m_ref, None, acc_ref)

    @pl.when(j == nj - 1)
    def _():
        _group_finish(o_ref, gb_ref, acc_ref, g, 2, tq)


def _win_prompt(zq, kwd, vwd, gb, b, t, tq):
    nq = t // tq
    nj = -(-WINDOW // tq) + 1
    qoff = W_A // (R_B * DH_B)
    rows = R_B * tq
    tiles = _window_tiles(_slopes2(H_B).reshape(G_B, R_B), tq, nj)
    kern = functools.partial(_win_prompt_kernel, tq=tq, nj=nj)
    kv_spec = pl.BlockSpec((1, 1, tq, LANES),
                           lambda bb, g, i, j: (bb, g, jnp.maximum(i - (nj - 1) + j, 0), 0))
    return pl.pallas_call(
        kern, grid=(b, G_B, nq, nj),
        in_specs=[pl.BlockSpec((tq, R_B * DH_B), lambda bb, g, i, j: (bb * nq + i, qoff + g)),
                  kv_spec, kv_spec,
                  pl.BlockSpec((1, nj, rows, tq), lambda bb, g, i, j: (g, 0, 0, 0)),
                  pl.BlockSpec((tq, LANES), lambda bb, g, i, j: (bb * nq + i, 0))],
        out_specs=pl.BlockSpec((tq, R_B * DH_B), lambda bb, g, i, j: (bb * nq + i, g)),
        out_shape=jax.ShapeDtypeStruct((b * t, W_B), F32),
        scratch_shapes=[pltpu.VMEM((rows, LANES), BF16),
                        pltpu.VMEM((rows, LANES), F32),
                        pltpu.VMEM((rows, LANES), F32)],
        compiler_params=_params(("parallel", "parallel", "parallel", "arbitrary")),
        name="win_prompt",
    )(zq, kwd, vwd, tiles, gb)


def _split_w_in(w_in):
    sizes = (W_A, HKV_A * 2 * DH_A, HKV_A * 2 * DH_A, W_B, 6 * G_B * DH_B, 3 * H_B, 2 * w_in.shape[0])
    offs = np.concatenate([[0], np.cumsum(sizes)])
    seg = lambda a, b_: w_in[:, a:b_]
    gw = G_B * DH_B
    return dict(
        q=jnp.concatenate([seg(offs[0], offs[1]) * (DH_A ** -0.5 * LOG2E),
                           seg(offs[3], offs[4]) * (DH_B ** -0.5 * LOG2E)], axis=1).astype(BF16),
        kva=seg(offs[1], offs[3]).astype(BF16),
        nsa=seg(offs[4], offs[4] + 4 * gw).astype(BF16),
        win=seg(offs[4] + 4 * gw, offs[5]).astype(BF16),
        gb=jnp.pad(seg(offs[5], offs[6]), ((0, 0), (0, LANES - 3 * H_B))).astype(BF16),
        gm=seg(offs[6], offs[7]).astype(BF16))


def _in_projection(u, w, tm, q_dtype=BF16, kv_token_rows=False):
    zq, = _matmul(u, w["q"], (q_dtype,), tm, 512, "proj_q")
    if kv_token_rows:
        kva32, kva16 = _matmul_token_rows(u, w["kva"], 512, "proj_kva")
    else:
        kva32, kva16 = _matmul(u, w["kva"], (F32, BF16), tm, 512, "proj_kva")
    nsa32, = _matmul(u, w["nsa"], (F32,), tm, 512, "proj_nsa")
    win32, = _matmul(u, w["win"], (F32,), tm, 512, "proj_win")
    gb, = _matmul(u, w["gb"], (F32,), tm, LANES, "proj_gb")
    gm, = _matmul(u, w["gm"], (F32,), tm, 512, "proj_gm")
    return zq, kva32, kva16, nsa32, win32, gb, gm


def _prompt_mixer(u, w, lam_qk, gain, wexp, b, t, lam_init, tq=256):
    tq = min(tq, t)
    zq, kva32, kva16, nsa32, win32, gb, gm = _in_projection(u, w, 1024, kv_token_rows=True)
    nsa_t = _matmul_tokens_last(w["nsa"].T, u, b, 512, 1024, "proj_nsa_t")
    tbig = min(2 * tq, t)
    oa = _diff_prompt(zq, kva16, lam_qk, gain, b, t, tbig, lam_init)
    kcd, vcd, ksa, vsd, kwd, vwd = _nsa_prep(nsa32, win32, wexp, b, t, min(512, t))
    ocmp, qaug = _cmp_prompt(zq, kcd, vcd, gb, b, t, tq)
    osel = _sel_prompt(qaug, ksa, vsd, gb, b, t, tbig)
    owin = _win_prompt(zq, kwd, vwd, gb, b, t, tbig)
    return (oa, ocmp, osel, owin, gm), (kva32, nsa_t, win32)


def _pad_rows(x, rows):
    return jnp.concatenate([x, jnp.zeros((rows - x.shape[0], x.shape[1]), x.dtype)], axis=0)


def _diff_sample_kernel(pt_ref, q_ref, new_ref, lam_ref, gain_ref, *rest, tn, pps, past_len, lam_init):
    page_refs = rest[:pps]
    o_ref, qs_ref, m_ref, l_ref, acc_ref = rest[pps:]
    c = pl.program_id(1)
    rows = 2 * R_A * tn
    hw = HKV_A * 2 * DH_A
    stride = 2 * HKV_A
    row = _iota((rows, 1), 0)
    tq = row & (tn - 1)

    def slopes(hkv):
        head = hkv * R_A + (row >> _log2(2 * tn))
        return jnp.exp2(-8.0 * (head + 1).astype(F32) / H_A) * LOG2E

    def update(hkv, s, pv_fn):
        m, l, acc = _flash_step(s, None, m_ref[hkv], l_ref[hkv], acc_ref[hkv], pv_fn)
        m_ref[hkv] = m
        l_ref[hkv] = l
        acc_ref[hkv] = acc

    @pl.when(c == 0)
    def _():
        upper = _iota((tn, LANES), 1) >= HALF
        for hkv in range(HKV_A):
            parts = []
            for r in range(R_A):
                h = hkv * R_A + r
                qv = q_ref[:, h * LANES:(h + 1) * LANES]
                parts += [jnp.where(upper == bool(cc), qv, 0.0) for cc in range(2)]
            qs_ref[hkv] = jnp.concatenate(parts, axis=0).astype(BF16)
        _flash_init(m_ref, l_ref, acc_ref)

    kpos = c * (pps * PAGE_SIZE) + _iota((1, pps * PAGE_SIZE), 1)
    dist = ((past_len + tq) - kpos).astype(F32)
    scores = []
    for hkv in range(HKV_A):
        q = qs_ref[hkv]
        s = jnp.concatenate(
            [_nt_dot(q, pr[pl.ds(hkv, PAGE_SIZE, stride=stride), :].astype(BF16)) for pr in page_refs], axis=1)
        scores.append(s - slopes(hkv) * dist)
    for hkv in range(HKV_A):

        def pv(p, hkv=hkv):
            out = None
            for n, pr in enumerate(page_refs):
                vals = pr[pl.ds(HKV_A + hkv, PAGE_SIZE, stride=stride), :].astype(BF16)
                part = jnp.dot(p[:, n * PAGE_SIZE:(n + 1) * PAGE_SIZE], vals, preferred_element_type=F32)
                out = part if out is None else out + part
            return out

        update(hkv, scores[hkv], pv)

    @pl.when(c == pl.num_programs(1) - 1)
    def _():
        lam = _diff_lambda(lam_ref, lam_init)
        newp = _pad_rows(new_ref[...], LANES)
        dnew = tq - _iota((1, LANES), 1)
        for hkv in range(HKV_A):
            c0 = hkv * LANES
            s = _nt_dot(qs_ref[hkv], newp[:, c0:c0 + LANES].astype(BF16))
            s = jnp.where(dnew >= 0, s - slopes(hkv) * dnew.astype(F32), NEG_INF)
            vn = newp[:, hw + c0:hw + c0 + LANES].astype(BF16)
            update(hkv, s, lambda p, vn=vn: jnp.dot(p, vn, preferred_element_type=F32))
            o = acc_ref[hkv] / _row_total(l_ref[hkv])
            for r in range(R_A):
                b0 = 2 * r * tn
                y = _diff_finish(o[b0:b0 + tn], o[b0 + tn:b0 + 2 * tn], lam, gain_ref[...], lam_init)
                h = hkv * R_A + r
                o_ref[:, h * LANES:(h + 1) * LANES] = y


def _diff_sample(zq, kva_new, cache, pt, lam_qk, gain, bd, tn, n_pages, pps, lam_init):
    rows = 2 * R_A * tn
    page_rows = PAGE_SIZE * 2 * HKV_A
    kern = functools.partial(_diff_sample_kernel, tn=tn, pps=pps, past_len=n_pages * PAGE_SIZE,
                             lam_init=lam_init)

    def page_spec(p):
        return pl.BlockSpec((page_rows, LANES), lambda bb, c, pt: (pt[bb * n_pages + c * pps + p], 0))

    grid_spec = pltpu.PrefetchScalarGridSpec(
        num_scalar_prefetch=1,
        grid=(bd, n_pages // pps),
        in_specs=[pl.BlockSpec((tn, W_A), lambda bb, c, pt: (bb, 0)),
                  pl.BlockSpec((tn, W_A), lambda bb, c, pt: (bb, 0)),
                  pl.BlockSpec((4, DH_A), lambda bb, c, pt: (0, 0)),
                  pl.BlockSpec((1, 2 * DH_A), lambda bb, c, pt: (0, 0))]
                 + [page_spec(p) for p in range(pps)],
        out_specs=pl.BlockSpec((tn, W_A), lambda bb, c, pt: (bb, 0)),
        scratch_shapes=[pltpu.VMEM((HKV_A, rows, LANES), BF16),
                        pltpu.VMEM((HKV_A, rows, LANES), F32),
                        pltpu.VMEM((HKV_A, rows, LANES), F32),
                        pltpu.VMEM((HKV_A, rows, LANES), F32)])
    return pl.pallas_call(
        kern, grid_spec=grid_spec,
        out_shape=jax.ShapeDtypeStruct((bd * tn, W_A), F32),
        compiler_params=_params(("parallel", "arbitrary")),
        name="diff_sample",
    )(pt, zq, kva_new, lam_qk, gain.reshape(1, 2 * DH_A), *([cache] * pps))


def _sample_queries(qblk, tn):
    low = _iota((tn, LANES), 1) < HALF
    zero = jnp.zeros((tn, LANES), F32)
    cols = ([], [])
    for r in range(R_B):
        for g in range(G_B):
            h = g * R_B + r
            qv = qblk[:, (h // 2) * LANES:(h // 2 + 1) * LANES]
            if h % 2 != g % 2:
                qv = pltpu.roll(qv, HALF, 1)
            cols[g // 2].append(jnp.where(low == (g % 2 == 0), qv, 0.0))
            cols[1 - g // 2].append(zero)
    return jnp.concatenate([jnp.concatenate(c, axis=0) for c in cols], axis=1).astype(BF16)


def _sample_outputs(o_ref, o_all, sig, branch, tn):
    low = _iota((tn, LANES), 1) < HALF
    for g in range(G_B):
        for k in range(R_B // 2):
            parts = []
            for r in (2 * k, 2 * k + 1):
                r0 = (r * G_B + g) * tn
                src = o_all[r0:r0 + tn, (g // 2) * LANES:(g // 2 + 1) * LANES]
                if g % 2 != r % 2:
                    src = pltpu.roll(src, HALF, 1)
                col = branch * H_B + g * R_B + r
                parts.append(src * sig[:, col:col + 1])
            c0 = (g * (R_B // 2) + k) * LANES
            o_ref[:, c0:c0 + LANES] = jnp.where(low, parts[0], parts[1])


def _query_consts(idx, tn, past_len):
    r_q = idx >> _log2(G_B * tn)
    g_q = (idx >> _log2(tn)) & (G_B - 1)
    slope = jnp.exp2(-8.0 * (g_q * R_B + r_q + 1).astype(F32) / H_B) * LOG2E
    return g_q >> 1, slope, past_len + (idx & (tn - 1))


def _topk_member(sc):
    ridx = _iota(sc.shape, 0)
    removed = -3.0e38

    def body(_, carry):
        cur, mem = carry
        top = jnp.max(cur, axis=0, keepdims=True)
        first = jnp.min(jnp.where(cur == top, ridx, sc.shape[0]), axis=0, keepdims=True)
        pick = ridx == first
        return jnp.where(pick, removed, cur), jnp.where(pick, 1.0, mem)

    return lax.fori_loop(0, TOP_N, body, (sc, jnp.zeros(sc.shape, F32)), unroll=True)[1]


def _col_of(row_vec):
    return jnp.broadcast_to(row_vec, (LANES, LANES)).T


def _nsa_sample_kernel(pt_ref, q_ref, new_ref, gb_ref, wc_ref, e_ref, tile_ref, step_ref, *rest,
                       tn, pps, past_len):
    page_refs = rest[:pps]
    (ocmp_ref, osel_ref, qop_ref, kcvc_ref, imp_ref, sc_ref, selb_ref,
     m_ref, l_ref, acc_ref) = rest[pps:]
    ph = pl.program_id(1)
    c = pl.program_id(2)
    last = pl.num_programs(2) - 1
    gw = G_B * DH_B
    ncp = past_len // CMP_BLOCK
    nsp = past_len // SEL_BLOCK
    ck = pps * PAGE_SIZE
    lane_consts = lambda: _query_consts(_iota((1, LANES), 1), tn, past_len)
    row_consts = lambda: _query_consts(_iota((LANES, 1), 0), tn, past_len)

    @pl.when((ph == 0) & (c == 0))
    def _():
        qop_ref[...] = _sample_queries(q_ref[...], tn)

    @pl.when(ph == 0)
    def _():
        wc = wc_ref[...].astype(BF16)
        low8 = _iota((8, LANES), 1) < HALF
        comps = [_nt_dot(wc, jnp.concatenate([page_refs[2 * pp][...], page_refs[2 * pp + 1][...]],
                                             axis=1).astype(BF16)) for pp in range(pps // 2)]
        for pp, comp in enumerate(comps):
            row0 = pl.multiple_of((c * (pps // 2) + pp) * 8, 8)
            for slot in range(2):
                for p in range(2):
                    va = slot * G_B + 2 * p
                    c0 = slot * gw + p * LANES
                    piece = jnp.where(low8, comp[va * 8:(va + 1) * 8, c0:c0 + LANES],
                                      comp[(va + 1) * 8:(va + 2) * 8, c0:c0 + LANES])
                    kcvc_ref[pl.ds(row0, 8), c0:c0 + LANES] = piece

    @pl.when((ph == 0) & (c == last))
    def _():
        _, slope_l, qpos_l = lane_consts()
        kcvc = kcvc_ref[...]
        s = _nt_dot(kcvc[:, 0:gw].astype(BF16), qop_ref[...])
        blk = _iota((ncp, LANES), 0)
        c_mid = (blk * CMP_BLOCK).astype(F32) + (CMP_BLOCK - 1) * 0.5
        valid = blk * CMP_BLOCK + (CMP_BLOCK - 1) <= qpos_l
        s = s - slope_l * jnp.abs(qpos_l.astype(F32) - c_mid)
        sm = jnp.where(valid, s, NEG_INF)
        e = jnp.exp2(sm - jnp.max(sm, axis=0, keepdims=True))
        p_t = jnp.where(valid, e / jnp.sum(e, axis=0, keepdims=True), 0.0)
        p = p_t.T.astype(BF16)
        o_all = jnp.dot(p, kcvc[:, gw:2 * gw].astype(BF16), preferred_element_type=F32)
        _sample_outputs(ocmp_ref, o_all, jax.nn.sigmoid(gb_ref[...]), 0, tn)
        imp = p_t
        for k in range(1, R_B):
            imp = imp + pltpu.roll(p_t, k * G_B * tn, 1)
        imp_ref[...] = imp
        imp = imp_ref[pl.ds(0, nsp, stride=2), :] + imp_ref[pl.ds(1, nsp, stride=2), :]
        sidx = _iota((nsp, LANES), 0)
        forced = (sidx == (qpos_l >> _log2(SEL_BLOCK))) | (sidx == 0)
        sc_ref[0:nsp, :] = jnp.where(forced, FORCE_SCORE,
                                     jnp.where(sidx * SEL_BLOCK <= qpos_l, imp, -1.0))
        tail = sc_ref.shape[0] - nsp
        tidx = nsp + _iota((tail, LANES), 0)
        tforced = (tidx == (qpos_l >> _log2(SEL_BLOCK))) | (tidx == 0)
        tsc = jnp.where(tforced, FORCE_SCORE, jnp.where(tidx * SEL_BLOCK <= qpos_l, 0.0, -1.0))
        sc_ref[nsp:, :] = jnp.where(tidx == nsp, tsc, NEG_INF)
        selb_ref[...] = _topk_member(sc_ref[...])
        _flash_init(m_ref, l_ref, acc_ref)

    def sel_step(kt, vt, shift, s_bias_fn):
        s = s_bias_fn(jnp.dot(qop_ref[...], kt, preferred_element_type=F32))
        m, l, acc = _flash_step(s, shift, m_ref[...], l_ref[...], acc_ref[...], lambda p: _nt_dot(p, vt))
        m_ref[...] = m
        l_ref[...] = l
        acc_ref[...] = acc

    @pl.when(ph == 1)
    def _():
        kt = jnp.concatenate([pr[0:gw, :] for pr in page_refs], axis=1).astype(BF16)
        vt = jnp.concatenate([pr[gw:2 * gw, :] for pr in page_refs], axis=1).astype(BF16)
        nblk = ck // SEL_BLOCK
        member = _pad_rows(selb_ref[pl.ds(pl.multiple_of(c * nblk, 8), nblk), :], LANES).T
        mask = jnp.dot(member.astype(BF16), e_ref[...], preferred_element_type=F32)
        shift = step_ref[...] * (last - c).astype(F32)
        sel_step(kt, vt, shift, lambda s: jnp.where(mask > 0.5, s + tile_ref[...], NEG_INF))

    @pl.when((ph == 1) & (c == last))
    def _():
        xn = _pad_rows(new_ref[:, 2 * gw:4 * gw], LANES)
        kt = jnp.concatenate([xn[:, q * LANES:(q + 1) * LANES].T for q in range(2)], axis=0).astype(BF16)
        vt = jnp.concatenate([xn[:, gw + q * LANES:gw + (q + 1) * LANES].T for q in range(2)], axis=0).astype(BF16)
        _, slope_r, qpos_r = row_consts()
        member = _col_of(selb_ref[nsp:nsp + 1, :])
        col = _iota((1, LANES), 1)
        dist = qpos_r - (past_len + col)
        ok = (member > 0.5) & (dist >= 0) & (col < tn)
        sel_step(kt, vt, None, lambda s: jnp.where(ok, s - slope_r * dist.astype(F32), NEG_INF))
        _sample_outputs(osel_ref, acc_ref[...] / _row_total(l_ref[...]), jax.nn.sigmoid(gb_ref[...]), 1, tn)


def _compress_weights(w_cmp):
    nblk = 2 * PAGE_SIZE // CMP_BLOCK
    tok = jnp.arange(2 * PAGE_SIZE)
    w = jnp.transpose(w_cmp, (0, 2, 1))[:, :, tok % CMP_BLOCK]
    hit = (tok[None, :] // CMP_BLOCK) == jnp.arange(nblk)[:, None]
    return jnp.where(hit[None, None], w[:, :, None, :], 0.0).reshape(2 * G_B * nblk, 2 * PAGE_SIZE)


def _nsa_sample(zq, nsa_new, gb, w_cmp, cache, pt, bd, tn, n_pages, pps):
    past_len = n_pages * PAGE_SIZE
    gw = G_B * DH_B
    ncp = past_len // CMP_BLOCK
    nsp = past_len // SEL_BLOCK
    nsc = -(-(nsp + 1) // 8) * 8
    ck = pps * PAGE_SIZE
    assert pps % 2 == 0 and 2 * PAGE_SIZE // CMP_BLOCK == 8 and (ck // SEL_BLOCK) % 8 == 0
    kern = functools.partial(_nsa_sample_kernel, tn=tn, pps=pps, past_len=past_len)
    expand = (jnp.arange(ck)[None, :] // SEL_BLOCK == jnp.arange(LANES)[:, None]).astype(BF16)
    qrow = jnp.arange(LANES)
    slope_r = _slopes2(H_B)[((qrow // tn) % G_B) * R_B + qrow // (G_B * tn)]
    tile = -slope_r[:, None] * ((qrow % tn)[:, None] + ck - jnp.arange(ck)[None, :]).astype(F32)
    step = jnp.broadcast_to((slope_r * ck)[:, None], (LANES, LANES))

    def page_spec(p):
        return pl.BlockSpec((2 * gw, PAGE_SIZE),
                            lambda bb, ph, c, pt, p=p: (pt[bb * n_pages + c * pps + p] * 2 + ph, 0))

    row_spec = lambda width, col: pl.BlockSpec((tn, width), lambda bb, ph, c, pt: (bb, col))
    grid_spec = pltpu.PrefetchScalarGridSpec(
        num_scalar_prefetch=1,
        grid=(bd, 2, n_pages // pps),
        in_specs=[row_spec(W_B, W_A // W_B), row_spec(4 * gw, 0), row_spec(LANES, 0),
                  pl.BlockSpec((2 * G_B * 8, 2 * PAGE_SIZE), lambda bb, ph, c, pt: (0, 0)),
                  pl.BlockSpec((LANES, ck), lambda bb, ph, c, pt: (0, 0)),
                  pl.BlockSpec((LANES, ck), lambda bb, ph, c, pt: (0, 0)),
                  pl.BlockSpec((LANES, LANES), lambda bb, ph, c, pt: (0, 0))]
                 + [page_spec(p) for p in range(pps)],
        out_specs=[row_spec(W_B, 0), row_spec(W_B, 0)],
        scratch_shapes=[pltpu.VMEM((LANES, gw), BF16),
                        pltpu.VMEM((ncp, 2 * gw), F32),
                        pltpu.VMEM((ncp, LANES), F32),
                        pltpu.VMEM((nsc, LANES), F32),
                        pltpu.VMEM((nsc, LANES), F32),
                        pltpu.VMEM((LANES, LANES), F32),
                        pltpu.VMEM((LANES, LANES), F32),
                        pltpu.VMEM((LANES, gw), F32)])
    return pl.pallas_call(
        kern, grid_spec=grid_spec,
        out_shape=[jax.ShapeDtypeStruct((bd * tn, W_B), F32)] * 2,
        compiler_params=_params(("parallel", "arbitrary", "arbitrary")),
        name="nsa_sample",
    )(pt, zq, nsa_new, gb, _compress_weights(w_cmp), expand, tile, step, *([cache] * pps))


def _win_sample_kernel(q_ref, new_ref, st_ref, gb_ref, o_ref, ns_ref, *, tn, past_len):
    gw = G_B * DH_B
    wb = st_ref.shape[2]
    qs = _sample_queries(q_ref[...], tn)
    st = st_ref[0]
    new_t = _pad_rows(new_ref[...], LANES).T
    _, slope_r, qpos_r = _query_consts(_iota((LANES, 1), 0), tn, past_len)

    def scores(x):
        return jnp.dot(qs, x[0:gw, :].astype(BF16), preferred_element_type=F32)

    kpos_s = (past_len - wb) + _iota((1, wb), 1)
    d_s = qpos_r - kpos_s
    s_s = jnp.where((d_s >= 0) & (d_s < WINDOW) & (kpos_s >= 0), scores(st) - slope_r * d_s.astype(F32), NEG_INF)
    col_n = _iota((1, LANES), 1)
    d_n = qpos_r - (past_len + col_n)
    s_n = jnp.where((d_n >= 0) & (d_n < WINDOW) & (col_n < tn), scores(new_t) - slope_r * d_n.astype(F32), NEG_INF)
    s = jnp.concatenate([s_s, s_n], axis=1)
    e = jnp.exp2(s - jnp.max(s, axis=-1, keepdims=True))
    p = (e / jnp.sum(e, axis=-1, keepdims=True)).astype(BF16)
    o_all = (_nt_dot(p[:, :wb], st[gw:2 * gw, :].astype(BF16))
             + _nt_dot(p[:, wb:], new_t[gw:2 * gw, :].astype(BF16)))
    _sample_outputs(o_ref, o_all, jax.nn.sigmoid(gb_ref[...]), 2, tn)
    shifted = pltpu.roll(st, wb - tn, 1)
    tail = jnp.concatenate([jnp.zeros((2 * gw, wb - LANES), F32), pltpu.roll(new_t, LANES - tn, 1)], axis=1)
    ns_ref[0] = jnp.where(_iota((2 * gw, wb), 1) >= wb - tn, tail, shifted)


def _win_sample(zq, win_new, state_t, gb, bd, tn, past_len):
    gw = G_B * DH_B
    wb = state_t.shape[2]
    assert wb == WINDOW and wb == min(WINDOW, past_len) and wb % LANES == 0
    kern = functools.partial(_win_sample_kernel, tn=tn, past_len=past_len)
    return pl.pallas_call(
        kern, grid=(bd,),
        in_specs=[pl.BlockSpec((tn, W_B), lambda bb: (bb, W_A // W_B)),
                  pl.BlockSpec((tn, 2 * gw), lambda bb: (bb, 0)),
                  pl.BlockSpec((1, 2 * gw, wb), lambda bb: (bb, 0, 0)),
                  pl.BlockSpec((tn, LANES), lambda bb: (bb, 0))],
        out_specs=[pl.BlockSpec((tn, W_B), lambda bb: (bb, 0)),
                   pl.BlockSpec((1, 2 * gw, wb), lambda bb: (bb, 0, 0))],
        out_shape=[jax.ShapeDtypeStruct((bd * tn, W_B), F32),
                   jax.ShapeDtypeStruct((bd, 2 * gw, wb), F32)],
        compiler_params=_params(("parallel",)),
        name="win_sample",
    )(zq, win_new, state_t, gb)


def kernel(x_prompt, x_sample, cache_diff_kv, cache_nsa_kv, state_win_kv, page_table, w_in, w_proj_a,
           w_proj_b, w_out, lambda_qk, diff_gain, w_cmp, norm_attn, norm_mlp, w_up, w_down, norm_final):
    depth = w_in.shape[0]
    assert depth == 1
    b, t, d = x_prompt.shape
    bd, tn, _ = x_sample.shape
    l = 0
    lam_init = 0.8 - 0.6 * math.exp(-0.3 * l)
    w = _split_w_in(w_in[l])
    wpa, wpb, wo = w_proj_a[l].astype(BF16), w_proj_b[l].astype(BF16), w_out[l].astype(BF16)
    wup, wdn = w_up[l].astype(BF16), w_down[l].astype(BF16)
    wexp = jnp.repeat(w_cmp[l], DH_B, axis=-1)

    xp = x_prompt.reshape(b * t, d)
    up = _rmsnorm(xp, norm_attn[l], BF16, 512)
    (oa, ocmp, osel, owin, gm), (kva32, nsa_t, win32) = _prompt_mixer(
        up, w, lambda_qk[l], diff_gain[l], wexp, b, t, lam_init)
    hp, u2 = _post_attention(oa, ocmp, osel, owin, gm, xp, wpa, wpb, wo, norm_mlp[l], 256)
    y_prompt = _mlp_final(u2, hp, wup, wdn, norm_final, 512, 1024).reshape(b, t, d)
    keep = min(WINDOW, t)
    diff_kv_prompt = kva32.reshape(1, b, t, 2, HKV_A, 2 * DH_A)
    nsa_kv_prompt = jnp.transpose(nsa_t.reshape(b, 4, G_B, DH_B, t), (0, 4, 1, 2, 3))[None]
    win_kv_prompt = win32.reshape(b, t, 2, G_B, DH_B)[None, :, t - keep:]

    n_pages = page_table.shape[1]
    past_len = n_pages * PAGE_SIZE
    n_pool = cache_diff_kv.shape[1]
    pps = 16
    pps_nsa = 2 * pps
    assert R_B * G_B * tn == LANES and tn < CMP_BLOCK and n_pages % pps_nsa == 0
    xs = x_sample.reshape(bd * tn, d)
    us = _rmsnorm(xs, norm_attn[l], BF16, bd * tn)
    zq_s, kva_s, _, nsa_s, win_s, gb_s, gm_s = _in_projection(us, w, bd * tn, F32)
    pt = page_table.reshape(-1)
    cache_d = cache_diff_kv[l].reshape(n_pool * PAGE_SIZE * 2 * HKV_A, 2 * DH_A)
    cache_n = jnp.transpose(cache_nsa_kv[l], (0, 2, 3, 4, 1)).reshape(n_pool * 4 * G_B * DH_B, PAGE_SIZE)
    wb = state_win_kv.shape[2]
    state_t = jnp.transpose(state_win_kv[l], (0, 2, 3, 4, 1)).reshape(bd, 2 * G_B * DH_B, wb)
    oa_s = _diff_sample(zq_s, kva_s, cache_d, pt, lambda_qk[l], diff_gain[l], bd, tn, n_pages, pps, lam_init)
    ocmp_s, osel_s = _nsa_sample(zq_s, nsa_s, gb_s, w_cmp[l], cache_n, pt, bd, tn, n_pages, pps_nsa)
    owin_s, new_state_t = _win_sample(zq_s, win_s, state_t, gb_s, bd, tn, past_len)
    hs, u2s = _post_attention(oa_s, ocmp_s, osel_s, owin_s, gm_s, xs, wpa, wpb, wo, norm_mlp[l], 256)
    y_sample = _mlp_final(u2s, hs, wup, wdn, norm_final, 512, 512).reshape(bd, tn, d)
    diff_kv_sample = kva_s.reshape(1, bd, tn, 2, HKV_A, 2 * DH_A)
    nsa_kv_sample = nsa_s.reshape(1, bd, tn, 4, G_B, DH_B)
    win_kv_sample = jnp.transpose(new_state_t.reshape(bd, 2, G_B, DH_B, wb), (0, 4, 1, 2, 3))[None]
    return (y_prompt, y_sample, diff_kv_prompt, nsa_kv_prompt, win_kv_prompt,
            diff_kv_sample, nsa_kv_sample, win_kv_sample)
```

```python
import functools
import math

import numpy as np
import jax
import jax.numpy as jnp
from jax import lax
from jax.experimental import pallas as pl
from jax.experimental.pallas import tpu as pltpu

F32, BF16, I32 = jnp.float32, jnp.bfloat16, jnp.int32

PAGE_SIZE = 128
H_A, HKV_A, R_A, DH_A = 8, 4, 2, 64
W_A = H_A * 2 * DH_A
H_B, G_B, R_B, DH_B = 16, 4, 4, 64
W_B = H_B * DH_B
CMP_BLOCK, SEL_BLOCK, TOP_N, WINDOW = 32, 64, 16, 512
EPS = 1e-6
NEG_INF = -1e30
FORCE_SCORE = 1e4
LOG2E = math.log2(math.e)
LANES = 128
HALF = LANES // 2
VMEM_LIMIT = 56 * 1024 * 1024


def _iota(shape, dim):
    return lax.broadcasted_iota(I32, shape, dim)


def _log2(n):
    assert n & (n - 1) == 0, n
    return n.bit_length() - 1


def _nt_dot(a, b):
    return lax.dot_general(a, b, (((1,), (1,)), ((), ())), preferred_element_type=F32)


def _params(sem):
    return pltpu.CompilerParams(dimension_semantics=sem, vmem_limit_bytes=VMEM_LIMIT)


def _tri_tiles(n):
    ii = np.array([i for i in range(n) for _ in range(i + 1)], np.int32)
    jj = np.array([j for i in range(n) for j in range(i + 1)], np.int32)
    return jnp.asarray(ii), jnp.asarray(jj)


def _slopes2(n):
    return jnp.exp2(-8.0 * jnp.arange(1, n + 1, dtype=F32) / n) * LOG2E


def _rmsnorm_kernel(x_ref, g_ref, o_ref):
    x = x_ref[...]
    y = x * lax.rsqrt(jnp.mean(x * x, axis=-1, keepdims=True) + EPS)
    o_ref[...] = (y * g_ref[...]).astype(o_ref.dtype)


def _rmsnorm(x, g, out_dtype, tm):
    m, d = x.shape
    return pl.pallas_call(
        _rmsnorm_kernel,
        grid=(m // tm,),
        in_specs=[pl.BlockSpec((tm, d), lambda i: (i, 0)),
                  pl.BlockSpec((1, d), lambda i: (0, 0))],
        out_specs=pl.BlockSpec((tm, d), lambda i: (i, 0)),
        out_shape=jax.ShapeDtypeStruct((m, d), out_dtype),
        compiler_params=_params(("parallel",)),
        name="rmsnorm",
    )(x, g.reshape(1, d))


def _mm_kernel(a_ref, w_ref, *o_refs):
    acc = jnp.dot(a_ref[...], w_ref[...], preferred_element_type=F32)
    for o in o_refs:
        o[...] = acc.astype(o.dtype)


def _matmul(a, w, out_dtypes, tm, tn, name):
    m, k = a.shape
    n = w.shape[1]
    tm, tn = min(tm, m), min(tn, n)
    outs = pl.pallas_call(
        _mm_kernel,
        grid=(n // tn, m // tm),
        in_specs=[pl.BlockSpec((tm, k), lambda j, i: (i, 0)),
                  pl.BlockSpec((k, tn), lambda j, i: (0, j))],
        out_specs=[pl.BlockSpec((tm, tn), lambda j, i: (i, j)) for _ in out_dtypes],
        out_shape=[jax.ShapeDtypeStruct((m, n), dt) for dt in out_dtypes],
        compiler_params=_params(("parallel", "parallel")),
        name=name,
    )(a, w)
    return outs


def _mm_tokens_last_kernel(w_ref, a_ref, o_ref):
    o_ref[...] = _nt_dot(w_ref[...], a_ref[...])


def _matmul_tokens_last(w_t, a, b, tn, tm, name):
    n, k = w_t.shape
    t = a.shape[0] // b
    tn, tm = min(tn, n), min(tm, t)
    return pl.pallas_call(
        _mm_tokens_last_kernel,
        grid=(b, n // tn, t // tm),
        in_specs=[pl.BlockSpec((tn, k), lambda bb, j, i: (j, 0)),
                  pl.BlockSpec((tm, k), lambda bb, j, i: (bb * (t // tm) + i, 0))],
        out_specs=pl.BlockSpec((tn, tm), lambda bb, j, i: (bb * (n // tn) + j, i)),
        out_shape=jax.ShapeDtypeStruct((b * n, t), F32),
        compiler_params=_params(("parallel", "parallel", "parallel")),
        name=name,
    )(w_t, a)


def _mm_token_rows_kernel(a_ref, w_ref, o32_ref, o16_ref):
    acc = jnp.dot(a_ref[...], w_ref[...], preferred_element_type=F32)
    o16_ref[...] = acc.astype(o16_ref.dtype)
    nrow = acc.shape[1] // LANES
    for j in range(nrow):
        o32_ref[pl.ds(j, acc.shape[0], stride=nrow), :] = acc[:, j * LANES:(j + 1) * LANES]


def _matmul_token_rows(a, w, tm, name):
    m, k = a.shape
    n = w.shape[1]
    tm = min(tm, m)
    return pl.pallas_call(
        _mm_token_rows_kernel,
        grid=(m // tm,),
        in_specs=[pl.BlockSpec((tm, k), lambda i: (i, 0)),
                  pl.BlockSpec((k, n), lambda i: (0, 0), pipeline_mode=pl.Buffered(1))],
        out_specs=[pl.BlockSpec((tm * (n // LANES), LANES), lambda i: (i, 0)),
                   pl.BlockSpec((tm, n), lambda i: (i, 0))],
        out_shape=[jax.ShapeDtypeStruct((m * (n // LANES), LANES), F32),
                   jax.ShapeDtypeStruct((m, n), BF16)],
        compiler_params=_params(("parallel",)),
        name=name,
    )(a, w)


def _post_kernel(oa_ref, o1_ref, o2_ref, o3_ref, gm0_ref, gm1_ref, x_ref,
                 wpa_ref, wpb_ref, wo_ref, g_ref, h_ref, u_ref):
    oa = oa_ref[...].astype(BF16)
    ob = (o1_ref[...] + o2_ref[...] + o3_ref[...]).astype(BF16)
    pa = jnp.dot(oa, wpa_ref[...], preferred_element_type=F32)
    pb = jnp.dot(ob, wpb_ref[...], preferred_element_type=F32)
    mix = jax.nn.sigmoid(gm0_ref[...]) * pa + jax.nn.sigmoid(gm1_ref[...]) * pb
    y = jnp.dot(mix.astype(BF16), wo_ref[...], preferred_element_type=F32)
    h = x_ref[...] + y
    h_ref[...] = h
    u = h * lax.rsqrt(jnp.mean(h * h, axis=-1, keepdims=True) + EPS)
    u_ref[...] = (u * g_ref[...]).astype(u_ref.dtype)


def _post_attention(oa, o1, o2, o3, gm, x, wpa, wpb, wo, g_mlp, tm):
    m, d = x.shape
    tm = min(tm, m)
    row = lambda i: (i, 0)
    const = lambda i: (0, 0)
    once = pl.Buffered(1)
    return pl.pallas_call(
        _post_kernel,
        grid=(m // tm,),
        in_specs=[pl.BlockSpec((tm, W_A), row),
                  pl.BlockSpec((tm, W_B), row),
                  pl.BlockSpec((tm, W_B), row),
                  pl.BlockSpec((tm, W_B), row),
                  pl.BlockSpec((tm, d), lambda i: (i, 0)),
                  pl.BlockSpec((tm, d), lambda i: (i, 1)),
                  pl.BlockSpec((tm, d), row),
                  pl.BlockSpec((W_A, d), const, pipeline_mode=once),
                  pl.BlockSpec((W_B, d), const, pipeline_mode=once),
                  pl.BlockSpec((d, d), const, pipeline_mode=once),
                  pl.BlockSpec((1, d), const)],
        out_specs=[pl.BlockSpec((tm, d), row), pl.BlockSpec((tm, d), row)],
        out_shape=[jax.ShapeDtypeStruct((m, d), F32), jax.ShapeDtypeStruct((m, d), BF16)],
        compiler_params=_params(("parallel",)),
        name="post_attention",
    )(oa, o1, o2, o3, gm, gm, x, wpa, wpb, wo, g_mlp.reshape(1, d))


def _mlp_kernel(u_ref, h_ref, wup_ref, wdn_ref, g_ref, o_ref, acc_ref):
    f = pl.program_id(1)

    @pl.when(f == 0)
    def _():
        acc_ref[...] = jnp.zeros_like(acc_ref)

    a = jnp.dot(u_ref[...], wup_ref[...], preferred_element_type=F32)
    a = jnp.square(jnp.maximum(a, 0.0)).astype(BF16)
    acc_ref[...] += jnp.dot(a, wdn_ref[...], preferred_element_type=F32)

    @pl.when(f == pl.num_programs(1) - 1)
    def _():
        y = h_ref[...] + acc_ref[...]
        y = y * lax.rsqrt(jnp.mean(y * y, axis=-1, keepdims=True) + EPS)
        o_ref[...] = y * g_ref[...]


def _mlp_final(u, h, wup, wdn, g_final, tm, tf):
    m, d = h.shape
    dff = wup.shape[1]
    tm, tf = min(tm, m), min(tf, dff)
    return pl.pallas_call(
        _mlp_kernel,
        grid=(m // tm, dff // tf),
        in_specs=[pl.BlockSpec((tm, d), lambda i, f: (i, 0)),
                  pl.BlockSpec((tm, d), lambda i, f: (i, 0)),
                  pl.BlockSpec((d, tf), lambda i, f: (0, f)),
                  pl.BlockSpec((tf, d), lambda i, f: (f, 0)),
                  pl.BlockSpec((1, d), lambda i, f: (0, 0))],
        out_specs=pl.BlockSpec((tm, d), lambda i, f: (i, 0)),
        out_shape=jax.ShapeDtypeStruct((m, d), F32),
        scratch_shapes=[pltpu.VMEM((tm, d), F32)],
        compiler_params=_params(("parallel", "arbitrary")),
        name="mlp_final",
    )(u, h, wup, wdn, g_final.reshape(1, d))


def _flash_init(m_ref, l_ref, acc_ref):
    m_ref[...] = jnp.full(m_ref.shape, NEG_INF, F32)
    if l_ref is not None:
        l_ref[...] = jnp.zeros(l_ref.shape, F32)
    acc_ref[...] = jnp.zeros(acc_ref.shape, F32)


def _lane_blocks_sum(p):
    out = p[:, 0:LANES]
    for n in range(1, p.shape[1] // LANES):
        out = out + p[:, n * LANES:(n + 1) * LANES]
    return out


def _flash_step(s, shift, m_prev, l_prev, acc_prev, pv_fn):
    m_cur = jnp.max(s, axis=-1, keepdims=True)
    if shift is not None:
        m_cur = m_cur - shift
    m_new = jnp.maximum(m_prev, m_cur)
    alpha = jnp.exp2(m_prev - m_new)
    m_adj = m_new if shift is None else m_new + shift
    p = jnp.exp2(s - jnp.concatenate([m_adj] * (s.shape[1] // LANES), axis=1))
    l_new = None if l_prev is None else alpha * l_prev + _lane_blocks_sum(p)
    alpha_acc = jnp.concatenate([alpha] * (acc_prev.shape[1] // LANES), axis=1)
    acc_new = alpha_acc * acc_prev + pv_fn(p.astype(BF16))
    return m_new, l_new, acc_new


def _flash_rows(rs, s, shift, v, m_ref, l_ref, acc_ref):
    m, l, acc = _flash_step(s, shift, m_ref[rs, :], None if l_ref is None else l_ref[rs, :], acc_ref[rs, :],
                            lambda p: jnp.dot(p, v, preferred_element_type=F32))
    m_ref[rs, :] = m
    if l_ref is not None:
        l_ref[rs, :] = l
    acc_ref[rs, :] = acc


def _row_total(l):
    return jnp.sum(l, axis=-1, keepdims=True)


def _alibi_tiles(slopes, tq):
    ng, nr = slopes.shape
    d = (jnp.arange(tq)[:, None] - jnp.arange(tq)[None, :]).astype(F32)
    off = -slopes[:, :, None, None] * d
    tiles = jnp.stack([off, jnp.where(d >= 0, off, NEG_INF)], axis=1).reshape(ng, 2, nr * tq, tq)
    step = jnp.broadcast_to((slopes * tq)[:, :, None, None], (ng, nr, tq, LANES)).reshape(ng, nr * tq, LANES)
    return tiles, step


def _window_tiles(slopes, tq, nj):
    ng, nr = slopes.shape
    d = jnp.arange(tq)[:, None] - jnp.arange(tq)[None, :]
    dist = (nj - 1 - jnp.arange(nj))[:, None, None] * tq + d[None]
    dist = dist[None, :, None]
    bias = jnp.where((dist >= 0) & (dist < WINDOW),
                     -slopes[:, None, :, None, None] * dist.astype(F32), NEG_INF)
    return bias.reshape(ng, nj, nr * tq, tq)


def _diff_lambda(lam_ref, lam_init):
    lf = lam_ref[...]
    e1 = jnp.exp(jnp.sum(lf[0:1] * lf[1:2], axis=-1, keepdims=True))
    e2 = jnp.exp(jnp.sum(lf[2:3] * lf[3:4], axis=-1, keepdims=True))
    return e1 - e2 + lam_init


def _diff_finish(o1, o2, lam, gain, lam_init):
    o = o1 - lam * o2
    y = o * lax.rsqrt(jnp.mean(o * o, axis=-1, keepdims=True) + EPS)
    return (y * gain) * (1.0 - lam_init)


def _diff_prompt_kernel(ii_ref, jj_ref, q_ref, k_ref, v_ref, tile_ref, step_ref, lam_ref, gain_ref, o_ref,
                        qs_ref, m_ref, l_ref, acc_ref, *, tq, lam_init):
    t = pl.program_id(2)
    i = ii_ref[t]
    j = jj_ref[t]
    nblk = 2 * R_A

    @pl.when(j == 0)
    def _():
        upper = _iota((tq, LANES), 1) >= HALF
        for r in range(R_A):
            qv = q_ref[:, r * LANES:(r + 1) * LANES]
            for c in range(2):
                qs_ref[(2 * r + c) * tq:(2 * r + c + 1) * tq, :] = jnp.where(upper == bool(c), qv, 0).astype(BF16)
        _flash_init(m_ref, l_ref, acc_ref)

    diag = (i == j).astype(I32)
    steps = (i - j).astype(F32)
    k = k_ref[...]
    v = v_ref[...]
    for n in range(nblk):
        rs = slice(n * tq, (n + 1) * tq)
        s = _nt_dot(qs_ref[rs, :], k) + tile_ref[0, diag, rs, :]
        _flash_rows(rs, s, step_ref[0, rs, :] * steps, v, m_ref, l_ref, acc_ref)

    @pl.when(j == i)
    def _():
        lam = _diff_lambda(lam_ref, lam_init)
        for r in range(R_A):
            b0 = 2 * r * tq
            o1 = acc_ref[b0:b0 + tq, :] / _row_total(l_ref[b0:b0 + tq, :])
            o2 = acc_ref[b0 + tq:b0 + 2 * tq, :] / _row_total(l_ref[b0 + tq:b0 + 2 * tq, :])
            y = _diff_finish(o1, o2, lam, gain_ref[...], lam_init)
            o_ref[:, r * LANES:(r + 1) * LANES] = y.astype(o_ref.dtype)


def _diff_prompt(qa, kva, lam_qk, gain, b, t, tq, lam_init):
    nq = t // tq
    ii, jj = _tri_tiles(nq)
    rows = 2 * R_A * tq
    slopes = jnp.repeat(_slopes2(H_A).reshape(HKV_A, R_A), 2, axis=1)
    tiles, step = _alibi_tiles(slopes, tq)
    kern = functools.partial(_diff_prompt_kernel, tq=tq, lam_init=lam_init)
    grid_spec = pltpu.PrefetchScalarGridSpec(
        num_scalar_prefetch=2,
        grid=(b, HKV_A, ii.shape[0]),
        in_specs=[pl.BlockSpec((tq, R_A * LANES), lambda bb, h, n, ii, jj: (bb * nq + ii[n], h)),
                  pl.BlockSpec((tq, LANES), lambda bb, h, n, ii, jj: (bb * nq + jj[n], h)),
                  pl.BlockSpec((tq, LANES), lambda bb, h, n, ii, jj: (bb * nq + jj[n], HKV_A + h)),
                  pl.BlockSpec((1, 2, rows, tq), lambda bb, h, n, ii, jj: (h, 0, 0, 0)),
                  pl.BlockSpec((1, rows, LANES), lambda bb, h, n, ii, jj: (h, 0, 0)),
                  pl.BlockSpec((4, DH_A), lambda bb, h, n, ii, jj: (0, 0)),
                  pl.BlockSpec((1, 2 * DH_A), lambda bb, h, n, ii, jj: (0, 0))],
        out_specs=pl.BlockSpec((tq, R_A * LANES), lambda bb, h, n, ii, jj: (bb * nq + ii[n], h)),
        scratch_shapes=[pltpu.VMEM((rows, LANES), BF16),
                        pltpu.VMEM((rows, LANES), F32),
                        pltpu.VMEM((rows, LANES), F32),
                        pltpu.VMEM((rows, LANES), F32)])
    return pl.pallas_call(
        kern, grid_spec=grid_spec,
        out_shape=jax.ShapeDtypeStruct((b * t, W_A), BF16),
        compiler_params=_params(("parallel", "parallel", "arbitrary")),
        name="diff_prompt",
    )(ii, jj, qa, kva, kva, tiles, step, lam_qk, gain.reshape(1, 2 * DH_A))


def _dup_half(v, g):
    rolled = pltpu.roll(v, HALF, 1)
    low = _iota(v.shape, 1) < HALF
    return jnp.where(low, v, rolled) if g % 2 == 0 else jnp.where(low, rolled, v)


def _value_ones(v, g):
    low = _iota(v.shape, 1) < HALF
    return jnp.where(low, v if g % 2 == 0 else pltpu.roll(v, HALF, 1), 1.0)


def _prep_kernel(nsa_ref, win_ref, wexp_ref, kcd_ref, vcd_ref, ksa_ref, vsd_ref, kwd_ref, vwd_ref,
                 tmp_ref, *, tp, nc):
    step = pl.program_id(1)
    gw = G_B * DH_B
    x = nsa_ref[...]
    w = win_ref[...]
    lane = _iota((tp, LANES), 1)
    low = lane < HALF
    blk = (step * tp + _iota((tp, LANES), 0)) >> _log2(SEL_BLOCK)
    onehot = jnp.where(lane - HALF == blk, 1.0, 0.0)
    for g in range(G_B):
        c0 = (g // 2) * LANES
        vk = x[:, 2 * gw + c0:2 * gw + c0 + LANES]
        klow = vk if g % 2 == 0 else pltpu.roll(vk, HALF, 1)
        ksa_ref[0, g] = jnp.where(low, klow, onehot).astype(BF16)
        vsd_ref[0, g] = _value_ones(x[:, 3 * gw + c0:3 * gw + c0 + LANES], g).astype(BF16)
        kwd_ref[0, g] = _dup_half(w[:, c0:c0 + LANES], g).astype(BF16)
        vwd_ref[0, g] = _value_ones(w[:, gw + c0:gw + c0 + LANES], g).astype(BF16)
    nb = tp // CMP_BLOCK
    hb = nb // 2
    for slot, dst in ((0, kcd_ref), (1, vcd_ref)):
        xc = x[:, slot * gw:(slot + 1) * gw]
        comp = jnp.sum(xc.reshape(nb, CMP_BLOCK, gw) * wexp_ref[slot][None], axis=1)
        for c in range(gw // LANES):
            tmp_ref[c * nb:(c + 1) * nb, :] = comp[:, c * LANES:(c + 1) * LANES]
        for par in range(2):
            start = pl.multiple_of(par * (nc // 2) + step * hb, 8)
            for g in range(G_B):
                rr = tmp_ref[pl.ds((g // 2) * nb + par, hb, stride=2), :]
                dst[0, g, pl.ds(start, hb), :] = _dup_half(rr, g)


def _nsa_prep(nsa, win, wexp, b, t, tp):
    nc = t // CMP_BLOCK
    gw = G_B * DH_B
    kern = functools.partial(_prep_kernel, tp=tp, nc=nc)
    small = jax.ShapeDtypeStruct((b, G_B, nc, LANES), F32)
    big = jax.ShapeDtypeStruct((b, G_B, t, LANES), BF16)
    small_spec = pl.BlockSpec((1, G_B, nc, LANES), lambda bb, s: (bb, 0, 0, 0))
    big_spec = pl.BlockSpec((1, G_B, tp, LANES), lambda bb, s: (bb, 0, s, 0))
    nt = t // tp
    return pl.pallas_call(
        kern, grid=(b, nt),
        in_specs=[pl.BlockSpec((tp, 4 * gw), lambda bb, s: (bb * nt + s, 0)),
                  pl.BlockSpec((tp, 2 * gw), lambda bb, s: (bb * nt + s, 0)),
                  pl.BlockSpec((2, CMP_BLOCK, gw), lambda bb, s: (0, 0, 0))],
        out_specs=[small_spec, small_spec, big_spec, big_spec, big_spec, big_spec],
        out_shape=[small, small, big, big, big, big],
        scratch_shapes=[pltpu.VMEM((gw // LANES * (tp // CMP_BLOCK), LANES), F32)],
        compiler_params=_params(("parallel", "arbitrary")),
        name="nsa_prep",
    )(nsa, win, wexp)


def _topk_bias(sc, sidx):
    rank = jnp.zeros(sc.shape, I32)
    for c in range(sc.shape[0]):
        other = sc[c:c + 1, :]
        beats = (other > sc) | ((other == sc) & (sidx > c))
        rank = rank + beats.astype(I32)
    return jnp.where(rank < TOP_N, 0.0, NEG_INF)


def _cmp_prompt_kernel(q_ref, kcd_ref, vcd_ref, gb_ref, ocmp_ref, qaug_ref, *, tq, nc):
    i = pl.program_id(1)
    ns = nc // 2
    lowq = _iota((tq, LANES), 1) < HALF
    rowc = _iota((nc, tq), 0)
    qpos = i * tq + _iota((nc, tq), 1)
    blk = jnp.where(rowc >= ns, 2 * (rowc - ns) + 1, 2 * rowc)
    c_mid = (blk * CMP_BLOCK).astype(F32) + (CMP_BLOCK - 1) * 0.5
    valid = blk * CMP_BLOCK + (CMP_BLOCK - 1) <= qpos
    adist = jnp.abs(qpos.astype(F32) - c_mid)
    sig = jax.nn.sigmoid(gb_ref[...])
    sidx = _iota((ns, tq), 0)
    qp = i * tq + _iota((ns, tq), 1)
    forced = (sidx == (qp >> _log2(SEL_BLOCK))) | (sidx == 0)
    valid_s = sidx * SEL_BLOCK <= qp
    for g in range(G_B):
        kc = kcd_ref[0, g].astype(BF16)
        vc = vcd_ref[0, g].astype(BF16)
        imp = jnp.zeros((nc, tq), F32)
        outs, qlows = [], []
        for r in range(R_B):
            h = g * R_B + r
            qv = q_ref[:, (h // 2) * LANES:(h // 2 + 1) * LANES]
            qm = jnp.where(lowq == (h % 2 == 0), qv, 0).astype(BF16)
            s = _nt_dot(kc, qm)
            s = s - (2.0 ** (-8.0 * (h + 1) / H_B) * LOG2E) * adist
            sm = jnp.where(valid, s, NEG_INF)
            e = jnp.exp2(sm - jnp.max(sm, axis=0, keepdims=True))
            p = jnp.where(valid, e / jnp.sum(e, axis=0, keepdims=True), 0.0)
            imp = imp + p
            o2 = jnp.dot(p.T.astype(BF16), vc, preferred_element_type=F32)
            outs.append(o2 * sig[:, h:h + 1])
            qlows.append(qv if h % 2 == 0 else pltpu.roll(qv, HALF, 1))
        for k in range(R_B // 2):
            c0 = (g * (R_B // 2) + k) * LANES
            ocmp_ref[:, c0:c0 + LANES] = jnp.where(lowq, outs[2 * k], outs[2 * k + 1])
        sc = jnp.where(forced, FORCE_SCORE, jnp.where(valid_s, imp[:ns] + imp[ns:], -1.0))
        bias_t = _topk_bias(sc, sidx)
        bias = jnp.concatenate([bias_t] * (LANES // ns), axis=0).T.astype(BF16)
        for r in range(R_B):
            qaug_ref[0, g, r] = jnp.where(lowq, qlows[r], bias).astype(BF16)


def _cmp_prompt(zq, kcd, vcd, gb, b, t, tq):
    nc = t // CMP_BLOCK
    nq = t // tq
    kern = functools.partial(_cmp_prompt_kernel, tq=tq, nc=nc)
    small_spec = pl.BlockSpec((1, G_B, nc, LANES), lambda bb, i: (bb, 0, 0, 0))
    return pl.pallas_call(
        kern, grid=(b, nq),
        in_specs=[pl.BlockSpec((tq, W_B), lambda bb, i: (bb * nq + i, W_A // W_B)),
                  small_spec, small_spec,
                  pl.BlockSpec((tq, LANES), lambda bb, i: (bb * nq + i, 0))],
        out_specs=[pl.BlockSpec((tq, W_B), lambda bb, i: (bb * nq + i, 0)),
                   pl.BlockSpec((1, G_B, R_B, tq, LANES), lambda bb, i: (bb, 0, 0, i, 0))],
        out_shape=[jax.ShapeDtypeStruct((b * t, W_B), F32),
                   jax.ShapeDtypeStruct((b, G_B, R_B, t, LANES), BF16)],
        compiler_params=_params(("parallel", "parallel")),
        name="cmp_prompt",
    )(zq, kcd, vcd, gb)


def _gate_column(sig, col):
    return jnp.sum(jnp.where(_iota(sig.shape, 1) == col, sig, 0.0), axis=-1, keepdims=True)


def _group_finish(o_ref, gb_ref, acc_ref, g, branch, tq):
    sig = jax.nn.sigmoid(gb_ref[...])
    low = _iota((tq, LANES), 1) < HALF
    outs = []
    for r in range(R_B):
        a = acc_ref[r * tq:(r + 1) * tq, :]
        o = a / pltpu.roll(a, HALF, 1)
        outs.append(o * _gate_column(sig, branch * H_B + g * R_B + r))
    for k in range(R_B // 2):
        o_ref[:, k * LANES:(k + 1) * LANES] = jnp.where(low, outs[2 * k], pltpu.roll(outs[2 * k + 1], HALF, 1))


def _sel_prompt_kernel(ii_ref, jj_ref, qa_ref, k_ref, v_ref, tile_ref, step_ref, gb_ref, o_ref,
                       m_ref, acc_ref, *, tq):
    g = pl.program_id(1)
    t = pl.program_id(2)
    i = ii_ref[t]
    j = jj_ref[t]

    @pl.when(j == 0)
    def _():
        _flash_init(m_ref, None, acc_ref)

    diag = (i == j).astype(I32)
    steps = (i - j).astype(F32)
    k = k_ref[0, 0]
    v = v_ref[0, 0]
    for r in range(R_B):
        rs = slice(r * tq, (r + 1) * tq)
        s = _nt_dot(qa_ref[0, 0, r], k) + tile_ref[0, diag, rs, :]
        _flash_rows(rs, s, step_ref[0, rs, :] * steps, v, m_ref, None, acc_ref)

    @pl.when(j == i)
    def _():
        _group_finish(o_ref, gb_ref, acc_ref, g, 1, tq)


def _sel_prompt(qaug, ksa, vsd, gb, b, t, tq):
    nq = t // tq
    ii, jj = _tri_tiles(nq)
    rows = R_B * tq
    tiles, step = _alibi_tiles(_slopes2(H_B).reshape(G_B, R_B), tq)
    kern = functools.partial(_sel_prompt_kernel, tq=tq)
    kv_spec = pl.BlockSpec((1, 1, tq, LANES), lambda bb, g, n, ii, jj: (bb, g, jj[n], 0))
    grid_spec = pltpu.PrefetchScalarGridSpec(
        num_scalar_prefetch=2,
        grid=(b, G_B, ii.shape[0]),
        in_specs=[pl.BlockSpec((1, 1, R_B, tq, LANES), lambda bb, g, n, ii, jj: (bb, g, 0, ii[n], 0)),
                  kv_spec, kv_spec,
                  pl.BlockSpec((1, 2, rows, tq), lambda bb, g, n, ii, jj: (g, 0, 0, 0)),
                  pl.BlockSpec((1, rows, LANES), lambda bb, g, n, ii, jj: (g, 0, 0)),
                  pl.BlockSpec((tq, LANES), lambda bb, g, n, ii, jj: (bb * nq + ii[n], 0))],
        out_specs=pl.BlockSpec((tq, R_B * DH_B), lambda bb, g, n, ii, jj: (bb * nq + ii[n], g)),
        scratch_shapes=[pltpu.VMEM((rows, LANES), F32),
                        pltpu.VMEM((rows, LANES), F32)])
    return pl.pallas_call(
        kern, grid_spec=grid_spec,
        out_shape=jax.ShapeDtypeStruct((b * t, W_B), F32),
        compiler_params=_params(("parallel", "parallel", "arbitrary")),
        name="sel_prompt",
    )(ii, jj, qaug, ksa, vsd, tiles, step, gb)


def _win_prompt_kernel(q_ref, k_ref, v_ref, tile_ref, gb_ref, o_ref, qs_ref, m_ref, acc_ref, *, tq, nj):
    g = pl.program_id(1)
    i = pl.program_id(2)
    j = pl.program_id(3)

    @pl.when(j == 0)
    def _():
        low = _iota((tq, LANES), 1) < HALF
        for r in range(R_B):
            qv = q_ref[:, (r // 2) * LANES:(r // 2 + 1) * LANES]
            qs_ref[r * tq:(r + 1) * tq, :] = jnp.where(low == (r % 2 == 0), qv, 0).astype(BF16)
        _flash_init(m_ref, None, acc_ref)

    @pl.when(i - (nj - 1) + j >= 0)
    def _():
        k = k_ref[0, 0]
        v = v_ref[0, 0]
        for r in range(R_B):
            rs = slice(r * tq, (r + 1) * tq)
            s = _nt_dot(qs_ref[rs, :], k) + tile_ref[0, j, rs, :]
            _flash_rows(rs, s, None, v, m_ref, None, acc_ref)

    @pl.when(j == nj - 1)
    def _():
        _group_finish(o_ref, gb_ref, acc_ref, g, 2, tq)


def _win_prompt(zq, kwd, vwd, gb, b, t, tq):
    nq = t // tq
    nj = -(-WINDOW // tq) + 1
    qoff = W_A // (R_B * DH_B)
    rows = R_B * tq
    tiles = _window_tiles(_slopes2(H_B).reshape(G_B, R_B), tq, nj)
    kern = functools.partial(_win_prompt_kernel, tq=tq, nj=nj)
    kv_spec = pl.BlockSpec((1, 1, tq, LANES),
                           lambda bb, g, i, j: (bb, g, jnp.maximum(i - (nj - 1) + j, 0), 0))
    return pl.pallas_call(
        kern, grid=(b, G_B, nq, nj),
        in_specs=[pl.BlockSpec((tq, R_B * DH_B), lambda bb, g, i, j: (bb * nq + i, qoff + g)),
                  kv_spec, kv_spec,
                  pl.BlockSpec((1, nj, rows, tq), lambda bb, g, i, j: (g, 0, 0, 0)),
                  pl.BlockSpec((tq, LANES), lambda bb, g, i, j: (bb * nq + i, 0))],
        out_specs=pl.BlockSpec((tq, R_B * DH_B), lambda bb, g, i, j: (bb * nq + i, g)),
        out_shape=jax.ShapeDtypeStruct((b * t, W_B), F32),
        scratch_shapes=[pltpu.VMEM((rows, LANES), BF16),
                        pltpu.VMEM((rows, LANES), F32),
                        pltpu.VMEM((rows, LANES), F32)],
        compiler_params=_params(("parallel", "parallel", "parallel", "arbitrary")),
        name="win_prompt",
    )(zq, kwd, vwd, tiles, gb)


def _split_w_in(w_in):
    sizes = (W_A, HKV_A * 2 * DH_A, HKV_A * 2 * DH_A, W_B, 6 * G_B * DH_B, 3 * H_B, 2 * w_in.shape[0])
    offs = np.concatenate([[0], np.cumsum(sizes)])
    seg = lambda a, b_: w_in[:, a:b_]
    gw = G_B * DH_B
    return dict(
        q=jnp.concatenate([seg(offs[0], offs[1]) * (DH_A ** -0.5 * LOG2E),
                           seg(offs[3], offs[4]) * (DH_B ** -0.5 * LOG2E)], axis=1).astype(BF16),
        kva=seg(offs[1], offs[3]).astype(BF16),
        nsa=seg(offs[4], offs[4] + 4 * gw).astype(BF16),
        win=seg(offs[4] + 4 * gw, offs[5]).astype(BF16),
        gb=jnp.pad(seg(offs[5], offs[6]), ((0, 0), (0, LANES - 3 * H_B))).astype(BF16),
        gm=seg(offs[6], offs[7]).astype(BF16))


def _in_projection(u, w, tm, q_dtype=BF16, kv_token_rows=False):
    zq, = _matmul(u, w["q"], (q_dtype,), tm, 512, "proj_q")
    if kv_token_rows:
        kva32, kva16 = _matmul_token_rows(u, w["kva"], 512, "proj_kva")
    else:
        kva32, kva16 = _matmul(u, w["kva"], (F32, BF16), tm, 512, "proj_kva")
    nsa32, = _matmul(u, w["nsa"], (F32,), tm, 512, "proj_nsa")
    win32, = _matmul(u, w["win"], (F32,), tm, 512, "proj_win")
    gb, = _matmul(u, w["gb"], (F32,), tm, LANES, "proj_gb")
    gm, = _matmul(u, w["gm"], (F32,), tm, 512, "proj_gm")
    return zq, kva32, kva16, nsa32, win32, gb, gm


def _prompt_mixer(u, w, lam_qk, gain, wexp, b, t, lam_init, tq=256):
    tq = min(tq, t)
    zq, kva32, kva16, nsa32, win32, gb, gm = _in_projection(u, w, 1024, kv_token_rows=True)
    nsa_t = _matmul_tokens_last(w["nsa"].T, u, b, 512, 1024, "proj_nsa_t")
    tbig = min(2 * tq, t)
    oa = _diff_prompt(zq, kva16, lam_qk, gain, b, t, tbig, lam_init)
    kcd, vcd, ksa, vsd, kwd, vwd = _nsa_prep(nsa32, win32, wexp, b, t, min(512, t))
    ocmp, qaug = _cmp_prompt(zq, kcd, vcd, gb, b, t, tq)
    osel = _sel_prompt(qaug, ksa, vsd, gb, b, t, tbig)
    owin = _win_prompt(zq, kwd, vwd, gb, b, t, tbig)
    return (oa, ocmp, osel, owin, gm), (kva32, nsa_t, win32)


def _pad_rows(x, rows):
    if x.shape[0] == rows:
        return x
    return jnp.concatenate([x, jnp.zeros((rows - x.shape[0], x.shape[1]), x.dtype)], axis=0)


def _diff_sample_kernel(pt_ref, q_ref, new_ref, lam_ref, gain_ref, *rest, tn, pps, past_len, lam_init):
    page_refs = rest[:pps]
    o_ref, qs_ref, m_ref, l_ref, acc_ref = rest[pps:]
    c = pl.program_id(1)
    rows = 2 * R_A * tn
    hw = HKV_A * 2 * DH_A
    stride = 2 * HKV_A
    row = _iota((rows, 1), 0)
    tq = row & (tn - 1)

    def slopes(hkv):
        head = hkv * R_A + (row >> _log2(2 * tn))
        return jnp.exp2(-8.0 * (head + 1).astype(F32) / H_A) * LOG2E

    def update(hkv, s, pv_fn):
        m, l, acc = _flash_step(s, None, m_ref[hkv], l_ref[hkv], acc_ref[hkv], pv_fn)
        m_ref[hkv] = m
        l_ref[hkv] = l
        acc_ref[hkv] = acc

    @pl.when(c == 0)
    def _():
        upper = _iota((tn, LANES), 1) >= HALF
        for hkv in range(HKV_A):
            parts = []
            for r in range(R_A):
                h = hkv * R_A + r
                qv = q_ref[:, h * LANES:(h + 1) * LANES]
                parts += [jnp.where(upper == bool(cc), qv, 0.0) for cc in range(2)]
            qs_ref[hkv] = jnp.concatenate(parts, axis=0).astype(BF16)
        _flash_init(m_ref, l_ref, acc_ref)

    kpos = c * (pps * PAGE_SIZE) + _iota((1, pps * PAGE_SIZE), 1)
    dist = ((past_len + tq) - kpos).astype(F32)
    scores = []
    for hkv in range(HKV_A):
        q = qs_ref[hkv]
        s = jnp.concatenate(
            [_nt_dot(q, pr[pl.ds(hkv, PAGE_SIZE, stride=stride), :].astype(BF16)) for pr in page_refs], axis=1)
        scores.append(s - slopes(hkv) * dist)
    for hkv in range(HKV_A):

        def pv(p, hkv=hkv):
            out = None
            for n, pr in enumerate(page_refs):
                vals = pr[pl.ds(HKV_A + hkv, PAGE_SIZE, stride=stride), :].astype(BF16)
                part = jnp.dot(p[:, n * PAGE_SIZE:(n + 1) * PAGE_SIZE], vals, preferred_element_type=F32)
                out = part if out is None else out + part
            return out

        update(hkv, scores[hkv], pv)

    @pl.when(c == pl.num_programs(1) - 1)
    def _():
        lam = _diff_lambda(lam_ref, lam_init)
        newp = _pad_rows(new_ref[...], LANES)
        dnew = tq - _iota((1, LANES), 1)
        for hkv in range(HKV_A):
            c0 = hkv * LANES
            s = _nt_dot(qs_ref[hkv], newp[:, c0:c0 + LANES].astype(BF16))
            s = jnp.where(dnew >= 0, s - slopes(hkv) * dnew.astype(F32), NEG_INF)
            vn = newp[:, hw + c0:hw + c0 + LANES].astype(BF16)
            update(hkv, s, lambda p, vn=vn: jnp.dot(p, vn, preferred_element_type=F32))
            o = acc_ref[hkv] / _row_total(l_ref[hkv])
            for r in range(R_A):
                b0 = 2 * r * tn
                y = _diff_finish(o[b0:b0 + tn], o[b0 + tn:b0 + 2 * tn], lam, gain_ref[...], lam_init)
                h = hkv * R_A + r
                o_ref[:, h * LANES:(h + 1) * LANES] = y


def _diff_sample(zq, kva_new, cache, pt, lam_qk, gain, bd, tn, n_pages, pps, lam_init):
    rows = 2 * R_A * tn
    page_rows = PAGE_SIZE * 2 * HKV_A
    kern = functools.partial(_diff_sample_kernel, tn=tn, pps=pps, past_len=n_pages * PAGE_SIZE,
                             lam_init=lam_init)

    def page_spec(p):
        return pl.BlockSpec((page_rows, LANES), lambda bb, c, pt: (pt[bb * n_pages + c * pps + p], 0))

    grid_spec = pltpu.PrefetchScalarGridSpec(
        num_scalar_prefetch=1,
        grid=(bd, n_pages // pps),
        in_specs=[pl.BlockSpec((tn, W_A), lambda bb, c, pt: (bb, 0)),
                  pl.BlockSpec((tn, W_A), lambda bb, c, pt: (bb, 0)),
                  pl.BlockSpec((4, DH_A), lambda bb, c, pt: (0, 0)),
                  pl.BlockSpec((1, 2 * DH_A), lambda bb, c, pt: (0, 0))]
                 + [page_spec(p) for p in range(pps)],
        out_specs=pl.BlockSpec((tn, W_A), lambda bb, c, pt: (bb, 0)),
        scratch_shapes=[pltpu.VMEM((HKV_A, rows, LANES), BF16),
                        pltpu.VMEM((HKV_A, rows, LANES), F32),
                        pltpu.VMEM((HKV_A, rows, LANES), F32),
                        pltpu.VMEM((HKV_A, rows, LANES), F32)])
    return pl.pallas_call(
        kern, grid_spec=grid_spec,
        out_shape=jax.ShapeDtypeStruct((bd * tn, W_A), F32),
        compiler_params=_params(("parallel", "arbitrary")),
        name="diff_sample",
    )(pt, zq, kva_new, lam_qk, gain.reshape(1, 2 * DH_A), *([cache] * pps))


def _sample_queries(qblk, tn):
    low = _iota((tn, LANES), 1) < HALF
    zero = jnp.zeros((tn, LANES), F32)
    cols = ([], [])
    for r in range(R_B):
        for g in range(G_B):
            h = g * R_B + r
            qv = qblk[:, (h // 2) * LANES:(h // 2 + 1) * LANES]
            if h % 2 != g % 2:
                qv = pltpu.roll(qv, HALF, 1)
            cols[g // 2].append(jnp.where(low == (g % 2 == 0), qv, 0.0))
            cols[1 - g // 2].append(zero)
    return jnp.concatenate([jnp.concatenate(c, axis=0) for c in cols], axis=1).astype(BF16)


def _sample_outputs(o_ref, o_all, sig, branch, tn):
    low = _iota((tn, LANES), 1) < HALF
    for g in range(G_B):
        for k in range(R_B // 2):
            parts = []
            for r in (2 * k, 2 * k + 1):
                r0 = (r * G_B + g) * tn
                src = o_all[r0:r0 + tn, (g // 2) * LANES:(g // 2 + 1) * LANES]
                if g % 2 != r % 2:
                    src = pltpu.roll(src, HALF, 1)
                col = branch * H_B + g * R_B + r
                parts.append(src * sig[:, col:col + 1])
            c0 = (g * (R_B // 2) + k) * LANES
            o_ref[:, c0:c0 + LANES] = jnp.where(low, parts[0], parts[1])


def _query_consts(idx, tn, past_len):
    r_q = idx >> _log2(G_B * tn)
    g_q = (idx >> _log2(tn)) & (G_B - 1)
    slope = jnp.exp2(-8.0 * (g_q * R_B + r_q + 1).astype(F32) / H_B) * LOG2E
    return g_q >> 1, slope, past_len + (idx & (tn - 1))


def _topk_member(sc):
    ridx = _iota(sc.shape, 0)
    removed = -3.0e38

    def body(_, carry):
        cur, mem = carry
        top = jnp.max(cur, axis=0, keepdims=True)
        first = jnp.min(jnp.where(cur == top, ridx, sc.shape[0]), axis=0, keepdims=True)
        pick = ridx == first
        return jnp.where(pick, removed, cur), jnp.where(pick, 1.0, mem)

    return lax.fori_loop(0, TOP_N, body, (sc, jnp.zeros(sc.shape, F32)), unroll=True)[1]


def _col_of(row_vec):
    return jnp.broadcast_to(row_vec, (LANES, LANES)).T


def _nsa_sample_kernel(pt_ref, q_ref, new_ref, gb_ref, wc_ref, e_ref, tile_ref, step_ref, *rest,
                       tn, pps, past_len):
    page_refs = rest[:pps]
    (ocmp_ref, osel_ref, qop_ref, kcvc_ref, imp_ref, sc_ref, selb_ref,
     m_ref, l_ref, acc_ref) = rest[pps:]
    ph = pl.program_id(1)
    c = pl.program_id(2)
    last = pl.num_programs(2) - 1
    gw = G_B * DH_B
    ncp = past_len // CMP_BLOCK
    nsp = past_len // SEL_BLOCK
    ck = pps * PAGE_SIZE
    lane_consts = lambda: _query_consts(_iota((1, LANES), 1), tn, past_len)
    row_consts = lambda: _query_consts(_iota((LANES, 1), 0), tn, past_len)

    @pl.when((ph == 0) & (c == 0))
    def _():
        qop_ref[...] = _sample_queries(q_ref[...], tn)

    @pl.when(ph == 0)
    def _():
        wc = wc_ref[...].astype(BF16)
        low8 = _iota((8, LANES), 1) < HALF
        comps = [_nt_dot(wc, jnp.concatenate([page_refs[2 * pp][...], page_refs[2 * pp + 1][...]],
                                             axis=1).astype(BF16)) for pp in range(pps // 2)]
        for pp, comp in enumerate(comps):
            row0 = pl.multiple_of((c * (pps // 2) + pp) * 8, 8)
            for slot in range(2):
                for p in range(2):
                    va = slot * G_B + 2 * p
                    c0 = slot * gw + p * LANES
                    piece = jnp.where(low8, comp[va * 8:(va + 1) * 8, c0:c0 + LANES],
                                      comp[(va + 1) * 8:(va + 2) * 8, c0:c0 + LANES])
                    kcvc_ref[pl.ds(row0, 8), c0:c0 + LANES] = piece

    @pl.when((ph == 0) & (c == last))
    def _():
        _, slope_l, qpos_l = lane_consts()
        kcvc = kcvc_ref[...]
        s = _nt_dot(kcvc[:, 0:gw].astype(BF16), qop_ref[...])
        blk = _iota((ncp, LANES), 0)
        c_mid = (blk * CMP_BLOCK).astype(F32) + (CMP_BLOCK - 1) * 0.5
        valid = blk * CMP_BLOCK + (CMP_BLOCK - 1) <= qpos_l
        s = s - slope_l * jnp.abs(qpos_l.astype(F32) - c_mid)
        sm = jnp.where(valid, s, NEG_INF)
        e = jnp.exp2(sm - jnp.max(sm, axis=0, keepdims=True))
        p_t = jnp.where(valid, e / jnp.sum(e, axis=0, keepdims=True), 0.0)
        p = p_t.T.astype(BF16)
        o_all = jnp.dot(p, kcvc[:, gw:2 * gw].astype(BF16), preferred_element_type=F32)
        _sample_outputs(ocmp_ref, o_all, jax.nn.sigmoid(gb_ref[...]), 0, tn)
        imp = p_t
        for k in range(1, R_B):
            imp = imp + pltpu.roll(p_t, k * G_B * tn, 1)
        imp_ref[...] = imp
        imp = imp_ref[pl.ds(0, nsp, stride=2), :] + imp_ref[pl.ds(1, nsp, stride=2), :]
        sidx = _iota((nsp, LANES), 0)
        forced = (sidx == (qpos_l >> _log2(SEL_BLOCK))) | (sidx == 0)
        sc_ref[0:nsp, :] = jnp.where(forced, FORCE_SCORE,
                                     jnp.where(sidx * SEL_BLOCK <= qpos_l, imp, -1.0))
        tail = sc_ref.shape[0] - nsp
        tidx = nsp + _iota((tail, LANES), 0)
        tforced = (tidx == (qpos_l >> _log2(SEL_BLOCK))) | (tidx == 0)
        tsc = jnp.where(tforced, FORCE_SCORE, jnp.where(tidx * SEL_BLOCK <= qpos_l, 0.0, -1.0))
        sc_ref[nsp:, :] = jnp.where(tidx == nsp, tsc, NEG_INF)
        selb_ref[...] = _topk_member(sc_ref[...])
        _flash_init(m_ref, l_ref, acc_ref)

    def sel_step(kt, vt, shift, s_bias_fn):
        s = s_bias_fn(jnp.dot(qop_ref[...], kt, preferred_element_type=F32))
        m, l, acc = _flash_step(s, shift, m_ref[...], l_ref[...], acc_ref[...], lambda p: _nt_dot(p, vt))
        m_ref[...] = m
        l_ref[...] = l
        acc_ref[...] = acc

    @pl.when(ph == 1)
    def _():
        kt = jnp.concatenate([pr[0:gw, :] for pr in page_refs], axis=1).astype(BF16)
        vt = jnp.concatenate([pr[gw:2 * gw, :] for pr in page_refs], axis=1).astype(BF16)
        nblk = ck // SEL_BLOCK
        member = _pad_rows(selb_ref[pl.ds(pl.multiple_of(c * nblk, 8), nblk), :], LANES).T
        mask = jnp.dot(member.astype(BF16), e_ref[...], preferred_element_type=F32)
        shift = step_ref[...] * (last - c).astype(F32)
        sel_step(kt, vt, shift, lambda s: jnp.where(mask > 0.5, s + tile_ref[...], NEG_INF))

    @pl.when((ph == 1) & (c == last))
    def _():
        xn = _pad_rows(new_ref[:, 2 * gw:4 * gw], LANES)
        kt = jnp.concatenate([xn[:, q * LANES:(q + 1) * LANES].T for q in range(2)], axis=0).astype(BF16)
        vt = jnp.concatenate([xn[:, gw + q * LANES:gw + (q + 1) * LANES].T for q in range(2)], axis=0).astype(BF16)
        _, slope_r, qpos_r = row_consts()
        member = _col_of(selb_ref[nsp:nsp + 1, :])
        col = _iota((1, LANES), 1)
        dist = qpos_r - (past_len + col)
        ok = (member > 0.5) & (dist >= 0) & (col < tn)
        sel_step(kt, vt, None, lambda s: jnp.where(ok, s - slope_r * dist.astype(F32), NEG_INF))
        _sample_outputs(osel_ref, acc_ref[...] / _row_total(l_ref[...]), jax.nn.sigmoid(gb_ref[...]), 1, tn)


def _compress_weights(w_cmp):
    nblk = 2 * PAGE_SIZE // CMP_BLOCK
    tok = jnp.arange(2 * PAGE_SIZE)
    w = jnp.transpose(w_cmp, (0, 2, 1))[:, :, tok % CMP_BLOCK]
    hit = (tok[None, :] // CMP_BLOCK) == jnp.arange(nblk)[:, None]
    return jnp.where(hit[None, None], w[:, :, None, :], 0.0).reshape(2 * G_B * nblk, 2 * PAGE_SIZE)


def _nsa_sample(zq, nsa_new, gb, w_cmp, cache, pt, bd, tn, n_pages, pps):
    past_len = n_pages * PAGE_SIZE
    gw = G_B * DH_B
    ncp = past_len // CMP_BLOCK
    nsp = past_len // SEL_BLOCK
    nsc = -(-(nsp + 1) // 8) * 8
    ck = pps * PAGE_SIZE
    assert pps % 2 == 0 and 2 * PAGE_SIZE // CMP_BLOCK == 8 and (ck // SEL_BLOCK) % 8 == 0
    kern = functools.partial(_nsa_sample_kernel, tn=tn, pps=pps, past_len=past_len)
    expand = (jnp.arange(ck)[None, :] // SEL_BLOCK == jnp.arange(LANES)[:, None]).astype(BF16)
    qrow = jnp.arange(LANES)
    slope_r = _slopes2(H_B)[((qrow // tn) % G_B) * R_B + qrow // (G_B * tn)]
    tile = -slope_r[:, None] * ((qrow % tn)[:, None] + ck - jnp.arange(ck)[None, :]).astype(F32)
    step = jnp.broadcast_to((slope_r * ck)[:, None], (LANES, LANES))

    def page_spec(p):
        return pl.BlockSpec((2 * gw, PAGE_SIZE),
                            lambda bb, ph, c, pt, p=p: (pt[bb * n_pages + c * pps + p] * 2 + ph, 0))

    row_spec = lambda width, col: pl.BlockSpec((tn, width), lambda bb, ph, c, pt: (bb, col))
    grid_spec = pltpu.PrefetchScalarGridSpec(
        num_scalar_prefetch=1,
        grid=(bd, 2, n_pages // pps),
        in_specs=[row_spec(W_B, W_A // W_B), row_spec(4 * gw, 0), row_spec(LANES, 0),
                  pl.BlockSpec((2 * G_B * 8, 2 * PAGE_SIZE), lambda bb, ph, c, pt: (0, 0)),
                  pl.BlockSpec((LANES, ck), lambda bb, ph, c, pt: (0, 0)),
                  pl.BlockSpec((LANES, ck), lambda bb, ph, c, pt: (0, 0)),
                  pl.BlockSpec((LANES, LANES), lambda bb, ph, c, pt: (0, 0))]
                 + [page_spec(p) for p in range(pps)],
        out_specs=[row_spec(W_B, 0), row_spec(W_B, 0)],
        scratch_shapes=[pltpu.VMEM((LANES, gw), BF16),
                        pltpu.VMEM((ncp, 2 * gw), F32),
                        pltpu.VMEM((ncp, LANES), F32),
                        pltpu.VMEM((nsc, LANES), F32),
                        pltpu.VMEM((nsc, LANES), F32),
                        pltpu.VMEM((LANES, LANES), F32),
                        pltpu.VMEM((LANES, LANES), F32),
                        pltpu.VMEM((LANES, gw), F32)])
    return pl.pallas_call(
        kern, grid_spec=grid_spec,
        out_shape=[jax.ShapeDtypeStruct((bd * tn, W_B), F32)] * 2,
        compiler_params=_params(("parallel", "arbitrary", "arbitrary")),
        name="nsa_sample",
    )(pt, zq, nsa_new, gb, _compress_weights(w_cmp), expand, tile, step, *([cache] * pps))


def _win_sample_kernel(q_ref, new_ref, st_ref, gb_ref, o_ref, ns_ref, *, tn, past_len):
    gw = G_B * DH_B
    wb = st_ref.shape[2]
    qs = _sample_queries(q_ref[...], tn)
    st = st_ref[0]
    new_t = _pad_rows(new_ref[...], LANES).T
    _, slope_r, qpos_r = _query_consts(_iota((LANES, 1), 0), tn, past_len)

    def scores(x):
        return jnp.dot(qs, x[0:gw, :].astype(BF16), preferred_element_type=F32)

    kpos_s = (past_len - wb) + _iota((1, wb), 1)
    d_s = qpos_r - kpos_s
    s_s = jnp.where((d_s >= 0) & (d_s < WINDOW) & (kpos_s >= 0), scores(st) - slope_r * d_s.astype(F32), NEG_INF)
    col_n = _iota((1, LANES), 1)
    d_n = qpos_r - (past_len + col_n)
    s_n = jnp.where((d_n >= 0) & (d_n < WINDOW) & (col_n < tn), scores(new_t) - slope_r * d_n.astype(F32), NEG_INF)
    s = jnp.concatenate([s_s, s_n], axis=1)
    e = jnp.exp2(s - jnp.max(s, axis=-1, keepdims=True))
    p = (e / jnp.sum(e, axis=-1, keepdims=True)).astype(BF16)
    o_all = (_nt_dot(p[:, :wb], st[gw:2 * gw, :].astype(BF16))
             + _nt_dot(p[:, wb:], new_t[gw:2 * gw, :].astype(BF16)))
    _sample_outputs(o_ref, o_all, jax.nn.sigmoid(gb_ref[...]), 2, tn)
    shifted = pltpu.roll(st, wb - tn, 1)
    tail = jnp.concatenate([jnp.zeros((2 * gw, wb - LANES), F32), pltpu.roll(new_t, LANES - tn, 1)], axis=1)
    ns_ref[0] = jnp.where(_iota((2 * gw, wb), 1) >= wb - tn, tail, shifted)


def _win_sample(zq, win_new, state_t, gb, bd, tn, past_len):
    gw = G_B * DH_B
    wb = state_t.shape[2]
    assert wb == WINDOW and wb == min(WINDOW, past_len) and wb % LANES == 0
    kern = functools.partial(_win_sample_kernel, tn=tn, past_len=past_len)
    return pl.pallas_call(
        kern, grid=(bd,),
        in_specs=[pl.BlockSpec((tn, W_B), lambda bb: (bb, W_A // W_B)),
                  pl.BlockSpec((tn, 2 * gw), lambda bb: (bb, 0)),
                  pl.BlockSpec((1, 2 * gw, wb), lambda bb: (bb, 0, 0)),
                  pl.BlockSpec((tn, LANES), lambda bb: (bb, 0))],
        out_specs=[pl.BlockSpec((tn, W_B), lambda bb: (bb, 0)),
                   pl.BlockSpec((1, 2 * gw, wb), lambda bb: (bb, 0, 0))],
        out_shape=[jax.ShapeDtypeStruct((bd * tn, W_B), F32),
                   jax.ShapeDtypeStruct((bd, 2 * gw, wb), F32)],
        compiler_params=_params(("parallel",)),
        name="win_sample",
    )(zq, win_new, state_t, gb)


def kernel(x_prompt, x_sample, cache_diff_kv, cache_nsa_kv, state_win_kv, page_table, w_in, w_proj_a,
           w_proj_b, w_out, lambda_qk, diff_gain, w_cmp, norm_attn, norm_mlp, w_up, w_down, norm_final):
    depth = w_in.shape[0]
    assert depth == 1
    b, t, d = x_prompt.shape
    bd, tn, _ = x_sample.shape
    l = 0
    lam_init = 0.8 - 0.6 * math.exp(-0.3 * l)
    w = _split_w_in(w_in[l])
    wpa, wpb, wo = w_proj_a[l].astype(BF16), w_proj_b[l].astype(BF16), w_out[l].astype(BF16)
    wup, wdn = w_up[l].astype(BF16), w_down[l].astype(BF16)
    wexp = jnp.repeat(w_cmp[l], DH_B, axis=-1)

    xp = x_prompt.reshape(b * t, d)
    up = _rmsnorm(xp, norm_attn[l], BF16, 512)
    (oa, ocmp, osel, owin, gm), (kva32, nsa_t, win32) = _prompt_mixer(
        up, w, lambda_qk[l], diff_gain[l], wexp, b, t, lam_init)
    hp, u2 = _post_attention(oa, ocmp, osel, owin, gm, xp, wpa, wpb, wo, norm_mlp[l], 256)
    y_prompt = _mlp_final(u2, hp, wup, wdn, norm_final, 512, 1024).reshape(b, t, d)
    keep = min(WINDOW, t)
    diff_kv_prompt = kva32.reshape(1, b, t, 2, HKV_A, 2 * DH_A)
    nsa_kv_prompt = jnp.transpose(nsa_t.reshape(b, 4, G_B, DH_B, t), (0, 4, 1, 2, 3))[None]
    win_kv_prompt = win32.reshape(b, t, 2, G_B, DH_B)[None, :, t - keep:]

    n_pages = page_table.shape[1]
    past_len = n_pages * PAGE_SIZE
    n_pool = cache_diff_kv.shape[1]
    pps = 32
    pps_nsa = 2 * pps
    assert R_B * G_B * tn == LANES and tn < CMP_BLOCK and n_pages % pps_nsa == 0
    xs = x_sample.reshape(bd * tn, d)
    us = _rmsnorm(xs, norm_attn[l], BF16, bd * tn)
    zq_s, kva_s, _, nsa_s, win_s, gb_s, gm_s = _in_projection(us, w, bd * tn, F32)
    pt = page_table.reshape(-1)
    cache_d = cache_diff_kv[l].reshape(n_pool * PAGE_SIZE * 2 * HKV_A, 2 * DH_A)
    cache_n = jnp.transpose(cache_nsa_kv[l], (0, 2, 3, 4, 1)).reshape(n_pool * 4 * G_B * DH_B, PAGE_SIZE)
    wb = state_win_kv.shape[2]
    state_t = jnp.transpose(state_win_kv[l], (0, 2, 3, 4, 1)).reshape(bd, 2 * G_B * DH_B, wb)
    oa_s = _diff_sample(zq_s, kva_s, cache_d, pt, lambda_qk[l], diff_gain[l], bd, tn, n_pages, pps, lam_init)
    ocmp_s, osel_s = _nsa_sample(zq_s, nsa_s, gb_s, w_cmp[l], cache_n, pt, bd, tn, n_pages, pps_nsa)
    owin_s, new_state_t = _win_sample(zq_s, win_s, state_t, gb_s, bd, tn, past_len)
    hs, u2s = _post_attention(oa_s, ocmp_s, osel_s, owin_s, gm_s, xs, wpa, wpb, wo, norm_mlp[l], 256)
    y_sample = _mlp_final(u2s, hs, wup, wdn, norm_final, 512, 512).reshape(bd, tn, d)
    diff_kv_sample = kva_s.reshape(1, bd, tn, 2, HKV_A, 2 * DH_A)
    nsa_kv_sample = nsa_s.reshape(1, bd, tn, 4, G_B, DH_B)
    win_kv_sample = jnp.transpose(new_state_t.reshape(bd, 2, G_B, DH_B, wb), (0, 4, 1, 2, 3))[None]
    return (y_prompt, y_sample, diff_kv_prompt, nsa_kv_prompt, win_kv_prompt,
            diff_kv_sample, nsa_kv_sample, win_kv_sample)
```

```python
import functools
import math

import numpy as np
import jax
import jax.numpy as jnp
from jax import lax
from jax.experimental import pallas as pl
from jax.experimental.pallas import tpu as pltpu

F32, BF16, I32 = jnp.float32, jnp.bfloat16, jnp.int32

PAGE_SIZE = 128
H_A, HKV_A, R_A, DH_A = 8, 4, 2, 64
W_A = H_A * 2 * DH_A
H_B, G_B, R_B, DH_B = 16, 4, 4, 64
W_B = H_B * DH_B
CMP_BLOCK, SEL_BLOCK, TOP_N, WINDOW = 32, 64, 16, 512
EPS = 1e-6
NEG_INF = -1e30
FORCE_SCORE = 1e4
LOG2E = math.log2(math.e)
LANES = 128
HALF = LANES // 2
VMEM_LIMIT = 56 * 1024 * 1024


def _iota(shape, dim):
    return lax.broadcasted_iota(I32, shape, dim)


def _log2(n):
    assert n & (n - 1) == 0, n
    return n.bit_length() - 1


def _nt_dot(a, b):
    return lax.dot_general(a, b, (((1,), (1,)), ((), ())), preferred_element_type=F32)


def _params(sem):
    return pltpu.CompilerParams(dimension_semantics=sem, vmem_limit_bytes=VMEM_LIMIT)


def _tri_tiles(n):
    ii = np.array([i for i in range(n) for _ in range(i + 1)], np.int32)
    jj = np.array([j for i in range(n) for j in range(i + 1)], np.int32)
    return jnp.asarray(ii), jnp.asarray(jj)


def _slopes2(n):
    return jnp.exp2(-8.0 * jnp.arange(1, n + 1, dtype=F32) / n) * LOG2E


def _rmsnorm_kernel(x_ref, g_ref, o_ref):
    x = x_ref[...]
    y = x * lax.rsqrt(jnp.mean(x * x, axis=-1, keepdims=True) + EPS)
    o_ref[...] = (y * g_ref[...]).astype(o_ref.dtype)


def _rmsnorm(x, g, out_dtype, tm):
    m, d = x.shape
    return pl.pallas_call(
        _rmsnorm_kernel,
        grid=(m // tm,),
        in_specs=[pl.BlockSpec((tm, d), lambda i: (i, 0)),
                  pl.BlockSpec((1, d), lambda i: (0, 0))],
        out_specs=pl.BlockSpec((tm, d), lambda i: (i, 0)),
        out_shape=jax.ShapeDtypeStruct((m, d), out_dtype),
        compiler_params=_params(("parallel",)),
        name="rmsnorm",
    )(x, g.reshape(1, d))


def _mm_kernel(a_ref, w_ref, *o_refs):
    acc = jnp.dot(a_ref[...], w_ref[...], preferred_element_type=F32)
    for o in o_refs:
        o[...] = acc.astype(o.dtype)


def _matmul(a, w, out_dtypes, tm, tn, name):
    m, k = a.shape
    n = w.shape[1]
    tm, tn = min(tm, m), min(tn, n)
    outs = pl.pallas_call(
        _mm_kernel,
        grid=(n // tn, m // tm),
        in_specs=[pl.BlockSpec((tm, k), lambda j, i: (i, 0)),
                  pl.BlockSpec((k, tn), lambda j, i: (0, j))],
        out_specs=[pl.BlockSpec((tm, tn), lambda j, i: (i, j)) for _ in out_dtypes],
        out_shape=[jax.ShapeDtypeStruct((m, n), dt) for dt in out_dtypes],
        compiler_params=_params(("parallel", "parallel")),
        name=name,
    )(a, w)
    return outs


def _mm_tokens_last_kernel(w_ref, a_ref, o_ref):
    o_ref[...] = _nt_dot(w_ref[...], a_ref[...])


def _matmul_tokens_last(w_t, a, b, tn, tm, name):
    n, k = w_t.shape
    t = a.shape[0] // b
    tn, tm = min(tn, n), min(tm, t)
    return pl.pallas_call(
        _mm_tokens_last_kernel,
        grid=(b, n // tn, t // tm),
        in_specs=[pl.BlockSpec((tn, k), lambda bb, j, i: (j, 0)),
                  pl.BlockSpec((tm, k), lambda bb, j, i: (bb * (t // tm) + i, 0))],
        out_specs=pl.BlockSpec((tn, tm), lambda bb, j, i: (bb * (n // tn) + j, i)),
        out_shape=jax.ShapeDtypeStruct((b * n, t), F32),
        compiler_params=_params(("parallel", "parallel", "parallel")),
        name=name,
    )(w_t, a)


def _mm_token_rows_kernel(a_ref, w_ref, o32_ref, o16_ref):
    acc = jnp.dot(a_ref[...], w_ref[...], preferred_element_type=F32)
    o16_ref[...] = acc.astype(o16_ref.dtype)
    nrow = acc.shape[1] // LANES
    for j in range(nrow):
        o32_ref[pl.ds(j, acc.shape[0], stride=nrow), :] = acc[:, j * LANES:(j + 1) * LANES]


def _matmul_token_rows(a, w, tm, name):
    m, k = a.shape
    n = w.shape[1]
    tm = min(tm, m)
    return pl.pallas_call(
        _mm_token_rows_kernel,
        grid=(m // tm,),
        in_specs=[pl.BlockSpec((tm, k), lambda i: (i, 0)),
                  pl.BlockSpec((k, n), lambda i: (0, 0), pipeline_mode=pl.Buffered(1))],
        out_specs=[pl.BlockSpec((tm * (n // LANES), LANES), lambda i: (i, 0)),
                   pl.BlockSpec((tm, n), lambda i: (i, 0))],
        out_shape=[jax.ShapeDtypeStruct((m * (n // LANES), LANES), F32),
                   jax.ShapeDtypeStruct((m, n), BF16)],
        compiler_params=_params(("parallel",)),
        name=name,
    )(a, w)


def _post_kernel(oa_ref, o1_ref, o2_ref, o3_ref, gm0_ref, gm1_ref, x_ref,
                 wpa_ref, wpb_ref, wo_ref, g_ref, h_ref, u_ref):
    oa = oa_ref[...].astype(BF16)
    ob = (o1_ref[...] + o2_ref[...] + o3_ref[...]).astype(BF16)
    pa = jnp.dot(oa, wpa_ref[...], preferred_element_type=F32)
    pb = jnp.dot(ob, wpb_ref[...], preferred_element_type=F32)
    mix = jax.nn.sigmoid(gm0_ref[...]) * pa + jax.nn.sigmoid(gm1_ref[...]) * pb
    y = jnp.dot(mix.astype(BF16), wo_ref[...], preferred_element_type=F32)
    h = x_ref[...] + y
    h_ref[...] = h
    u = h * lax.rsqrt(jnp.mean(h * h, axis=-1, keepdims=True) + EPS)
    u_ref[...] = (u * g_ref[...]).astype(u_ref.dtype)


def _post_attention(oa, o1, o2, o3, gm, x, wpa, wpb, wo, g_mlp, tm):
    m, d = x.shape
    tm = min(tm, m)
    row = lambda i: (i, 0)
    const = lambda i: (0, 0)
    once = pl.Buffered(1)
    return pl.pallas_call(
        _post_kernel,
        grid=(m // tm,),
        in_specs=[pl.BlockSpec((tm, W_A), row),
                  pl.BlockSpec((tm, W_B), row),
                  pl.BlockSpec((tm, W_B), row),
                  pl.BlockSpec((tm, W_B), row),
                  pl.BlockSpec((tm, d), lambda i: (i, 0)),
                  pl.BlockSpec((tm, d), lambda i: (i, 1)),
                  pl.BlockSpec((tm, d), row),
                  pl.BlockSpec((W_A, d), const, pipeline_mode=once),
                  pl.BlockSpec((W_B, d), const, pipeline_mode=once),
                  pl.BlockSpec((d, d), const, pipeline_mode=once),
                  pl.BlockSpec((1, d), const)],
        out_specs=[pl.BlockSpec((tm, d), row), pl.BlockSpec((tm, d), row)],
        out_shape=[jax.ShapeDtypeStruct((m, d), F32), jax.ShapeDtypeStruct((m, d), BF16)],
        compiler_params=_params(("parallel",)),
        name="post_attention",
    )(oa, o1, o2, o3, gm, gm, x, wpa, wpb, wo, g_mlp.reshape(1, d))


def _mlp_kernel(u_ref, h_ref, wup_ref, wdn_ref, g_ref, o_ref, acc_ref):
    f = pl.program_id(1)

    @pl.when(f == 0)
    def _():
        acc_ref[...] = jnp.zeros_like(acc_ref)

    a = jnp.dot(u_ref[...], wup_ref[...], preferred_element_type=F32)
    a = jnp.square(jnp.maximum(a, 0.0)).astype(BF16)
    acc_ref[...] += jnp.dot(a, wdn_ref[...], preferred_element_type=F32)

    @pl.when(f == pl.num_programs(1) - 1)
    def _():
        y = h_ref[...] + acc_ref[...]
        y = y * lax.rsqrt(jnp.mean(y * y, axis=-1, keepdims=True) + EPS)
        o_ref[...] = y * g_ref[...]


def _mlp_final(u, h, wup, wdn, g_final, tm, tf):
    m, d = h.shape
    dff = wup.shape[1]
    tm, tf = min(tm, m), min(tf, dff)
    return pl.pallas_call(
        _mlp_kernel,
        grid=(m // tm, dff // tf),
        in_specs=[pl.BlockSpec((tm, d), lambda i, f: (i, 0)),
                  pl.BlockSpec((tm, d), lambda i, f: (i, 0)),
                  pl.BlockSpec((d, tf), lambda i, f: (0, f)),
                  pl.BlockSpec((tf, d), lambda i, f: (f, 0)),
                  pl.BlockSpec((1, d), lambda i, f: (0, 0))],
        out_specs=pl.BlockSpec((tm, d), lambda i, f: (i, 0)),
        out_shape=jax.ShapeDtypeStruct((m, d), F32),
        scratch_shapes=[pltpu.VMEM((tm, d), F32)],
        compiler_params=_params(("parallel", "arbitrary")),
        name="mlp_final",
    )(u, h, wup, wdn, g_final.reshape(1, d))


def _flash_init(m_ref, l_ref, acc_ref):
    m_ref[...] = jnp.full(m_ref.shape, NEG_INF, F32)
    if l_ref is not None:
        l_ref[...] = jnp.zeros(l_ref.shape, F32)
    acc_ref[...] = jnp.zeros(acc_ref.shape, F32)


def _lane_blocks_sum(p):
    out = p[:, 0:LANES]
    for n in range(1, p.shape[1] // LANES):
        out = out + p[:, n * LANES:(n + 1) * LANES]
    return out


def _flash_step(s, shift, m_prev, l_prev, acc_prev, pv_fn):
    m_cur = jnp.max(s, axis=-1, keepdims=True)
    if shift is not None:
        m_cur = m_cur - shift
    m_new = jnp.maximum(m_prev, m_cur)
    alpha = jnp.exp2(m_prev - m_new)
    m_adj = m_new if shift is None else m_new + shift
    p = jnp.exp2(s - jnp.concatenate([m_adj] * (s.shape[1] // LANES), axis=1))
    l_new = None if l_prev is None else alpha * l_prev + _lane_blocks_sum(p)
    alpha_acc = jnp.concatenate([alpha] * (acc_prev.shape[1] // LANES), axis=1)
    acc_new = alpha_acc * acc_prev + pv_fn(p.astype(BF16))
    return m_new, l_new, acc_new


def _flash_rows(rs, s, shift, v, m_ref, l_ref, acc_ref):
    m, l, acc = _flash_step(s, shift, m_ref[rs, :], None if l_ref is None else l_ref[rs, :], acc_ref[rs, :],
                            lambda p: jnp.dot(p, v, preferred_element_type=F32))
    m_ref[rs, :] = m
    if l_ref is not None:
        l_ref[rs, :] = l
    acc_ref[rs, :] = acc


def _row_total(l):
    return jnp.sum(l, axis=-1, keepdims=True)


def _alibi_tiles(slopes, tq):
    ng, nr = slopes.shape
    d = (jnp.arange(tq)[:, None] - jnp.arange(tq)[None, :]).astype(F32)
    off = -slopes[:, :, None, None] * d
    tiles = jnp.stack([off, jnp.where(d >= 0, off, NEG_INF)], axis=1).reshape(ng, 2, nr * tq, tq)
    step = jnp.broadcast_to((slopes * tq)[:, :, None, None], (ng, nr, tq, LANES)).reshape(ng, nr * tq, LANES)
    return tiles, step


def _window_tiles(slopes, tq, nj):
    ng, nr = slopes.shape
    d = jnp.arange(tq)[:, None] - jnp.arange(tq)[None, :]
    dist = (nj - 1 - jnp.arange(nj))[:, None, None] * tq + d[None]
    dist = dist[None, :, None]
    bias = jnp.where((dist >= 0) & (dist < WINDOW),
                     -slopes[:, None, :, None, None] * dist.astype(F32), NEG_INF)
    return bias.reshape(ng, nj, nr * tq, tq)


def _diff_lambda(lam_ref, lam_init):
    lf = lam_ref[...]
    e1 = jnp.exp(jnp.sum(lf[0:1] * lf[1:2], axis=-1, keepdims=True))
    e2 = jnp.exp(jnp.sum(lf[2:3] * lf[3:4], axis=-1, keepdims=True))
    return e1 - e2 + lam_init


def _diff_finish(o1, o2, lam, gain, lam_init):
    o = o1 - lam * o2
    y = o * lax.rsqrt(jnp.mean(o * o, axis=-1, keepdims=True) + EPS)
    return (y * gain) * (1.0 - lam_init)


def _diff_prompt_kernel(ii_ref, jj_ref, q_ref, k_ref, v_ref, tile_ref, step_ref, lam_ref, gain_ref, o_ref,
                        qs_ref, m_ref, l_ref, acc_ref, *, tq, lam_init):
    t = pl.program_id(2)
    i = ii_ref[t]
    j = jj_ref[t]
    nblk = 2 * R_A

    @pl.when(j == 0)
    def _():
        upper = _iota((tq, LANES), 1) >= HALF
        for r in range(R_A):
            qv = q_ref[:, r * LANES:(r + 1) * LANES]
            for c in range(2):
                qs_ref[(2 * r + c) * tq:(2 * r + c + 1) * tq, :] = jnp.where(upper == bool(c), qv, 0).astype(BF16)
        _flash_init(m_ref, l_ref, acc_ref)

    diag = (i == j).astype(I32)
    steps = (i - j).astype(F32)
    k = k_ref[...]
    v = v_ref[...]
    for n in range(nblk):
        rs = slice(n * tq, (n + 1) * tq)
        s = _nt_dot(qs_ref[rs, :], k) + tile_ref[0, diag, rs, :]
        _flash_rows(rs, s, step_ref[0, rs, :] * steps, v, m_ref, l_ref, acc_ref)

    @pl.when(j == i)
    def _():
        lam = _diff_lambda(lam_ref, lam_init)
        for r in range(R_A):
            b0 = 2 * r * tq
            o1 = acc_ref[b0:b0 + tq, :] / _row_total(l_ref[b0:b0 + tq, :])
            o2 = acc_ref[b0 + tq:b0 + 2 * tq, :] / _row_total(l_ref[b0 + tq:b0 + 2 * tq, :])
            y = _diff_finish(o1, o2, lam, gain_ref[...], lam_init)
            o_ref[:, r * LANES:(r + 1) * LANES] = y.astype(o_ref.dtype)


def _diff_prompt(qa, kva, lam_qk, gain, b, t, tq, lam_init):
    nq = t // tq
    ii, jj = _tri_tiles(nq)
    rows = 2 * R_A * tq
    slopes = jnp.repeat(_slopes2(H_A).reshape(HKV_A, R_A), 2, axis=1)
    tiles, step = _alibi_tiles(slopes, tq)
    kern = functools.partial(_diff_prompt_kernel, tq=tq, lam_init=lam_init)
    grid_spec = pltpu.PrefetchScalarGridSpec(
        num_scalar_prefetch=2,
        grid=(b, HKV_A, ii.shape[0]),
        in_specs=[pl.BlockSpec((tq, R_A * LANES), lambda bb, h, n, ii, jj: (bb * nq + ii[n], h)),
                  pl.BlockSpec((tq, LANES), lambda bb, h, n, ii, jj: (bb * nq + jj[n], h)),
                  pl.BlockSpec((tq, LANES), lambda bb, h, n, ii, jj: (bb * nq + jj[n], HKV_A + h)),
                  pl.BlockSpec((1, 2, rows, tq), lambda bb, h, n, ii, jj: (h, 0, 0, 0)),
                  pl.BlockSpec((1, rows, LANES), lambda bb, h, n, ii, jj: (h, 0, 0)),
                  pl.BlockSpec((4, DH_A), lambda bb, h, n, ii, jj: (0, 0)),
                  pl.BlockSpec((1, 2 * DH_A), lambda bb, h, n, ii, jj: (0, 0))],
        out_specs=pl.BlockSpec((tq, R_A * LANES), lambda bb, h, n, ii, jj: (bb * nq + ii[n], h)),
        scratch_shapes=[pltpu.VMEM((rows, LANES), BF16),
                        pltpu.VMEM((rows, LANES), F32),
                        pltpu.VMEM((rows, LANES), F32),
                        pltpu.VMEM((rows, LANES), F32)])
    return pl.pallas_call(
        kern, grid_spec=grid_spec,
        out_shape=jax.ShapeDtypeStruct((b * t, W_A), BF16),
        compiler_params=_params(("parallel", "parallel", "arbitrary")),
        name="diff_prompt",
    )(ii, jj, qa, kva, kva, tiles, step, lam_qk, gain.reshape(1, 2 * DH_A))


def _dup_half(v, g):
    rolled = pltpu.roll(v, HALF, 1)
    low = _iota(v.shape, 1) < HALF
    return jnp.where(low, v, rolled) if g % 2 == 0 else jnp.where(low, rolled, v)


def _value_ones(v, g):
    low = _iota(v.shape, 1) < HALF
    return jnp.where(low, v if g % 2 == 0 else pltpu.roll(v, HALF, 1), 1.0)


def _prep_kernel(nsa_ref, win_ref, wexp_ref, kcd_ref, vcd_ref, ksa_ref, vsd_ref, kwd_ref, vwd_ref,
                 tmp_ref, *, tp, nc):
    step = pl.program_id(1)
    gw = G_B * DH_B
    x = nsa_ref[...]
    w = win_ref[...]
    lane = _iota((tp, LANES), 1)
    low = lane < HALF
    blk = (step * tp + _iota((tp, LANES), 0)) >> _log2(SEL_BLOCK)
    onehot = jnp.where(lane - HALF == blk, 1.0, 0.0)
    for g in range(G_B):
        c0 = (g // 2) * LANES
        vk = x[:, 2 * gw + c0:2 * gw + c0 + LANES]
        klow = vk if g % 2 == 0 else pltpu.roll(vk, HALF, 1)
        ksa_ref[0, g] = jnp.where(low, klow, onehot).astype(BF16)
        vsd_ref[0, g] = _value_ones(x[:, 3 * gw + c0:3 * gw + c0 + LANES], g).astype(BF16)
        kwd_ref[0, g] = _dup_half(w[:, c0:c0 + LANES], g).astype(BF16)
        vwd_ref[0, g] = _value_ones(w[:, gw + c0:gw + c0 + LANES], g).astype(BF16)
    nb = tp // CMP_BLOCK
    hb = nb // 2
    for slot, dst in ((0, kcd_ref), (1, vcd_ref)):
        xc = x[:, slot * gw:(slot + 1) * gw]
        comp = jnp.sum(xc.reshape(nb, CMP_BLOCK, gw) * wexp_ref[slot][None], axis=1)
        for c in range(gw // LANES):
            tmp_ref[c * nb:(c + 1) * nb, :] = comp[:, c * LANES:(c + 1) * LANES]
        for par in range(2):
            start = pl.multiple_of(par * (nc // 2) + step * hb, 8)
            for g in range(G_B):
                rr = tmp_ref[pl.ds((g // 2) * nb + par, hb, stride=2), :]
                dst[0, g, pl.ds(start, hb), :] = _dup_half(rr, g)


def _nsa_prep(nsa, win, wexp, b, t, tp):
    nc = t // CMP_BLOCK
    gw = G_B * DH_B
    kern = functools.partial(_prep_kernel, tp=tp, nc=nc)
    small = jax.ShapeDtypeStruct((b, G_B, nc, LANES), F32)
    big = jax.ShapeDtypeStruct((b, G_B, t, LANES), BF16)
    small_spec = pl.BlockSpec((1, G_B, nc, LANES), lambda bb, s: (bb, 0, 0, 0))
    big_spec = pl.BlockSpec((1, G_B, tp, LANES), lambda bb, s: (bb, 0, s, 0))
    nt = t // tp
    return pl.pallas_call(
        kern, grid=(b, nt),
        in_specs=[pl.BlockSpec((tp, 4 * gw), lambda bb, s: (bb * nt + s, 0)),
                  pl.BlockSpec((tp, 2 * gw), lambda bb, s: (bb * nt + s, 0)),
                  pl.BlockSpec((2, CMP_BLOCK, gw), lambda bb, s: (0, 0, 0))],
        out_specs=[small_spec, small_spec, big_spec, big_spec, big_spec, big_spec],
        out_shape=[small, small, big, big, big, big],
        scratch_shapes=[pltpu.VMEM((gw // LANES * (tp // CMP_BLOCK), LANES), F32)],
        compiler_params=_params(("parallel", "arbitrary")),
        name="nsa_prep",
    )(nsa, win, wexp)


def _topk_bias(sc, sidx):
    rank = jnp.zeros(sc.shape, I32)
    for c in range(sc.shape[0]):
        other = sc[c:c + 1, :]
        beats = (other > sc) | ((other == sc) & (sidx > c))
        rank = rank + beats.astype(I32)
    return jnp.where(rank < TOP_N, 0.0, NEG_INF)


def _cmp_prompt_kernel(q_ref, kcd_ref, vcd_ref, gb_ref, ocmp_ref, qaug_ref, *, tq, nc):
    i = pl.program_id(1)
    ns = nc // 2
    lowq = _iota((tq, LANES), 1) < HALF
    rowc = _iota((nc, tq), 0)
    qpos = i * tq + _iota((nc, tq), 1)
    blk = jnp.where(rowc >= ns, 2 * (rowc - ns) + 1, 2 * rowc)
    c_mid = (blk * CMP_BLOCK).astype(F32) + (CMP_BLOCK - 1) * 0.5
    valid = blk * CMP_BLOCK + (CMP_BLOCK - 1) <= qpos
    adist = jnp.abs(qpos.astype(F32) - c_mid)
    sig = jax.nn.sigmoid(gb_ref[...])
    sidx = _iota((ns, tq), 0)
    qp = i * tq + _iota((ns, tq), 1)
    forced = (sidx == (qp >> _log2(SEL_BLOCK))) | (sidx == 0)
    valid_s = sidx * SEL_BLOCK <= qp
    for g in range(G_B):
        kc = kcd_ref[0, g].astype(BF16)
        vc = vcd_ref[0, g].astype(BF16)
        imp = jnp.zeros((nc, tq), F32)
        outs, qlows = [], []
        for r in range(R_B):
            h = g * R_B + r
            qv = q_ref[:, (h // 2) * LANES:(h // 2 + 1) * LANES]
            qm = jnp.where(lowq == (h % 2 == 0), qv, 0).astype(BF16)
            s = _nt_dot(kc, qm)
            s = s - (2.0 ** (-8.0 * (h + 1) / H_B) * LOG2E) * adist
            sm = jnp.where(valid, s, NEG_INF)
            e = jnp.exp2(sm - jnp.max(sm, axis=0, keepdims=True))
            p = jnp.where(valid, e / jnp.sum(e, axis=0, keepdims=True), 0.0)
            imp = imp + p
            o2 = jnp.dot(p.T.astype(BF16), vc, preferred_element_type=F32)
            outs.append(o2 * sig[:, h:h + 1])
            qlows.append(qv if h % 2 == 0 else pltpu.roll(qv, HALF, 1))
        for k in range(R_B // 2):
            c0 = (g * (R_B // 2) + k) * LANES
            ocmp_ref[:, c0:c0 + LANES] = jnp.where(lowq, outs[2 * k], outs[2 * k + 1])
        sc = jnp.where(forced, FORCE_SCORE, jnp.where(valid_s, imp[:ns] + imp[ns:], -1.0))
        bias_t = _topk_bias(sc, sidx)
        bias = jnp.concatenate([bias_t] * (LANES // ns), axis=0).T.astype(BF16)
        for r in range(R_B):
            qaug_ref[0, g, r] = jnp.where(lowq, qlows[r], bias).astype(BF16)


def _cmp_prompt(zq, kcd, vcd, gb, b, t, tq):
    nc = t // CMP_BLOCK
    nq = t // tq
    kern = functools.partial(_cmp_prompt_kernel, tq=tq, nc=nc)
    small_spec = pl.BlockSpec((1, G_B, nc, LANES), lambda bb, i: (bb, 0, 0, 0))
    return pl.pallas_call(
        kern, grid=(b, nq),
        in_specs=[pl.BlockSpec((tq, W_B), lambda bb, i: (bb * nq + i, W_A // W_B)),
                  small_spec, small_spec,
                  pl.BlockSpec((tq, LANES), lambda bb, i: (bb * nq + i, 0))],
        out_specs=[pl.BlockSpec((tq, W_B), lambda bb, i: (bb * nq + i, 0)),
                   pl.BlockSpec((1, G_B, R_B, tq, LANES), lambda bb, i: (bb, 0, 0, i, 0))],
        out_shape=[jax.ShapeDtypeStruct((b * t, W_B), F32),
                   jax.ShapeDtypeStruct((b, G_B, R_B, t, LANES), BF16)],
        compiler_params=_params(("parallel", "parallel")),
        name="cmp_prompt",
    )(zq, kcd, vcd, gb)


def _gate_column(sig, col):
    return jnp.sum(jnp.where(_iota(sig.shape, 1) == col, sig, 0.0), axis=-1, keepdims=True)


def _group_finish(o_ref, gb_ref, acc_ref, g, branch, tq):
    sig = jax.nn.sigmoid(gb_ref[...])
    low = _iota((tq, LANES), 1) < HALF
    outs = []
    for r in range(R_B):
        a = acc_ref[r * tq:(r + 1) * tq, :]
        o = a / pltpu.roll(a, HALF, 1)
        outs.append(o * _gate_column(sig, branch * H_B + g * R_B + r))
    for k in range(R_B // 2):
        o_ref[:, k * LANES:(k + 1) * LANES] = jnp.where(low, outs[2 * k], pltpu.roll(outs[2 * k + 1], HALF, 1))


def _sel_prompt_kernel(ii_ref, jj_ref, qa_ref, k_ref, v_ref, tile_ref, step_ref, gb_ref, o_ref,
                       m_ref, acc_ref, *, tq):
    g = pl.program_id(1)
    t = pl.program_id(2)
    i = ii_ref[t]
    j = jj_ref[t]

    @pl.when(j == 0)
    def _():
        _flash_init(m_ref, None, acc_ref)

    diag = (i == j).astype(I32)
    steps = (i - j).astype(F32)
    k = k_ref[0, 0]
    v = v_ref[0, 0]
    for r in range(R_B):
        rs = slice(r * tq, (r + 1) * tq)
        s = _nt_dot(qa_ref[0, 0, r], k) + tile_ref[0, diag, rs, :]
        _flash_rows(rs, s, step_ref[0, rs, :] * steps, v, m_ref, None, acc_ref)

    @pl.when(j == i)
    def _():
        _group_finish(o_ref, gb_ref, acc_ref, g, 1, tq)


def _sel_prompt(qaug, ksa, vsd, gb, b, t, tq):
    nq = t // tq
    ii, jj = _tri_tiles(nq)
    rows = R_B * tq
    tiles, step = _alibi_tiles(_slopes2(H_B).reshape(G_B, R_B), tq)
    kern = functools.partial(_sel_prompt_kernel, tq=tq)
    kv_spec = pl.BlockSpec((1, 1, tq, LANES), lambda bb, g, n, ii, jj: (bb, g, jj[n], 0))
    grid_spec = pltpu.PrefetchScalarGridSpec(
        num_scalar_prefetch=2,
        grid=(b, G_B, ii.shape[0]),
        in_specs=[pl.BlockSpec((1, 1, R_B, tq, LANES), lambda bb, g, n, ii, jj: (bb, g, 0, ii[n], 0)),
                  kv_spec, kv_spec,
                  pl.BlockSpec((1, 2, rows, tq), lambda bb, g, n, ii, jj: (g, 0, 0, 0)),
                  pl.BlockSpec((1, rows, LANES), lambda bb, g, n, ii, jj: (g, 0, 0)),
                  pl.BlockSpec((tq, LANES), lambda bb, g, n, ii, jj: (bb * nq + ii[n], 0))],
        out_specs=pl.BlockSpec((tq, R_B * DH_B), lambda bb, g, n, ii, jj: (bb * nq + ii[n], g)),
        scratch_shapes=[pltpu.VMEM((rows, LANES), F32),
                        pltpu.VMEM((rows, LANES), F32)])
    return pl.pallas_call(
        kern, grid_spec=grid_spec,
        out_shape=jax.ShapeDtypeStruct((b * t, W_B), F32),
        compiler_params=_params(("parallel", "parallel", "arbitrary")),
        name="sel_prompt",
    )(ii, jj, qaug, ksa, vsd, tiles, step, gb)


def _win_prompt_kernel(q_ref, k_ref, v_ref, tile_ref, gb_ref, o_ref, qs_ref, m_ref, acc_ref, *, tq, nj):
    g = pl.program_id(1)
    i = pl.program_id(2)
    j = pl.program_id(3)

    @pl.when(j == 0)
    def _():
        low = _iota((tq, LANES), 1) < HALF
        for r in range(R_B):
            qv = q_ref[:, (r // 2) * LANES:(r // 2 + 1) * LANES]
            qs_ref[r * tq:(r + 1) * tq, :] = jnp.where(low == (r % 2 == 0), qv, 0).astype(BF16)
        _flash_init(m_ref, None, acc_ref)

    @pl.when(i - (nj - 1) + j >= 0)
    def _():
        k = k_ref[0, 0]
        v = v_ref[0, 0]
        for r in range(R_B):
            rs = slice(r * tq, (r + 1) * tq)
            s = _nt_dot(qs_ref[rs, :], k) + tile_ref[0, j, rs, :]
            _flash_rows(rs, s, None, v, m_ref, None, acc_ref)

    @pl.when(j == nj - 1)
    def _():
        _group_finish(o_ref, gb_ref, acc_ref, g, 2, tq)


def _win_prompt(zq, kwd, vwd, gb, b, t, tq):
    nq = t // tq
    nj = -(-WINDOW // tq) + 1
    qoff = W_A // (R_B * DH_B)
    rows = R_B * tq
    tiles = _window_tiles(_slopes2(H_B).reshape(G_B, R_B), tq, nj)
    kern = functools.partial(_win_prompt_kernel, tq=tq, nj=nj)
    kv_spec = pl.BlockSpec((1, 1, tq, LANES),
                           lambda bb, g, i, j: (bb, g, jnp.maximum(i - (nj - 1) + j, 0), 0))
    return pl.pallas_call(
        kern, grid=(b, G_B, nq, nj),
        in_specs=[pl.BlockSpec((tq, R_B * DH_B), lambda bb, g, i, j: (bb * nq + i, qoff + g)),
                  kv_spec, kv_spec,
                  pl.BlockSpec((1, nj, rows, tq), lambda bb, g, i, j: (g, 0, 0, 0)),
                  pl.BlockSpec((tq, LANES), lambda bb, g, i, j: (bb * nq + i, 0))],
        out_specs=pl.BlockSpec((tq, R_B * DH_B), lambda bb, g, i, j: (bb * nq + i, g)),
        out_shape=jax.ShapeDtypeStruct((b * t, W_B), F32),
        scratch_shapes=[pltpu.VMEM((rows, LANES), BF16),
                        pltpu.VMEM((rows, LANES), F32),
                        pltpu.VMEM((rows, LANES), F32)],
        compiler_params=_params(("parallel", "parallel", "parallel", "arbitrary")),
        name="win_prompt",
    )(zq, kwd, vwd, tiles, gb)


def _split_w_in(w_in):
    sizes = (W_A, HKV_A * 2 * DH_A, HKV_A * 2 * DH_A, W_B, 6 * G_B * DH_B, 3 * H_B, 2 * w_in.shape[0])
    offs = np.concatenate([[0], np.cumsum(sizes)])
    seg = lambda a, b_: w_in[:, a:b_]
    gw = G_B * DH_B
    return dict(
        q=jnp.concatenate([seg(offs[0], offs[1]) * (DH_A ** -0.5 * LOG2E),
                           seg(offs[3], offs[4]) * (DH_B ** -0.5 * LOG2E)], axis=1).astype(BF16),
        kva=seg(offs[1], offs[3]).astype(BF16),
        nsa=seg(offs[4], offs[4] + 4 * gw).astype(BF16),
        win=seg(offs[4] + 4 * gw, offs[5]).astype(BF16),
        gb=jnp.pad(seg(offs[5], offs[6]), ((0, 0), (0, LANES - 3 * H_B))).astype(BF16),
        gm=seg(offs[6], offs[7]).astype(BF16))


def _in_projection(u, w, tm, q_dtype=BF16, kv_token_rows=False):
    zq, = _matmul(u, w["q"], (q_dtype,), tm, 1024, "proj_q")
    if kv_token_rows:
        kva32, kva16 = _matmul_token_rows(u, w["kva"], 512, "proj_kva")
    else:
        kva32, kva16 = _matmul(u, w["kva"], (F32, BF16), tm, 512, "proj_kva")
    nsa32, = _matmul(u, w["nsa"], (F32,), tm, 512, "proj_nsa")
    win32, = _matmul(u, w["win"], (F32,), tm, 512, "proj_win")
    gb, = _matmul(u, w["gb"], (F32,), tm, LANES, "proj_gb")
    gm, = _matmul(u, w["gm"], (F32,), tm, 1024, "proj_gm")
    return zq, kva32, kva16, nsa32, win32, gb, gm


def _prompt_mixer(u, w, lam_qk, gain, wexp, b, t, lam_init, tq=256):
    tq = min(tq, t)
    zq, kva32, kva16, nsa32, win32, gb, gm = _in_projection(u, w, 1024, kv_token_rows=True)
    nsa_t = _matmul_tokens_last(w["nsa"].T, u, b, 512, 1024, "proj_nsa_t")
    tbig = min(2 * tq, t)
    oa = _diff_prompt(zq, kva16, lam_qk, gain, b, t, tbig, lam_init)
    kcd, vcd, ksa, vsd, kwd, vwd = _nsa_prep(nsa32, win32, wexp, b, t, min(512, t))
    ocmp, qaug = _cmp_prompt(zq, kcd, vcd, gb, b, t, tq)
    osel = _sel_prompt(qaug, ksa, vsd, gb, b, t, tbig)
    owin = _win_prompt(zq, kwd, vwd, gb, b, t, tbig)
    return (oa, ocmp, osel, owin, gm), (kva32, nsa_t, win32)


def _pad_rows(x, rows):
    if x.shape[0] == rows:
        return x
    return jnp.concatenate([x, jnp.zeros((rows - x.shape[0], x.shape[1]), x.dtype)], axis=0)


def _diff_sample_kernel(pt_ref, q_ref, new_ref, lam_ref, gain_ref, *rest, tn, pps, past_len, lam_init):
    page_refs = rest[:pps]
    o_ref, qs_ref, m_ref, l_ref, acc_ref = rest[pps:]
    c = pl.program_id(1)
    rows = 2 * R_A * tn
    hw = HKV_A * 2 * DH_A
    stride = 2 * HKV_A
    row = _iota((rows, 1), 0)
    tq = row & (tn - 1)

    def slopes(hkv):
        head = hkv * R_A + (row >> _log2(2 * tn))
        return jnp.exp2(-8.0 * (head + 1).astype(F32) / H_A) * LOG2E

    def update(hkv, s, pv_fn):
        m, l, acc = _flash_step(s, None, m_ref[hkv], l_ref[hkv], acc_ref[hkv], pv_fn)
        m_ref[hkv] = m
        l_ref[hkv] = l
        acc_ref[hkv] = acc

    @pl.when(c == 0)
    def _():
        upper = _iota((tn, LANES), 1) >= HALF
        for hkv in range(HKV_A):
            parts = []
            for r in range(R_A):
                h = hkv * R_A + r
                qv = q_ref[:, h * LANES:(h + 1) * LANES]
                parts += [jnp.where(upper == bool(cc), qv, 0.0) for cc in range(2)]
            qs_ref[hkv] = jnp.concatenate(parts, axis=0).astype(BF16)
        _flash_init(m_ref, l_ref, acc_ref)

    kpos = c * (pps * PAGE_SIZE) + _iota((1, pps * PAGE_SIZE), 1)
    dist = ((past_len + tq) - kpos).astype(F32)
    scores = []
    for hkv in range(HKV_A):
        q = qs_ref[hkv]
        s = jnp.concatenate(
            [_nt_dot(q, pr[pl.ds(hkv, PAGE_SIZE, stride=stride), :].astype(BF16)) for pr in page_refs], axis=1)
        scores.append(s - slopes(hkv) * dist)
    for hkv in range(HKV_A):

        def pv(p, hkv=hkv):
            out = None
            for n, pr in enumerate(page_refs):
                vals = pr[pl.ds(HKV_A + hkv, PAGE_SIZE, stride=stride), :].astype(BF16)
                part = jnp.dot(p[:, n * PAGE_SIZE:(n + 1) * PAGE_SIZE], vals, preferred_element_type=F32)
                out = part if out is None else out + part
            return out

        update(hkv, scores[hkv], pv)

    @pl.when(c == pl.num_programs(1) - 1)
    def _():
        lam = _diff_lambda(lam_ref, lam_init)
        newp = _pad_rows(new_ref[...], LANES)
        dnew = tq - _iota((1, LANES), 1)
        for hkv in range(HKV_A):
            c0 = hkv * LANES
            s = _nt_dot(qs_ref[hkv], newp[:, c0:c0 + LANES].astype(BF16))
            s = jnp.where(dnew >= 0, s - slopes(hkv) * dnew.astype(F32), NEG_INF)
            vn = newp[:, hw + c0:hw + c0 + LANES].astype(BF16)
            update(hkv, s, lambda p, vn=vn: jnp.dot(p, vn, preferred_element_type=F32))
            o = acc_ref[hkv] / _row_total(l_ref[hkv])
            for r in range(R_A):
                b0 = 2 * r * tn
                y = _diff_finish(o[b0:b0 + tn], o[b0 + tn:b0 + 2 * tn], lam, gain_ref[...], lam_init)
                h = hkv * R_A + r
                o_ref[:, h * LANES:(h + 1) * LANES] = y


def _diff_sample(zq, kva_new, cache, pt, lam_qk, gain, bd, tn, n_pages, pps, lam_init):
    rows = 2 * R_A * tn
    page_rows = PAGE_SIZE * 2 * HKV_A
    kern = functools.partial(_diff_sample_kernel, tn=tn, pps=pps, past_len=n_pages * PAGE_SIZE,
                             lam_init=lam_init)

    def page_spec(p):
        return pl.BlockSpec((page_rows, LANES), lambda bb, c, pt: (pt[bb * n_pages + c * pps + p], 0))

    grid_spec = pltpu.PrefetchScalarGridSpec(
        num_scalar_prefetch=1,
        grid=(bd, n_pages // pps),
        in_specs=[pl.BlockSpec((tn, W_A), lambda bb, c, pt: (bb, 0)),
                  pl.BlockSpec((tn, W_A), lambda bb, c, pt: (bb, 0)),
                  pl.BlockSpec((4, DH_A), lambda bb, c, pt: (0, 0)),
                  pl.BlockSpec((1, 2 * DH_A), lambda bb, c, pt: (0, 0))]
                 + [page_spec(p) for p in range(pps)],
        out_specs=pl.BlockSpec((tn, W_A), lambda bb, c, pt: (bb, 0)),
        scratch_shapes=[pltpu.VMEM((HKV_A, rows, LANES), BF16),
                        pltpu.VMEM((HKV_A, rows, LANES), F32),
                        pltpu.VMEM((HKV_A, rows, LANES), F32),
                        pltpu.VMEM((HKV_A, rows, LANES), F32)])
    return pl.pallas_call(
        kern, grid_spec=grid_spec,
        out_shape=jax.ShapeDtypeStruct((bd * tn, W_A), F32),
        compiler_params=_params(("parallel", "arbitrary")),
        name="diff_sample",
    )(pt, zq, kva_new, lam_qk, gain.reshape(1, 2 * DH_A), *([cache] * pps))


def _sample_queries(qblk, tn):
    low = _iota((tn, LANES), 1) < HALF
    zero = jnp.zeros((tn, LANES), F32)
    cols = ([], [])
    for r in range(R_B):
        for g in range(G_B):
            h = g * R_B + r
            qv = qblk[:, (h // 2) * LANES:(h // 2 + 1) * LANES]
            if h % 2 != g % 2:
                qv = pltpu.roll(qv, HALF, 1)
            cols[g // 2].append(jnp.where(low == (g % 2 == 0), qv, 0.0))
            cols[1 - g // 2].append(zero)
    return jnp.concatenate([jnp.concatenate(c, axis=0) for c in cols], axis=1).astype(BF16)


def _sample_outputs(o_ref, o_all, sig, branch, tn):
    low = _iota((tn, LANES), 1) < HALF
    for g in range(G_B):
        for k in range(R_B // 2):
            parts = []
            for r in (2 * k, 2 * k + 1):
                r0 = (r * G_B + g) * tn
                src = o_all[r0:r0 + tn, (g // 2) * LANES:(g // 2 + 1) * LANES]
                if g % 2 != r % 2:
                    src = pltpu.roll(src, HALF, 1)
                col = branch * H_B + g * R_B + r
                parts.append(src * sig[:, col:col + 1])
            c0 = (g * (R_B // 2) + k) * LANES
            o_ref[:, c0:c0 + LANES] = jnp.where(low, parts[0], parts[1])


def _query_consts(idx, tn, past_len):
    r_q = idx >> _log2(G_B * tn)
    g_q = (idx >> _log2(tn)) & (G_B - 1)
    slope = jnp.exp2(-8.0 * (g_q * R_B + r_q + 1).astype(F32) / H_B) * LOG2E
    return g_q >> 1, slope, past_len + (idx & (tn - 1))


def _topk_member(sc):
    ridx = _iota(sc.shape, 0)
    removed = -3.0e38

    def body(_, carry):
        cur, mem = carry
        top = jnp.max(cur, axis=0, keepdims=True)
        first = jnp.min(jnp.where(cur == top, ridx, sc.shape[0]), axis=0, keepdims=True)
        pick = ridx == first
        return jnp.where(pick, removed, cur), jnp.where(pick, 1.0, mem)

    return lax.fori_loop(0, TOP_N, body, (sc, jnp.zeros(sc.shape, F32)), unroll=True)[1]


def _col_of(row_vec):
    return jnp.broadcast_to(row_vec, (LANES, LANES)).T


def _nsa_sample_kernel(pt_ref, q_ref, new_ref, gb_ref, wc_ref, e_ref, tile_ref, step_ref, *rest,
                       tn, pps, past_len):
    page_refs = rest[:pps]
    (ocmp_ref, osel_ref, qop_ref, kcvc_ref, imp_ref, sc_ref, selb_ref,
     m_ref, l_ref, acc_ref) = rest[pps:]
    ph = pl.program_id(1)
    c = pl.program_id(2)
    last = pl.num_programs(2) - 1
    gw = G_B * DH_B
    ncp = past_len // CMP_BLOCK
    nsp = past_len // SEL_BLOCK
    ck = pps * PAGE_SIZE
    lane_consts = lambda: _query_consts(_iota((1, LANES), 1), tn, past_len)
    row_consts = lambda: _query_consts(_iota((LANES, 1), 0), tn, past_len)

    @pl.when((ph == 0) & (c == 0))
    def _():
        qop_ref[...] = _sample_queries(q_ref[...], tn)

    @pl.when(ph == 0)
    def _():
        wc = wc_ref[...].astype(BF16)
        low8 = _iota((8, LANES), 1) < HALF
        comps = [_nt_dot(wc, jnp.concatenate([page_refs[2 * pp][...], page_refs[2 * pp + 1][...]],
                                             axis=1).astype(BF16)) for pp in range(pps // 2)]
        for pp, comp in enumerate(comps):
            row0 = pl.multiple_of((c * (pps // 2) + pp) * 8, 8)
            for slot in range(2):
                for p in range(2):
                    va = slot * G_B + 2 * p
                    c0 = slot * gw + p * LANES
                    piece = jnp.where(low8, comp[va * 8:(va + 1) * 8, c0:c0 + LANES],
                                      comp[(va + 1) * 8:(va + 2) * 8, c0:c0 + LANES])
                    kcvc_ref[pl.ds(row0, 8), c0:c0 + LANES] = piece

    @pl.when((ph == 0) & (c == last))
    def _():
        _, slope_l, qpos_l = lane_consts()
        kcvc = kcvc_ref[...]
        s = _nt_dot(kcvc[:, 0:gw].astype(BF16), qop_ref[...])
        blk = _iota((ncp, LANES), 0)
        c_mid = (blk * CMP_BLOCK).astype(F32) + (CMP_BLOCK - 1) * 0.5
        valid = blk * CMP_BLOCK + (CMP_BLOCK - 1) <= qpos_l
        s = s - slope_l * jnp.abs(qpos_l.astype(F32) - c_mid)
        sm = jnp.where(valid, s, NEG_INF)
        e = jnp.exp2(sm - jnp.max(sm, axis=0, keepdims=True))
        p_t = jnp.where(valid, e / jnp.sum(e, axis=0, keepdims=True), 0.0)
        p = p_t.T.astype(BF16)
        o_all = jnp.dot(p, kcvc[:, gw:2 * gw].astype(BF16), preferred_element_type=F32)
        _sample_outputs(ocmp_ref, o_all, jax.nn.sigmoid(gb_ref[...]), 0, tn)
        imp = p_t
        for k in range(1, R_B):
            imp = imp + pltpu.roll(p_t, k * G_B * tn, 1)
        imp_ref[...] = imp
        imp = imp_ref[pl.ds(0, nsp, stride=2), :] + imp_ref[pl.ds(1, nsp, stride=2), :]
        sidx = _iota((nsp, LANES), 0)
        forced = (sidx == (qpos_l >> _log2(SEL_BLOCK))) | (sidx == 0)
        sc_ref[0:nsp, :] = jnp.where(forced, FORCE_SCORE,
                                     jnp.where(sidx * SEL_BLOCK <= qpos_l, imp, -1.0))
        tail = sc_ref.shape[0] - nsp
        tidx = nsp + _iota((tail, LANES), 0)
        tforced = (tidx == (qpos_l >> _log2(SEL_BLOCK))) | (tidx == 0)
        tsc = jnp.where(tforced, FORCE_SCORE, jnp.where(tidx * SEL_BLOCK <= qpos_l, 0.0, -1.0))
        sc_ref[nsp:, :] = jnp.where(tidx == nsp, tsc, NEG_INF)
        selb_ref[...] = _topk_member(sc_ref[...])
        _flash_init(m_ref, l_ref, acc_ref)

    def sel_step(kt, vt, shift, s_bias_fn):
        s = s_bias_fn(jnp.dot(qop_ref[...], kt, preferred_element_type=F32))
        m, l, acc = _flash_step(s, shift, m_ref[...], l_ref[...], acc_ref[...], lambda p: _nt_dot(p, vt))
        m_ref[...] = m
        l_ref[...] = l
        acc_ref[...] = acc

    @pl.when(ph == 1)
    def _():
        kt = jnp.concatenate([pr[0:gw, :] for pr in page_refs], axis=1).astype(BF16)
        vt = jnp.concatenate([pr[gw:2 * gw, :] for pr in page_refs], axis=1).astype(BF16)
        nblk = ck // SEL_BLOCK
        member = _pad_rows(selb_ref[pl.ds(pl.multiple_of(c * nblk, 8), nblk), :], LANES).T
        mask = jnp.dot(member.astype(BF16), e_ref[...], preferred_element_type=F32)
        shift = step_ref[...] * (last - c).astype(F32)
        sel_step(kt, vt, shift, lambda s: jnp.where(mask > 0.5, s + tile_ref[...], NEG_INF))

    @pl.when((ph == 1) & (c == last))
    def _():
        xn = _pad_rows(new_ref[:, 2 * gw:4 * gw], LANES)
        kt = jnp.concatenate([xn[:, q * LANES:(q + 1) * LANES].T for q in range(2)], axis=0).astype(BF16)
        vt = jnp.concatenate([xn[:, gw + q * LANES:gw + (q + 1) * LANES].T for q in range(2)], axis=0).astype(BF16)
        _, slope_r, qpos_r = row_consts()
        member = _col_of(selb_ref[nsp:nsp + 1, :])
        col = _iota((1, LANES), 1)
        dist = qpos_r - (past_len + col)
        ok = (member > 0.5) & (dist >= 0) & (col < tn)
        sel_step(kt, vt, None, lambda s: jnp.where(ok, s - slope_r * dist.astype(F32), NEG_INF))
        _sample_outputs(osel_ref, acc_ref[...] / _row_total(l_ref[...]), jax.nn.sigmoid(gb_ref[...]), 1, tn)


def _compress_weights(w_cmp):
    nblk = 2 * PAGE_SIZE // CMP_BLOCK
    tok = jnp.arange(2 * PAGE_SIZE)
    w = jnp.transpose(w_cmp, (0, 2, 1))[:, :, tok % CMP_BLOCK]
    hit = (tok[None, :] // CMP_BLOCK) == jnp.arange(nblk)[:, None]
    return jnp.where(hit[None, None], w[:, :, None, :], 0.0).reshape(2 * G_B * nblk, 2 * PAGE_SIZE)


def _nsa_sample(zq, nsa_new, gb, w_cmp, cache, pt, bd, tn, n_pages, pps):
    past_len = n_pages * PAGE_SIZE
    gw = G_B * DH_B
    ncp = past_len // CMP_BLOCK
    nsp = past_len // SEL_BLOCK
    nsc = -(-(nsp + 1) // 8) * 8
    ck = pps * PAGE_SIZE
    assert pps % 2 == 0 and 2 * PAGE_SIZE // CMP_BLOCK == 8 and (ck // SEL_BLOCK) % 8 == 0
    kern = functools.partial(_nsa_sample_kernel, tn=tn, pps=pps, past_len=past_len)
    expand = (jnp.arange(ck)[None, :] // SEL_BLOCK == jnp.arange(LANES)[:, None]).astype(BF16)
    qrow = jnp.arange(LANES)
    slope_r = _slopes2(H_B)[((qrow // tn) % G_B) * R_B + qrow // (G_B * tn)]
    tile = -slope_r[:, None] * ((qrow % tn)[:, None] + ck - jnp.arange(ck)[None, :]).astype(F32)
    step = jnp.broadcast_to((slope_r * ck)[:, None], (LANES, LANES))

    def page_spec(p):
        return pl.BlockSpec((2 * gw, PAGE_SIZE),
                            lambda bb, ph, c, pt, p=p: (pt[bb * n_pages + c * pps + p] * 2 + ph, 0))

    row_spec = lambda width, col: pl.BlockSpec((tn, width), lambda bb, ph, c, pt: (bb, col))
    grid_spec = pltpu.PrefetchScalarGridSpec(
        num_scalar_prefetch=1,
        grid=(bd, 2, n_pages // pps),
        in_specs=[row_spec(W_B, W_A // W_B), row_spec(4 * gw, 0), row_spec(LANES, 0),
                  pl.BlockSpec((2 * G_B * 8, 2 * PAGE_SIZE), lambda bb, ph, c, pt: (0, 0)),
                  pl.BlockSpec((LANES, ck), lambda bb, ph, c, pt: (0, 0)),
                  pl.BlockSpec((LANES, ck), lambda bb, ph, c, pt: (0, 0)),
                  pl.BlockSpec((LANES, LANES), lambda bb, ph, c, pt: (0, 0))]
                 + [page_spec(p) for p in range(pps)],
        out_specs=[row_spec(W_B, 0), row_spec(W_B, 0)],
        scratch_shapes=[pltpu.VMEM((LANES, gw), BF16),
                        pltpu.VMEM((ncp, 2 * gw), F32),
                        pltpu.VMEM((ncp, LANES), F32),
                        pltpu.VMEM((nsc, LANES), F32),
                        pltpu.VMEM((nsc, LANES), F32),
                        pltpu.VMEM((LANES, LANES), F32),
                        pltpu.VMEM((LANES, LANES), F32),
                        pltpu.VMEM((LANES, gw), F32)])
    return pl.pallas_call(
        kern, grid_spec=grid_spec,
        out_shape=[jax.ShapeDtypeStruct((bd * tn, W_B), F32)] * 2,
        compiler_params=_params(("parallel", "arbitrary", "arbitrary")),
        name="nsa_sample",
    )(pt, zq, nsa_new, gb, _compress_weights(w_cmp), expand, tile, step, *([cache] * pps))


def _win_sample_kernel(q_ref, new_ref, st_ref, gb_ref, o_ref, ns_ref, *, tn, past_len):
    gw = G_B * DH_B
    wb = st_ref.shape[2]
    qs = _sample_queries(q_ref[...], tn)
    st = st_ref[0]
    new_t = _pad_rows(new_ref[...], LANES).T
    _, slope_r, qpos_r = _query_consts(_iota((LANES, 1), 0), tn, past_len)

    def scores(x):
        return jnp.dot(qs, x[0:gw, :].astype(BF16), preferred_element_type=F32)

    kpos_s = (past_len - wb) + _iota((1, wb), 1)
    d_s = qpos_r - kpos_s
    s_s = jnp.where((d_s >= 0) & (d_s < WINDOW) & (kpos_s >= 0), scores(st) - slope_r * d_s.astype(F32), NEG_INF)
    col_n = _iota((1, LANES), 1)
    d_n = qpos_r - (past_len + col_n)
    s_n = jnp.where((d_n >= 0) & (d_n < WINDOW) & (col_n < tn), scores(new_t) - slope_r * d_n.astype(F32), NEG_INF)
    s = jnp.concatenate([s_s, s_n], axis=1)
    e = jnp.exp2(s - jnp.max(s, axis=-1, keepdims=True))
    p = (e / jnp.sum(e, axis=-1, keepdims=True)).astype(BF16)
    o_all = (_nt_dot(p[:, :wb], st[gw:2 * gw, :].astype(BF16))
             + _nt_dot(p[:, wb:], new_t[gw:2 * gw, :].astype(BF16)))
    _sample_outputs(o_ref, o_all, jax.nn.sigmoid(gb_ref[...]), 2, tn)
    shifted = pltpu.roll(st, wb - tn, 1)
    tail = jnp.concatenate([jnp.zeros((2 * gw, wb - LANES), F32), pltpu.roll(new_t, LANES - tn, 1)], axis=1)
    ns_ref[0] = jnp.where(_iota((2 * gw, wb), 1) >= wb - tn, tail, shifted)


def _win_sample(zq, win_new, state_t, gb, bd, tn, past_len):
    gw = G_B * DH_B
    wb = state_t.shape[2]
    assert wb == WINDOW and wb == min(WINDOW, past_len) and wb % LANES == 0
    kern = functools.partial(_win_sample_kernel, tn=tn, past_len=past_len)
    return pl.pallas_call(
        kern, grid=(bd,),
        in_specs=[pl.BlockSpec((tn, W_B), lambda bb: (bb, W_A // W_B)),
                  pl.BlockSpec((tn, 2 * gw), lambda bb: (bb, 0)),
                  pl.BlockSpec((1, 2 * gw, wb), lambda bb: (bb, 0, 0)),
                  pl.BlockSpec((tn, LANES), lambda bb: (bb, 0))],
        out_specs=[pl.BlockSpec((tn, W_B), lambda bb: (bb, 0)),
                   pl.BlockSpec((1, 2 * gw, wb), lambda bb: (bb, 0, 0))],
        out_shape=[jax.ShapeDtypeStruct((bd * tn, W_B), F32),
                   jax.ShapeDtypeStruct((bd, 2 * gw, wb), F32)],
        compiler_params=_params(("parallel",)),
        name="win_sample",
    )(zq, win_new, state_t, gb)


def kernel(x_prompt, x_sample, cache_diff_kv, cache_nsa_kv, state_win_kv, page_table, w_in, w_proj_a,
           w_proj_b, w_out, lambda_qk, diff_gain, w_cmp, norm_attn, norm_mlp, w_up, w_down, norm_final):
    depth = w_in.shape[0]
    assert depth == 1
    b, t, d = x_prompt.shape
    bd, tn, _ = x_sample.shape
    l = 0
    lam_init = 0.8 - 0.6 * math.exp(-0.3 * l)
    w = _split_w_in(w_in[l])
    wpa, wpb, wo = w_proj_a[l].astype(BF16), w_proj_b[l].astype(BF16), w_out[l].astype(BF16)
    wup, wdn = w_up[l].astype(BF16), w_down[l].astype(BF16)
    wexp = jnp.repeat(w_cmp[l], DH_B, axis=-1)

    xp = x_prompt.reshape(b * t, d)
    up = _rmsnorm(xp, norm_attn[l], BF16, 512)
    (oa, ocmp, osel, owin, gm), (kva32, nsa_t, win32) = _prompt_mixer(
        up, w, lambda_qk[l], diff_gain[l], wexp, b, t, lam_init)
    hp, u2 = _post_attention(oa, ocmp, osel, owin, gm, xp, wpa, wpb, wo, norm_mlp[l], 256)
    y_prompt = _mlp_final(u2, hp, wup, wdn, norm_final, 512, 1024).reshape(b, t, d)
    keep = min(WINDOW, t)
    diff_kv_prompt = kva32.reshape(1, b, t, 2, HKV_A, 2 * DH_A)
    nsa_kv_prompt = jnp.transpose(nsa_t.reshape(b, 4, G_B, DH_B, t), (0, 4, 1, 2, 3))[None]
    win_kv_prompt = win32.reshape(b, t, 2, G_B, DH_B)[None, :, t - keep:]

    n_pages = page_table.shape[1]
    past_len = n_pages * PAGE_SIZE
    n_pool = cache_diff_kv.shape[1]
    pps = 32
    pps_nsa = 2 * pps
    assert R_B * G_B * tn == LANES and tn < CMP_BLOCK and n_pages % pps_nsa == 0
    xs = x_sample.reshape(bd * tn, d)
    us = _rmsnorm(xs, norm_attn[l], BF16, bd * tn)
    zq_s, kva_s, _, nsa_s, win_s, gb_s, gm_s = _in_projection(us, w, bd * tn, F32)
    pt = page_table.reshape(-1)
    cache_d = cache_diff_kv[l].reshape(n_pool * PAGE_SIZE * 2 * HKV_A, 2 * DH_A)
    cache_n = jnp.transpose(cache_nsa_kv[l], (0, 2, 3, 4, 1)).reshape(n_pool * 4 * G_B * DH_B, PAGE_SIZE)
    wb = state_win_kv.shape[2]
    state_t = jnp.transpose(state_win_kv[l], (0, 2, 3, 4, 1)).reshape(bd, 2 * G_B * DH_B, wb)
    oa_s = _diff_sample(zq_s, kva_s, cache_d, pt, lambda_qk[l], diff_gain[l], bd, tn, n_pages, pps, lam_init)
    ocmp_s, osel_s = _nsa_sample(zq_s, nsa_s, gb_s, w_cmp[l], cache_n, pt, bd, tn, n_pages, pps_nsa)
    owin_s, new_state_t = _win_sample(zq_s, win_s, state_t, gb_s, bd, tn, past_len)
    hs, u2s = _post_attention(oa_s, ocmp_s, osel_s, owin_s, gm_s, xs, wpa, wpb, wo, norm_mlp[l], 256)
    y_sample = _mlp_final(u2s, hs, wup, wdn, norm_final, 512, 512).reshape(bd, tn, d)
    diff_kv_sample = kva_s.reshape(1, bd, tn, 2, HKV_A, 2 * DH_A)
    nsa_kv_sample = nsa_s.reshape(1, bd, tn, 4, G_B, DH_B)
    win_kv_sample = jnp.transpose(new_state_t.reshape(bd, 2, G_B, DH_B, wb), (0, 4, 1, 2, 3))[None]
    return (y_prompt, y_sample, diff_kv_prompt, nsa_kv_prompt, win_kv_prompt,
            diff_kv_sample, nsa_kv_sample, win_kv_sample)
```
